```python
import math
import jax
import jax.numpy as jnp
from jax import lax
import numpy as np

D_MODEL = 1024
BATCH = 8
SEQ = 8192
DEPTH = 1
DEC_BATCH = 128
DEC_SEQ = 4
PAST_LEN = 8192
PAGE_SIZE = 128

HEAD_DIM = 128
HEADS_PER_GROUP = 4
GROUP_WINDOWS = (128, 512, 2048)
GROUP_DILATIONS = (1, 4, 16)
N_GROUPS = 3
N_HEADS_A = N_GROUPS * HEADS_PER_GROUP
ATT_WIDTH = N_HEADS_A * HEAD_DIM
MERGE_WIDTH = HEADS_PER_GROUP * HEAD_DIM
ATT_SCALE = HEAD_DIM ** -0.5
LRU_WIDTH = D_MODEL
LRU_BLOCKS = 16
LRU_BLOCK = LRU_WIDTH // LRU_BLOCKS
CONV_WIDTH = 4
RG_C = 8.0
D_FF = 4 * D_MODEL
PLE_DIM = 256
REL_BUCKETS = 32
REL_MAX_DIST = 2048
NORM_EPS = 1e-6
NEG_INF = -1e30
SPLITS = (ATT_WIDTH, 2 * ATT_WIDTH, 3 * ATT_WIDTH, 3 * ATT_WIDTH + LRU_WIDTH,
          3 * ATT_WIDTH + 2 * LRU_WIDTH, 3 * ATT_WIDTH + 2 * LRU_WIDTH + D_MODEL)
IN_WIDTH = 3 * ATT_WIDTH + 2 * LRU_WIDTH + 2 * D_MODEL

kernel_name = 'dilated_swa_rglru_hybrid_step'


def _rms_norm(x, gain):
    x32 = x.astype(jnp.float32)
    y = x32 * lax.rsqrt(jnp.mean(x32 * x32, axis=-1, keepdims=True) + NORM_EPS)
    return (y * gain.astype(jnp.float32)).astype(x.dtype)


def _softmax_lse(logits):
    m = jnp.max(logits, axis=-1, keepdims=True)
    p = jnp.exp(logits - m)
    s = jnp.sum(p, axis=-1, keepdims=True)
    return p / s, (m + jnp.log(s))[..., 0]


def _t5_bucket(dist):
    max_exact = REL_BUCKETS // 2
    d = jnp.maximum(dist, 1).astype(jnp.float32)
    large = max_exact + (jnp.log(d / max_exact) / math.log(REL_MAX_DIST / max_exact)
                         * (REL_BUCKETS - max_exact)).astype(jnp.int32)
    large = jnp.minimum(large, REL_BUCKETS - 1)
    return jnp.where(dist < max_exact, dist, large)


def _slot_bias(rel_bias, g):
    dil, win = GROUP_DILATIONS[g], GROUP_WINDOWS[g]
    dist = dil * jnp.arange(win // dil + 1, dtype=jnp.int32)
    hs = slice(g * HEADS_PER_GROUP, (g + 1) * HEADS_PER_GROUP)
    return rel_bias[_t5_bucket(dist)][:, hs].astype(jnp.float32)


def _band_dilated_attention(q, k, v, bias, dil, win):
    bsz, seq, nh, hd = q.shape
    qb = win // dil
    span = dil * qb
    s_pad = -(-seq // span) * span
    n_blk = s_pad // span
    m_len = s_pad // dil

    def to_blocks(t):
        t = jnp.pad(t, ((0, 0), (0, s_pad - seq), (0, 0), (0, 0)))
        t = t.reshape(bsz, m_len, dil, nh, hd).transpose(0, 2, 1, 3, 4)
        return t.reshape(bsz, dil, n_blk, qb, nh, hd)

    def with_prev(t):
        prev = jnp.pad(t, ((0, 0), (0, 0), (1, 0), (0, 0), (0, 0), (0, 0)))[:, :, :-1]
        return jnp.concatenate([prev, t], axis=3)

    qs = to_blocks(q)
    ks = with_prev(to_blocks(k))
    vs = with_prev(to_blocks(v))
    iq = jnp.arange(qb)[:, None]
    ik = jnp.arange(2 * qb)[None, :]
    slot = qb + iq - ik
    in_band = (slot >= 0) & (slot <= qb)
    has_key = (jnp.arange(n_blk)[:, None, None] > 0) | (ik >= qb)[None]
    mask = in_band[None, None] & has_key[:, None]
    band_bias = jnp.transpose(bias[jnp.clip(slot, 0, qb)], (2, 0, 1))
    logits = jnp.einsum('brnqhd,brnkhd->brnhqk', qs, ks) * ATT_SCALE + band_bias
    probs, lse = _softmax_lse(jnp.where(mask, logits, NEG_INF))
    o = jnp.einsum('brnhqk,brnkhd->brnqhd', probs, vs)
    o = o.reshape(bsz, dil, m_len, nh, hd).transpose(0, 2, 1, 3, 4).reshape(bsz, s_pad, nh, hd)[:, :seq]
    lse = jnp.swapaxes(lse, 3, 4).reshape(bsz, dil, m_len, nh).transpose(0, 2, 1, 3)
    lse = lse.reshape(bsz, s_pad, nh)[:, :seq]
    return o, lse


def _gather_dilated_attention(q, k_all, v_all, bias, dil, win, n_buf):
    t = q.shape[1]
    nk = win // dil + 1
    idx = n_buf + jnp.arange(t)[:, None] - dil * jnp.arange(nk)[None, :]
    valid = idx >= 0
    idx_c = jnp.maximum(idx, 0)
    kg = k_all[:, idx_c]
    vg = v_all[:, idx_c]
    logits = jnp.einsum('bthd,btkhd->bthk', q, kg) * ATT_SCALE + bias.T
    probs, lse = _softmax_lse(jnp.where(valid[None, :, None, :], logits, NEG_INF))
    o = jnp.einsum('bthk,btkhd->bthd', probs, vg)
    return o, lse


def _combine_groups(outs, lses):
    w = jax.nn.softmax(jnp.stack(lses, 0), axis=0)
    return jnp.sum(w[..., None] * jnp.stack(outs, 0), axis=0)


def _prompt_attend(q, k, v, biases):
    outs, lses, rows = [], [], []
    f32 = jnp.float32
    for g in range(N_GROUPS):
        hs = slice(g * HEADS_PER_GROUP, (g + 1) * HEADS_PER_GROUP)
        qg, kg, vg = q[:, :, hs], k[:, :, hs], v[:, :, hs]
        o, lse = _band_dilated_attention(qg.astype(f32), kg.astype(f32), vg.astype(f32), biases[g],
                                         GROUP_DILATIONS[g], GROUP_WINDOWS[g])
        outs.append(o)
        lses.append(lse)
        keep = min(GROUP_WINDOWS[g], q.shape[1])
        rows.append(jnp.stack([kg, vg], axis=2)[:, -keep:])
    return _combine_groups(outs, lses), rows


def _sample_attend(q, k, v, biases, bufs):
    outs, lses, rows = [], [], []
    f32 = jnp.float32
    for g in range(N_GROUPS):
        hs = slice(g * HEADS_PER_GROUP, (g + 1) * HEADS_PER_GROUP)
        qg, kg, vg = q[:, :, hs], k[:, :, hs], v[:, :, hs]
        buf = bufs[g]
        k_all = jnp.concatenate([buf[:, :, 0].astype(kg.dtype), kg], axis=1)
        v_all = jnp.concatenate([buf[:, :, 1].astype(vg.dtype), vg], axis=1)
        o, lse = _gather_dilated_attention(qg.astype(f32), k_all.astype(f32), v_all.astype(f32), biases[g],
                                           GROUP_DILATIONS[g], GROUP_WINDOWS[g], buf.shape[1])
        outs.append(o)
        lses.append(lse)
        rows.append(jnp.stack([kg, vg], axis=2))
    return _combine_groups(outs, lses), rows


def _lin_combine(left, right):
    a1, b1 = left
    a2, b2 = right
    return a1 * a2, a2 * b1 + b2


def _conv_rglru(xb, conv_prev, h0, w_conv, b_conv, w_rg_a, b_rg_a, w_rg_i, b_rg_i, lam):
    t = xb.shape[1]
    xcat = jnp.concatenate([conv_prev.astype(xb.dtype), xb], axis=1)
    xc = b_conv
    for j in range(CONV_WIDTH):
        xc = xc + w_conv[j] * xcat[:, j:j + t]
    new_conv = xcat[:, t:]
    x32 = xc.astype(jnp.float32)
    xblk = x32.reshape(x32.shape[0], t, LRU_BLOCKS, LRU_BLOCK)
    r = jax.nn.sigmoid(jnp.einsum('btni,nij->btnj', xblk, w_rg_a.astype(jnp.float32)).reshape(x32.shape) + b_rg_a)
    i = jax.nn.sigmoid(jnp.einsum('btni,nij->btnj', xblk, w_rg_i.astype(jnp.float32)).reshape(x32.shape) + b_rg_i)
    log_a = RG_C * r * jax.nn.log_sigmoid(lam.astype(jnp.float32))
    a = jnp.exp(log_a)
    b = jnp.sqrt(-jnp.expm1(2.0 * log_a)) * (i * x32)
    b = b.at[:, 0].add(a[:, 0] * h0)
    _, h = lax.associative_scan(_lin_combine, (a, b), axis=1)
    return h, new_conv, h[:, -1]


def _layer(x, pe, attend, conv_prev, h0, g_mix, w_in, g_q, g_k, w_oa, w_conv, b_conv, w_rg_a, b_rg_a,
           w_rg_i, b_rg_i, lam, w_ob, w_o, g_mlp, w_up, w_down, g_ple, w_ple_gate, w_ple_in):
    bsz, t = x.shape[0], x.shape[1]
    h = _rms_norm(x, g_mix)
    q, k, v, xb, yb, ga, gb = jnp.split(h @ w_in, SPLITS, axis=-1)
    q = _rms_norm(q.reshape(bsz, t, N_HEADS_A, HEAD_DIM), g_q)
    k = _rms_norm(k.reshape(bsz, t, N_HEADS_A, HEAD_DIM), g_k)
    v = v.reshape(bsz, t, N_HEADS_A, HEAD_DIM)
    att, kv_rows = attend(q, k, v)
    a_out = att.reshape(bsz, t, MERGE_WIDTH).astype(x.dtype) @ w_oa
    hr, conv_new, h_last = _conv_rglru(xb, conv_prev, h0, w_conv, b_conv, w_rg_a, b_rg_a, w_rg_i, b_rg_i, lam)
    b_out = (hr.astype(x.dtype) * jax.nn.gelu(yb)) @ w_ob
    x = x + (jax.nn.sigmoid(ga) * a_out + jax.nn.sigmoid(gb) * b_out) @ w_o
    x = x + jnp.square(jax.nn.relu(_rms_norm(x, g_mlp) @ w_up)) @ w_down
    x = x + jax.nn.sigmoid(_rms_norm(x, g_ple) @ w_ple_gate) * (pe @ w_ple_in)
    return x, kv_rows, conv_new, h_last


def setup_inputs(seed: int = 0) -> dict:
    key = jax.random.key(seed)
    k = jax.random.split(key, 32)
    f32 = jnp.float32

    def nrm(i, shape, scale):
        return scale * jax.random.normal(k[i], shape, f32)

    wb = [min(w, PAST_LEN) for w in GROUP_WINDOWS]
    u = jax.random.uniform(k[31], (DEPTH, LRU_WIDTH), f32, 0.9, 0.999)
    return {
        'x_prompt': nrm(0, (BATCH, SEQ, D_MODEL), 1.0),
        'x_sample': nrm(1, (DEC_BATCH, DEC_SEQ, D_MODEL), 1.0),
        'p_prompt': nrm(2, (DEPTH, BATCH, SEQ, PLE_DIM), 1.0),
        'p_sample': nrm(3, (DEPTH, DEC_BATCH, DEC_SEQ, PLE_DIM), 1.0),
        'cache_kv1': nrm(4, (DEPTH, DEC_BATCH, wb[0], 2, HEADS_PER_GROUP, HEAD_DIM), 1.0),
        'cache_kv2': nrm(5, (DEPTH, DEC_BATCH, wb[1], 2, HEADS_PER_GROUP, HEAD_DIM), 1.0),
        'cache_kv3': nrm(6, (DEPTH, DEC_BATCH, wb[2], 2, HEADS_PER_GROUP, HEAD_DIM), 1.0),
        'state_conv': nrm(7, (DEPTH, DEC_BATCH, CONV_WIDTH - 1, LRU_WIDTH), 1.0),
        'state_lru': nrm(8, (DEPTH, DEC_BATCH, LRU_WIDTH), 0.5),
        'rel_bias': nrm(9, (REL_BUCKETS, N_HEADS_A), 0.5),
        'g_mix': 1.0 + nrm(10, (DEPTH, D_MODEL), 0.05),
        'w_in': nrm(11, (DEPTH, D_MODEL, IN_WIDTH), D_MODEL ** -0.5),
        'g_q': 1.0 + nrm(12, (DEPTH, HEAD_DIM), 0.05),
        'g_k': 1.0 + nrm(13, (DEPTH, HEAD_DIM), 0.05),
        'w_oa': nrm(14, (DEPTH, MERGE_WIDTH, D_MODEL), MERGE_WIDTH ** -0.5),
        'w_conv': nrm(15, (DEPTH, CONV_WIDTH, LRU_WIDTH), CONV_WIDTH ** -0.5),
        'b_conv': nrm(16, (DEPTH, LRU_WIDTH), 0.05),
        'w_rg_a': nrm(17, (DEPTH, LRU_BLOCKS, LRU_BLOCK, LRU_BLOCK), LRU_BLOCK ** -0.5),
        'b_rg_a': nrm(18, (DEPTH, LRU_WIDTH), 0.1),
        'w_rg_i': nrm(19, (DEPTH, LRU_BLOCKS, LRU_BLOCK, LRU_BLOCK), LRU_BLOCK ** -0.5),
        'b_rg_i': nrm(20, (DEPTH, LRU_WIDTH), 0.1),
        'lam': jnp.log(u) - jnp.log1p(-u),
        'w_ob': nrm(21, (DEPTH, LRU_WIDTH, D_MODEL), LRU_WIDTH ** -0.5),
        'w_o': nrm(22, (DEPTH, D_MODEL, D_MODEL), D_MODEL ** -0.5),
        'g_mlp': 1.0 + nrm(23, (DEPTH, D_MODEL), 0.05),
        'w_up': nrm(24, (DEPTH, D_MODEL, D_FF), D_MODEL ** -0.5),
        'w_down': nrm(25, (DEPTH, D_FF, D_MODEL), D_FF ** -0.5),
        'g_ple': 1.0 + nrm(26, (DEPTH, D_MODEL), 0.05),
        'w_ple_gate': nrm(27, (DEPTH, D_MODEL, D_MODEL), D_MODEL ** -0.5),
        'w_ple_in': nrm(28, (DEPTH, PLE_DIM, D_MODEL), PLE_DIM ** -0.5),
    }


def reference(x_prompt, x_sample, p_prompt, p_sample, cache_kv1, cache_kv2, cache_kv3, state_conv, state_lru,
              rel_bias, g_mix, w_in, g_q, g_k, w_oa, w_conv, b_conv, w_rg_a, b_rg_a, w_rg_i, b_rg_i, lam,
              w_ob, w_o, g_mlp, w_up, w_down, g_ple, w_ple_gate, w_ple_in):
    biases = [_slot_bias(rel_bias, g) for g in range(N_GROUPS)]
    yp, ys = x_prompt, x_sample
    kvp = [[], [], []]
    kvs = [[], [], []]
    convp, lrup, convs, lrus = [], [], [], []
    for i in range(DEPTH):
        lp = (g_mix[i], w_in[i], g_q[i], g_k[i], w_oa[i], w_conv[i], b_conv[i], w_rg_a[i], b_rg_a[i],
              w_rg_i[i], b_rg_i[i], lam[i], w_ob[i], w_o[i], g_mlp[i], w_up[i], w_down[i], g_ple[i],
              w_ple_gate[i], w_ple_in[i])
        conv0 = jnp.zeros((yp.shape[0], CONV_WIDTH - 1, LRU_WIDTH), yp.dtype)
        h00 = jnp.zeros((yp.shape[0], LRU_WIDTH), jnp.float32)
        yp, rows_p, c_p, h_p = _layer(yp, p_prompt[i], lambda q, k, v: _prompt_attend(q, k, v, biases),
                                      conv0, h00, *lp)
        bufs = (cache_kv1[i], cache_kv2[i], cache_kv3[i])
        ys, rows_s, c_s, h_s = _layer(ys, p_sample[i], lambda q, k, v: _sample_attend(q, k, v, biases, bufs),
                                      state_conv[i], state_lru[i].astype(jnp.float32), *lp)
        for g in range(N_GROUPS):
            kvp[g].append(rows_p[g])
            kvs[g].append(rows_s[g])
        convp.append(c_p)
        lrup.append(h_p)
        convs.append(c_s)
        lrus.append(h_s)
    kv1_p, kv2_p, kv3_p = jnp.stack(kvp[0]), jnp.stack(kvp[1]), jnp.stack(kvp[2])
    kv1_s, kv2_s, kv3_s = jnp.stack(kvs[0]), jnp.stack(kvs[1]), jnp.stack(kvs[2])
    conv_p, lru_p = jnp.stack(convp), jnp.stack(lrup)
    conv_s, lru_s = jnp.stack(convs), jnp.stack(lrus)
    return (yp, ys, kv1_p, kv2_p, kv3_p, conv_p, lru_p, kv1_s, kv2_s, kv3_s, conv_s, lru_s)
```

```python
import functools
import math

import jax
import jax.numpy as jnp
from jax import lax
from jax.experimental import pallas as pl
from jax.experimental.pallas import tpu as pltpu

F32 = jnp.float32
BF16 = jnp.bfloat16

D_MODEL = 1024
HEAD_DIM = 128
HEADS_PER_GROUP = 4
GROUP_WINDOWS = (128, 512, 2048)
GROUP_DILATIONS = (1, 4, 16)
N_GROUPS = 3
GROUP_WIDTH = HEADS_PER_GROUP * HEAD_DIM
ATT_WIDTH = N_GROUPS * GROUP_WIDTH
ATT_SCALE = HEAD_DIM ** -0.5
LRU_WIDTH = D_MODEL
LRU_BLOCKS = 16
LRU_BLOCK = LRU_WIDTH // LRU_BLOCKS
CONV_WIDTH = 4
RG_C = 8.0
D_FF = 4 * D_MODEL
PLE_DIM = 256
REL_BUCKETS = 32
REL_MAX_DIST = 2048
NORM_EPS = 1e-6
NEG_INF = -1e30
WINDOW_SLOTS = 128
MAX_DIL = 16

OFF_Q, OFF_K, OFF_V = 0, ATT_WIDTH, 2 * ATT_WIDTH
OFF_XB = 3 * ATT_WIDTH
OFF_YB = OFF_XB + LRU_WIDTH
OFF_GA = OFF_YB + LRU_WIDTH
OFF_GB = OFF_GA + D_MODEL

ROW_TILE = 512
Q_BLOCK = 128
GATE_TILE = 256
VMEM_LIMIT = 58 * 1024 * 1024


def _cparams(n_axes):
    return pltpu.CompilerParams(dimension_semantics=("arbitrary",) * n_axes,
                                vmem_limit_bytes=VMEM_LIMIT)


def _resident(shape):
    nd = len(shape)
    return pl.BlockSpec(shape, lambda *_: (0,) * nd, pipeline_mode=pl.Buffered(1))


def _rms(x, gain):
    return x * lax.rsqrt(jnp.mean(x * x, axis=-1, keepdims=True) + NORM_EPS) * gain


def _dot(a, b):
    return jnp.dot(a, b, preferred_element_type=F32)


def _dot_nt(a, b):
    return lax.dot_general(a, b, (((1,), (1,)), ((), ())), preferred_element_type=F32)


def _in_proj_kernel(x_ref, gmix_ref, w_ref, gq_ref, gk_ref,
                    q1, q2, q3, k1, k2, k3, v1, v2, v3, xb_ref, gy_ref, sga_ref, sgb_ref,
                    kv1, kv2, kv3, xbt_ref):
    q_refs, k_refs, v_refs, kv_refs = (q1, q2, q3), (k1, k2, k3), (v1, v2, v3), (kv1, kv2, kv3)
    h = _rms(x_ref[...], gmix_ref[...]).astype(BF16)
    gq = gq_ref[...] * ATT_SCALE
    gk = gk_ref[...]
    for g in range(N_GROUPS):
        n_tail = kv_refs[g].shape[0]
        c0 = g * GROUP_WIDTH
        yq = _dot(h, w_ref[:, OFF_Q + c0:OFF_Q + c0 + GROUP_WIDTH])
        yk = _dot(h, w_ref[:, OFF_K + c0:OFF_K + c0 + GROUP_WIDTH])
        yv = _dot(h, w_ref[:, OFF_V + c0:OFF_V + c0 + GROUP_WIDTH])
        for hd in range(HEADS_PER_GROUP):
            cs = slice(hd * HEAD_DIM, (hd + 1) * HEAD_DIM)
            q_refs[g][:, cs] = _rms(yq[:, cs], gq).astype(BF16)
            kn = _rms(yk[:, cs], gk)
            k_refs[g][:, cs] = kn.astype(BF16)
            kv_refs[g][:, cs] = kn[kn.shape[0] - n_tail:, :]
        v_refs[g][...] = yv.astype(BF16)
        kv_refs[g][:, GROUP_WIDTH:] = yv[yv.shape[0] - n_tail:, :]
    y = _dot(h, w_ref[:, OFF_XB:OFF_XB + LRU_WIDTH])
    xb_ref[...] = y.astype(BF16)
    xbt_ref[...] = y[y.shape[0] - xbt_ref.shape[0]:, :]
    gy_ref[...] = jax.nn.gelu(_dot(h, w_ref[:, OFF_YB:OFF_YB + LRU_WIDTH])).astype(BF16)
    sga_ref[...] = jax.nn.sigmoid(_dot(h, w_ref[:, OFF_GA:OFF_GA + D_MODEL])).astype(BF16)
    sgb_ref[...] = jax.nn.sigmoid(_dot(h, w_ref[:, OFF_GB:OFF_GB + D_MODEL])).astype(BF16)


def _in_proj_prompt(x, g_mix, w_in, g_q, g_k):
    bsz, seq, _ = x.shape
    n_m = seq // MAX_DIL
    assert n_m == ROW_TILE
    tails = tuple(w // MAX_DIL for w in GROUP_WINDOWS)
    gw = GROUP_WIDTH

    def strided(width):
        return pl.BlockSpec((None, n_m, width), lambda b, r: (b, 0, r))

    spec_g2 = pl.BlockSpec((None, None, n_m, gw), lambda b, r: (b, r % 4, 0, r // 4))
    spec_g3 = pl.BlockSpec((None, None, n_m, gw), lambda b, r: (b, r, 0, 0))
    spec_tb = pl.BlockSpec((n_m, LRU_WIDTH), lambda b, r: (0, r * bsz + b))
    spec_tile = pl.BlockSpec((None, None, n_m, D_MODEL), lambda b, r: (b, r, 0, 0))

    def sds(shape, dt):
        return jax.ShapeDtypeStruct(shape, dt)

    qkv_shapes = [sds((bsz, n_m, MAX_DIL * gw), BF16), sds((bsz, 4, n_m, 4 * gw), BF16),
                  sds((bsz, MAX_DIL, n_m, gw), BF16)]
    qkv_specs = [strided(gw), spec_g2, spec_g3]
    out_shape = (qkv_shapes * 3
                 + [sds((n_m, MAX_DIL * bsz * LRU_WIDTH), BF16)] * 2
                 + [sds((bsz, MAX_DIL, n_m, D_MODEL), BF16)] * 2
                 + [sds((bsz, t, MAX_DIL * 2 * gw), F32) for t in tails]
                 + [sds((bsz, 8, MAX_DIL * LRU_WIDTH), F32)])
    out_specs = (qkv_specs * 3 + [spec_tb] * 2 + [spec_tile] * 2
                 + [pl.BlockSpec((None, t, 2 * gw), lambda b, r: (b, 0, r)) for t in tails]
                 + [pl.BlockSpec((None, 8, LRU_WIDTH), lambda b, r: (b, 0, r))])
    return pl.pallas_call(
        _in_proj_kernel,
        grid=(bsz, MAX_DIL),
        in_specs=[strided(D_MODEL), _resident(g_mix.shape), _resident(w_in.shape),
                  _resident(g_q.shape), _resident(g_k.shape)],
        out_specs=out_specs,
        out_shape=out_shape,
        compiler_params=_cparams(2),
        name="in_proj_prompt",
    )(x.reshape(bsz, n_m, MAX_DIL * D_MODEL), g_mix, w_in, g_q, g_k)


def _in_proj_sample(x, g_mix, w_in, g_q, g_k):
    n = x.shape[0]
    tm = 128
    gw = GROUP_WIDTH

    def rows(width):
        return pl.BlockSpec((tm, width), lambda i: (i, 0))

    def sds(width, dt):
        return jax.ShapeDtypeStruct((n, width), dt)

    out_shape = ([sds(gw, BF16)] * 9 + [sds(LRU_WIDTH, BF16)] * 2 + [sds(D_MODEL, BF16)] * 2
                 + [sds(2 * gw, F32)] * 3 + [sds(LRU_WIDTH, F32)])
    out_specs = ([rows(gw)] * 9 + [rows(LRU_WIDTH)] * 2 + [rows(D_MODEL)] * 2
                 + [rows(2 * gw)] * 3 + [rows(LRU_WIDTH)])
    return pl.pallas_call(
        _in_proj_kernel,
        grid=(n // tm,),
        in_specs=[rows(D_MODEL), _resident(g_mix.shape), _resident(w_in.shape),
                  _resident(g_q.shape), _resident(g_k.shape)],
        out_specs=out_specs,
        out_shape=out_shape,
        compiler_params=_cparams(1),
        name="in_proj_sample",
    )(x, g_mix, w_in, g_q, g_k)


def _swa_kernel(q_ref, k_ref, v_ref, kp_ref, vp_ref, bias_ref, o_ref, lse_ref):
    first = pl.program_id(1) == 0
    n_blk = q_ref.shape[0] // Q_BLOCK
    col = lax.broadcasted_iota(jnp.int32, (Q_BLOCK, 2 * Q_BLOCK), 1)
    lane = lax.broadcasted_iota(jnp.int32, (Q_BLOCK, HEAD_DIM), 1)
    for j in range(n_blk):
        rq = slice(j * Q_BLOCK, (j + 1) * Q_BLOCK)
        lse_tile = jnp.zeros((Q_BLOCK, HEAD_DIM), F32)
        for hd in range(HEADS_PER_GROUP):
            cs = slice(hd * HEAD_DIM, (hd + 1) * HEAD_DIM)
            q = q_ref[rq, cs]
            if j == 0:
                kk = jnp.concatenate([kp_ref[:, cs], k_ref[rq, cs]], axis=0)
                vv = jnp.concatenate([vp_ref[:, cs], v_ref[rq, cs]], axis=0)
            else:
                rk = slice((j - 1) * Q_BLOCK, (j + 1) * Q_BLOCK)
                kk = k_ref[rk, cs]
                vv = v_ref[rk, cs]
            s = _dot_nt(q, kk) + bias_ref[hd]
            if j == 0:
                s = jnp.where(col < jnp.where(first, Q_BLOCK, 0), NEG_INF, s)
            m = jnp.max(s, axis=-1, keepdims=True)
            p = jnp.exp(s - m)
            den = jnp.sum(p, axis=-1, keepdims=True)
            o = _dot(p.astype(BF16), vv) / den
            o_ref[rq, cs] = o.astype(BF16)
            lse_tile = jnp.where(lane == hd, m + jnp.log(den), lse_tile)
        lse_ref[rq, :] = lse_tile


def _swa(q, k, v, band_bias):
    n_seq, length, gw = q.shape
    tq = ROW_TILE
    ratio = tq // Q_BLOCK
    cur = pl.BlockSpec((None, tq, gw), lambda s, i: (s, i, 0))
    prev = pl.BlockSpec((None, Q_BLOCK, gw), lambda s, i: (s, jnp.maximum(i * ratio - 1, 0), 0))
    return pl.pallas_call(
        _swa_kernel,
        grid=(n_seq, length // tq),
        in_specs=[cur, cur, cur, prev, prev, _resident(band_bias.shape)],
        out_specs=[cur, pl.BlockSpec((None, tq, HEAD_DIM), lambda s, i: (s, i, 0))],
        out_shape=[jax.ShapeDtypeStruct((n_seq, length, gw), BF16),
                   jax.ShapeDtypeStruct((n_seq, length, HEAD_DIM), F32)],
        compiler_params=_cparams(2),
        name="swa",
    )(q, k, v, k, v, band_bias)


DEC_BATCH_TILE = 2
NEW_ROWS_PAD = 16


def _dec_attn_kernel(q1, q2, q3, c1, c2, c3, n1, n2, n3, bc_ref, bn_ref, e_ref, et_ref, att_ref):
    q_refs, c_refs, n_refs = (q1, q2, q3), (c1, c2, c3), (n1, n2, n3)
    gw = GROUP_WIDTH
    kvw = 2 * gw
    n_t = q1.shape[1]
    e = e_ref[...]
    et = et_ref[...]
    for bi in range(q1.shape[0]):
        for t in range(n_t):
            probs, lses, dens = [], [], []
            for g in range(N_GROUPS):
                c0 = 0 if g == 0 else t * kvw
                qg = q_refs[g][bi, t:t + 1, :]
                kc = c_refs[g][bi, :, c0:c0 + gw]
                kn = n_refs[g][bi, :, 0:gw]
                lc = _dot((kc * qg).astype(BF16), e) + bc_ref[g, t]
                ln = _dot((kn * qg).astype(BF16), e) + bn_ref[g, t]
                m = jnp.maximum(jnp.max(lc, axis=0, keepdims=True), jnp.max(ln, axis=0, keepdims=True))
                pc = jnp.exp(lc - m)
                pn = jnp.exp(ln - m)
                den = jnp.sum(pc, axis=0, keepdims=True) + jnp.sum(pn, axis=0, keepdims=True)
                probs.append((pc, pn))
                dens.append(den)
                lses.append(m + jnp.log(den))
            top = jnp.maximum(jnp.maximum(lses[0], lses[1]), lses[2])
            ws = [jnp.exp(l - top) for l in lses]
            tot = ws[0] + ws[1] + ws[2]
            acc = jnp.zeros((1, gw), F32)
            for g in range(N_GROUPS):
                c0 = 0 if g == 0 else t * kvw
                coef = ws[g] / tot / dens[g]
                pc, pn = probs[g]
                vc = c_refs[g][bi, :, c0 + gw:c0 + kvw]
                vn = n_refs[g][bi, :, gw:kvw]
                acc = acc + jnp.sum(_dot((pc * coef).astype(BF16), et) * vc, axis=0, keepdims=True)
                acc = acc + jnp.sum(_dot((pn * coef).astype(BF16), et) * vn, axis=0, keepdims=True)
            att_ref[bi, t:t + 1, :] = acc


def _dec_attn(qs, caches, news, bias_cache, bias_new):
    bd, n_t, gw = qs[0].shape
    bb = DEC_BATCH_TILE
    kvw = 2 * gw
    head_of_col = jnp.arange(gw, dtype=jnp.int32) // HEAD_DIM
    e = (head_of_col[:, None] == jnp.arange(HEAD_DIM, dtype=jnp.int32)[None, :]).astype(BF16)
    et = e.T
    cviews = [caches[0]]
    cspecs = [pl.BlockSpec((bb, WINDOW_SLOTS, kvw), lambda i: (i, 0, 0))]
    for g in (1, 2):
        dil = GROUP_DILATIONS[g]
        cviews.append(caches[g].reshape(bd, WINDOW_SLOTS, dil * kvw))
        cspecs.append(pl.BlockSpec((bb, WINDOW_SLOTS, n_t * kvw), lambda i: (i, 0, 0)))
    qspec = pl.BlockSpec((bb, n_t, gw), lambda i: (i, 0, 0))
    nspec = pl.BlockSpec((bb, NEW_ROWS_PAD, kvw), lambda i: (i, 0, 0))
    return pl.pallas_call(
        _dec_attn_kernel,
        grid=(bd // bb,),
        in_specs=[qspec] * 3 + cspecs + [nspec] * 3
        + [_resident(bias_cache.shape), _resident(bias_new.shape), _resident(e.shape), _resident(et.shape)],
        out_specs=qspec,
        out_shape=jax.ShapeDtypeStruct((bd, n_t, gw), F32),
        compiler_params=_cparams(1),
        name="dec_attn",
    )(*qs, *cviews, *news, bias_cache, bias_new, e, et)


def _lru_kernel(xb_ref, gy_ref, conv0_ref, h0_ref, wconv_ref, bconv_ref, wa_ref, ba_ref, wi_ref, bi_ref,
                lam_ref, wob_ref, out_ref, hlast_ref, xcat, a_s, b_s, hc, *, nb, tl):
    rows = nb * tl
    halo = (CONV_WIDTH - 1) * nb

    @pl.when(pl.program_id(0) == 0)
    def _():
        xcat[0:halo, :] = conv0_ref[...]
        hc[...] = h0_ref[...]

    xcat[halo:halo + rows, :] = xb_ref[...].astype(F32)
    xc = bconv_ref[...] + wconv_ref[0:1, :] * xcat[0:rows, :]
    for j in range(1, CONV_WIDTH):
        xc = xc + wconv_ref[j:j + 1, :] * xcat[j * nb:j * nb + rows, :]
    xcat[0:halo, :] = xcat[rows:rows + halo, :]
    xcb = xc.astype(BF16)
    for blk in range(LRU_WIDTH // GATE_TILE):
        cs = slice(blk * GATE_TILE, (blk + 1) * GATE_TILE)
        r = jax.nn.sigmoid(_dot(xcb[:, cs], wa_ref[blk]) + ba_ref[:, cs])
        ig = jax.nn.sigmoid(_dot(xcb[:, cs], wi_ref[blk]) + bi_ref[:, cs])
        lam = lam_ref[:, cs]
        log_sig = jnp.minimum(lam, 0.0) - jnp.log1p(jnp.exp(-jnp.abs(lam)))
        log_a = RG_C * r * log_sig
        a = jnp.exp(log_a)
        a_s[:, cs] = a
        b_s[:, cs] = jnp.sqrt(-jnp.tanh(log_a) * (1.0 + a * a)) * (ig * xc[:, cs])

    if tl <= 8:
        h = hc[...]
        for t in range(tl):
            rs = slice(t * nb, (t + 1) * nb)
            h = a_s[rs, :] * h + b_s[rs, :]
            b_s[rs, :] = h
    else:
        def step(t, h):
            rs = pl.ds(pl.multiple_of(t * nb, nb), nb)
            h = a_s[rs, :] * h + b_s[rs, :]
            b_s[rs, :] = h
            return h
        h = lax.fori_loop(0, tl, step, hc[...], unroll=8)
    hc[...] = h
    hlast_ref[...] = h
    out_ref[...] = _dot((b_s[...] * gy_ref[...].astype(F32)).astype(BF16), wob_ref[...]).astype(BF16)


def _lru(xb, gy, conv0, h0, w_conv, b_conv, wa, ba, wi, bi, lam, w_ob, *, nb, tl):
    n_rows, c = xb.shape
    rows = nb * tl
    tile = pl.BlockSpec((rows, c), lambda i: (i, 0))
    consts = [conv0, h0, w_conv, b_conv, wa, ba, wi, bi, lam, w_ob]
    return pl.pallas_call(
        functools.partial(_lru_kernel, nb=nb, tl=tl),
        grid=(n_rows // rows,),
        in_specs=[tile, tile] + [_resident(a.shape) for a in consts],
        out_specs=[tile, pl.BlockSpec((nb, c), lambda i: (0, 0))],
        out_shape=[jax.ShapeDtypeStruct((n_rows, D_MODEL), BF16), jax.ShapeDtypeStruct((nb, c), F32)],
        scratch_shapes=[pltpu.VMEM((rows + (CONV_WIDTH - 1) * nb, c), F32), pltpu.VMEM((rows, c), F32),
                        pltpu.VMEM((rows, c), F32), pltpu.VMEM((nb, c), F32)],
        compiler_params=_cparams(1),
        name="lru",
    )(xb, gy, *consts)


FF_CHUNK = 1024


def _mix_mlp_kernel(x_ref, o1, o2, o3, l1, l2, l3, bout_ref, sga_ref, sgb_ref, pe_ref,
                    woa_ref, wo_ref, gmlp_ref, wup_ref, wdown_ref, gple_ref, wpg_ref, wpe_ref, y_ref):
    o_refs = (o1, o2, o3)
    lses = [l[...] for l in (l1, l2, l3)]
    top = jnp.maximum(jnp.maximum(lses[0], lses[1]), lses[2])
    ws = [jnp.exp(l - top) for l in lses]
    tot = ws[0] + ws[1] + ws[2]
    ws = [w / tot for w in ws]
    heads = []
    for hd in range(HEADS_PER_GROUP):
        cs = slice(hd * HEAD_DIM, (hd + 1) * HEAD_DIM)
        acc = ws[0][:, hd:hd + 1] * o_refs[0][:, cs].astype(F32)
        for g in (1, 2):
            acc = acc + ws[g][:, hd:hd + 1] * o_refs[g][:, cs].astype(F32)
        heads.append(acc.astype(BF16))
    att = jnp.concatenate(heads, axis=1)
    a_out = _dot(att, woa_ref[...])
    mix = sga_ref[...].astype(F32) * a_out + sgb_ref[...].astype(F32) * bout_ref[...].astype(F32)
    x = x_ref[...] + _dot(mix.astype(BF16), wo_ref[...])
    h = _rms(x, gmlp_ref[...]).astype(BF16)
    acc = jnp.zeros(x.shape, F32)
    for c in range(D_FF // FF_CHUNK):
        cs = slice(c * FF_CHUNK, (c + 1) * FF_CHUNK)
        u = jnp.square(jnp.maximum(_dot(h, wup_ref[:, cs]), 0.0))
        acc = acc + _dot(u.astype(BF16), wdown_ref[cs, :])
    x = x + acc
    h = _rms(x, gple_ref[...]).astype(BF16)
    gate = jax.nn.sigmoid(_dot(h, wpg_ref[...]))
    y_ref[...] = x + gate * _dot(pe_ref[...].astype(BF16), wpe_ref[...])


def _mix_mlp_weights(w_oa, w_o, g_mlp, w_up, w_down, g_ple, w_pg, w_pe):
    ws = [w_oa, w_o, g_mlp, w_up, w_down, g_ple, w_pg, w_pe]
    return ws, [_resident(w.shape) for w in ws]


def _mix_mlp_prompt(x, os_, ls, bout, sga, sgb, pe, weights):
    bsz, seq, _ = x.shape
    n_m = seq // MAX_DIL
    gw = GROUP_WIDTH

    def strided(width):
        return pl.BlockSpec((None, n_m, width), lambda b, r: (b, 0, r))

    def g2(width):
        return pl.BlockSpec((None, None, n_m, width), lambda b, r: (b, r % 4, 0, r // 4))

    def g3(width):
        return pl.BlockSpec((None, None, n_m, width), lambda b, r: (b, r, 0, 0))

    ws, wspecs = _mix_mlp_weights(*weights)
    args = [x.reshape(bsz, n_m, MAX_DIL * D_MODEL),
            os_[0].reshape(bsz, n_m, MAX_DIL * gw), os_[1].reshape(bsz, 4, n_m, 4 * gw), os_[2],
            ls[0].reshape(bsz, n_m, MAX_DIL * HEAD_DIM), ls[1].reshape(bsz, 4, n_m, 4 * HEAD_DIM), ls[2],
            bout.reshape(n_m, MAX_DIL * bsz * D_MODEL), sga, sgb,
            pe.reshape(bsz, n_m, MAX_DIL * PLE_DIM)]
    specs = [strided(D_MODEL), strided(gw), g2(gw), g3(gw), strided(HEAD_DIM), g2(HEAD_DIM), g3(HEAD_DIM),
             pl.BlockSpec((n_m, D_MODEL), lambda b, r: (0, r * bsz + b)), g3(D_MODEL), g3(D_MODEL),
             strided(PLE_DIM)]
    y = pl.pallas_call(
        _mix_mlp_kernel,
        grid=(bsz, MAX_DIL),
        in_specs=specs + wspecs,
        out_specs=strided(D_MODEL),
        out_shape=jax.ShapeDtypeStruct((bsz, n_m, MAX_DIL * D_MODEL), F32),
        compiler_params=_cparams(2),
        name="mix_mlp_prompt",
    )(*args, *ws)
    return y.reshape(bsz, seq, D_MODEL)


def _mix_mlp_sample(x, os_, ls, bout, sga, sgb, pe, weights):
    n = x.shape[0]
    tm = 128

    def rows(width):
        return pl.BlockSpec((tm, width), lambda i: (i, 0))

    ws, wspecs = _mix_mlp_weights(*weights)
    gw = GROUP_WIDTH
    specs = [rows(D_MODEL)] + [rows(gw)] * 3 + [rows(HEAD_DIM)] * 3 + [rows(D_MODEL)] * 3 + [rows(PLE_DIM)]
    return pl.pallas_call(
        _mix_mlp_kernel,
        grid=(n // tm,),
        in_specs=specs + wspecs,
        out_specs=rows(D_MODEL),
        out_shape=jax.ShapeDtypeStruct((n, D_MODEL), F32),
        compiler_params=_cparams(1),
        name="mix_mlp_sample",
    )(x, *os_, *ls, bout, sga, sgb, pe, *ws)


def _t5_bucket(dist):
    max_exact = REL_BUCKETS // 2
    d = jnp.maximum(dist, 1).astype(F32)
    large = max_exact + (jnp.log(d / max_exact) / math.log(REL_MAX_DIST / max_exact)
                         * (REL_BUCKETS - max_exact)).astype(jnp.int32)
    large = jnp.minimum(large, REL_BUCKETS - 1)
    return jnp.where(dist < max_exact, dist, large)


def _slot_bias(rel_bias, g):
    dil = GROUP_DILATIONS[g]
    dist = dil * jnp.arange(WINDOW_SLOTS + 1, dtype=jnp.int32)
    hs = slice(g * HEADS_PER_GROUP, (g + 1) * HEADS_PER_GROUP)
    return rel_bias[_t5_bucket(dist)][:, hs].astype(F32)


def _band_bias(slot_bias):
    iq = jnp.arange(Q_BLOCK)[:, None]
    ik = jnp.arange(2 * Q_BLOCK)[None, :]
    slot = Q_BLOCK + iq - ik
    ok = (slot >= 0) & (slot <= WINDOW_SLOTS)
    band = jnp.transpose(slot_bias[jnp.clip(slot, 0, WINDOW_SLOTS)], (2, 0, 1))
    return jnp.where(ok[None], band, NEG_INF)


def _pad_lanes(tbl):
    return jnp.pad(tbl, [(0, 0)] * (tbl.ndim - 1) + [(0, HEAD_DIM - tbl.shape[-1])])


def _dec_bias(slot_biases, n_t):
    c = jnp.arange(WINDOW_SLOTS)[None, :]
    t = jnp.arange(n_t)[:, None]
    tn = jnp.arange(NEW_ROWS_PAD)[None, :]
    bc, bn = [], []
    for g in range(N_GROUPS):
        sb = slot_biases[g]
        if GROUP_DILATIONS[g] == 1:
            jc = WINDOW_SLOTS + t - c
            okc = jc <= WINDOW_SLOTS
            jn = t - tn
            okn = (jn >= 0) & (tn < n_t)
        else:
            jc = jnp.broadcast_to(WINDOW_SLOTS - c, (n_t, WINDOW_SLOTS))
            okc = jnp.ones_like(jc, dtype=bool)
            jn = jnp.zeros((n_t, NEW_ROWS_PAD), jnp.int32)
            okn = tn == t
        bc.append(_pad_lanes(jnp.where(okc[..., None], sb[jnp.clip(jc, 0, WINDOW_SLOTS)], NEG_INF)))
        bn.append(_pad_lanes(jnp.where(okn[..., None], sb[jnp.clip(jn, 0, WINDOW_SLOTS)], NEG_INF)))
    return jnp.stack(bc), jnp.stack(bn)


def _gate_tiles(w):
    per = GATE_TILE // LRU_BLOCK
    w = w.reshape(LRU_WIDTH // GATE_TILE, per, LRU_BLOCK, LRU_BLOCK)
    eye = jnp.eye(per, dtype=w.dtype)
    return jnp.einsum('npij,pq->npiqj', w, eye).reshape(-1, GATE_TILE, GATE_TILE).astype(BF16)


def kernel(x_prompt, x_sample, p_prompt, p_sample, cache_kv1, cache_kv2, cache_kv3, state_conv, state_lru,
           rel_bias, g_mix, w_in, g_q, g_k, w_oa, w_conv, b_conv, w_rg_a, b_rg_a, w_rg_i, b_rg_i, lam,
           w_ob, w_o, g_mlp, w_up, w_down, g_ple, w_ple_gate, w_ple_in):
    depth = w_in.shape[0]
    assert depth == 1
    bsz, seq, _ = x_prompt.shape
    bd, n_t, _ = x_sample.shape
    gw = GROUP_WIDTH
    row = lambda a: a.reshape(1, -1)

    slot_biases = [_slot_bias(rel_bias, g) for g in range(N_GROUPS)]
    band = [_band_bias(sb) for sb in slot_biases]
    bias_cache, bias_new = _dec_bias(slot_biases, n_t)

    i = 0
    w_in_b = w_in[i].astype(BF16)
    gm, gq, gk = row(g_mix[i]), row(g_q[i]), row(g_k[i])
    lru_w = (w_conv[i], row(b_conv[i]), _gate_tiles(w_rg_a[i]), row(b_rg_a[i]), _gate_tiles(w_rg_i[i]),
             row(b_rg_i[i]), row(lam[i]), w_ob[i].astype(BF16))
    mlp_w = (w_oa[i].astype(BF16), w_o[i].astype(BF16), row(g_mlp[i]), w_up[i].astype(BF16),
             w_down[i].astype(BF16), row(g_ple[i]), w_ple_gate[i].astype(BF16), w_ple_in[i].astype(BF16))

    (q1, q2, q3, k1, k2, k3, v1, v2, v3, xb, gy, sga, sgb, kvt1, kvt2, kvt3, xbt) = _in_proj_prompt(
        x_prompt, gm, w_in_b, gq, gk)
    seqs = [(bsz, seq), (bsz * 4, seq // 4), (bsz * MAX_DIL, seq // MAX_DIL)]
    os_, ls = [], []
    for g, (qq, kk, vv) in enumerate(((q1, k1, v1), (q2, k2, v2), (q3, k3, v3))):
        ns, ln = seqs[g]
        o, l = _swa(qq.reshape(ns, ln, gw), kk.reshape(ns, ln, gw), vv.reshape(ns, ln, gw), band[g])
        os_.append(o)
        ls.append(l)
    os_ = [os_[0], os_[1].reshape(bsz, 4, seq // 4, gw), os_[2].reshape(bsz, MAX_DIL, seq // MAX_DIL, gw)]
    ls = [ls[0], ls[1].reshape(bsz, 4, seq // 4, HEAD_DIM), ls[2].reshape(bsz, MAX_DIL, seq // MAX_DIL, HEAD_DIM)]
    conv0 = jnp.zeros(((CONV_WIDTH - 1) * bsz, LRU_WIDTH), F32)
    h00 = jnp.zeros((bsz, LRU_WIDTH), F32)
    bout, lru_p = _lru(xb.reshape(seq * bsz, LRU_WIDTH), gy.reshape(seq * bsz, LRU_WIDTH), conv0, h00,
                       *lru_w, nb=bsz, tl=128)
    y_prompt = _mix_mlp_prompt(x_prompt, os_, ls, bout, sga, sgb, p_prompt[i], mlp_w)
    kv_p = [t.reshape(bsz, GROUP_WINDOWS[g], 2, HEADS_PER_GROUP, HEAD_DIM)[None]
            for g, t in enumerate((kvt1, kvt2, kvt3))]
    conv_p = xbt.reshape(bsz, 8 * MAX_DIL, LRU_WIDTH)[:, -(CONV_WIDTH - 1):][None]

    n_s = bd * n_t
    (sq1, sq2, sq3, _, _, _, _, _, _, sxb, sgy, ssga, ssgb, skv1, skv2, skv3, sxbt) = _in_proj_sample(
        x_sample.reshape(n_s, D_MODEL), gm, w_in_b, gq, gk)
    qs = [q.astype(F32).reshape(bd, n_t, gw) for q in (sq1, sq2, sq3)]
    caches = [c[i].reshape(bd, c.shape[2], 2 * gw) for c in (cache_kv1, cache_kv2, cache_kv3)]
    news = [jnp.pad(t.reshape(bd, n_t, 2 * gw), ((0, 0), (0, NEW_ROWS_PAD - n_t), (0, 0)))
            for t in (skv1, skv2, skv3)]
    att_s = _dec_attn(qs, caches, news, bias_cache, bias_new)

    def to_tb(a):
        return a.reshape(bd, n_t, -1).transpose(1, 0, 2).reshape(n_t * bd, -1)

    def from_tb(a):
        return a.reshape(n_t, bd, -1).transpose(1, 0, 2).reshape(bd * n_t, -1)

    conv0_s = state_conv[i].transpose(1, 0, 2).reshape((CONV_WIDTH - 1) * bd, LRU_WIDTH)
    bout_s, lru_s = _lru(to_tb(sxb), to_tb(sgy), conv0_s, state_lru[i].astype(F32), *lru_w, nb=bd, tl=n_t)
    zero_o = jnp.zeros((n_s, gw), BF16)
    zero_l = jnp.zeros((n_s, HEAD_DIM), F32)
    neg_l = jnp.full((n_s, HEAD_DIM), NEG_INF, F32)
    y_sample = _mix_mlp_sample(x_sample.reshape(n_s, D_MODEL),
                               [att_s.reshape(n_s, gw).astype(BF16), zero_o, zero_o],
                               [zero_l, neg_l, neg_l], from_tb(bout_s), ssga, ssgb,
                               p_sample[i].reshape(n_s, PLE_DIM), mlp_w).reshape(bd, n_t, D_MODEL)
    kv_s = [t.reshape(bd, n_t, 2, HEADS_PER_GROUP, HEAD_DIM)[None] for t in (skv1, skv2, skv3)]
    xcat_s = jnp.concatenate([state_conv[i], sxbt.reshape(bd, n_t, LRU_WIDTH)], axis=1)
    conv_s = xcat_s[:, n_t:][None]

    return (y_prompt, y_sample, kv_p[0], kv_p[1], kv_p[2], conv_p, lru_p[None],
            kv_s[0], kv_s[1], kv_s[2], conv_s, lru_s[None])
```

```python
import functools
import math

import jax
import jax.numpy as jnp
from jax import lax
from jax.experimental import pallas as pl
from jax.experimental.pallas import tpu as pltpu

F32 = jnp.float32
BF16 = jnp.bfloat16

D_MODEL = 1024
HEAD_DIM = 128
HEADS_PER_GROUP = 4
GROUP_WINDOWS = (128, 512, 2048)
GROUP_DILATIONS = (1, 4, 16)
N_GROUPS = 3
GROUP_WIDTH = HEADS_PER_GROUP * HEAD_DIM
ATT_WIDTH = N_GROUPS * GROUP_WIDTH
ATT_SCALE = HEAD_DIM ** -0.5
LRU_WIDTH = D_MODEL
LRU_BLOCKS = 16
LRU_BLOCK = LRU_WIDTH // LRU_BLOCKS
CONV_WIDTH = 4
RG_C = 8.0
D_FF = 4 * D_MODEL
PLE_DIM = 256
REL_BUCKETS = 32
REL_MAX_DIST = 2048
NORM_EPS = 1e-6
NEG_INF = -1e30
WINDOW_SLOTS = 128
KV_ROWS = 2 * HEADS_PER_GROUP

OFF_Q, OFF_K, OFF_V = 0, ATT_WIDTH, 2 * ATT_WIDTH
OFF_XB = 3 * ATT_WIDTH
OFF_YB = OFF_XB + LRU_WIDTH
OFF_GA = OFF_YB + LRU_WIDTH
OFF_GB = OFF_GA + D_MODEL

SUBLANES = 8
LANES = 128
ROW_TILE = 512
SAMPLE_TILE = 128
Q_BLOCK = 128
GATE_TILE = 256
VMEM_LIMIT = 58 * 1024 * 1024


def _cparams(n_axes):
    return pltpu.CompilerParams(dimension_semantics=("arbitrary",) * n_axes,
                                vmem_limit_bytes=VMEM_LIMIT)


def _resident(shape):
    nd = len(shape)
    return pl.BlockSpec(shape, lambda *_: (0,) * nd, pipeline_mode=pl.Buffered(1))


def _rms(x, gain):
    return x * lax.rsqrt(jnp.mean(x * x, axis=-1, keepdims=True) + NORM_EPS) * gain


def _dot(a, b):
    return jnp.dot(a, b, preferred_element_type=F32)


def _dot_nt(a, b):
    return lax.dot_general(a, b, (((1,), (1,)), ((), ())), preferred_element_type=F32)


def _in_proj_kernel(x_ref, gmix_ref, w_ref, gq_ref, gk_ref,
                    q1, q2, q3, k1, k2, k3, v1, v2, v3, xb_ref, gy_ref, sga_ref, sgb_ref,
                    kv1, kv2, kv3, xbt_ref, slab_ref, *, dils, tail_start, seq_axis):
    q_refs, k_refs, v_refs, kv_refs = (q1, q2, q3), (k1, k2, k3), (v1, v2, v3), (kv1, kv2, kv3)
    step = pl.program_id(seq_axis)
    tm = x_ref.shape[0]
    h = _rms(x_ref[...], gmix_ref[...]).astype(BF16)
    gq = gq_ref[...] * ATT_SCALE
    gk = gk_ref[...]

    def put(ref, hd, val, dil, slab):
        cs = slice(hd * HEAD_DIM, (hd + 1) * HEAD_DIM)
        if dil == 1:
            ref[:, cs] = val.astype(BF16)
            return
        slab_ref[slab] = val
        n = tm // dil
        for r in range(dil):
            ref[r, :, cs] = slab_ref[slab, pl.ds(r, n, stride=dil), :].astype(BF16)

    def guarded(start, fn):
        if start is None:
            fn()
        else:
            pl.when(step >= start)(fn)

    for g in range(N_GROUPS):
        n_tail = kv_refs[g].shape[0]
        c0 = g * GROUP_WIDTH
        yq = _dot(h, w_ref[:, OFF_Q + c0:OFF_Q + c0 + GROUP_WIDTH])
        yk = _dot(h, w_ref[:, OFF_K + c0:OFF_K + c0 + GROUP_WIDTH])
        yv = _dot(h, w_ref[:, OFF_V + c0:OFF_V + c0 + GROUP_WIDTH])
        kn = []
        for hd in range(HEADS_PER_GROUP):
            cs = slice(hd * HEAD_DIM, (hd + 1) * HEAD_DIM)
            put(q_refs[g], hd, _rms(yq[:, cs], gq), dils[g], hd)
            kn.append(_rms(yk[:, cs], gk))
            put(k_refs[g], hd, kn[hd], dils[g], HEADS_PER_GROUP + hd)
            put(v_refs[g], hd, yv[:, cs], dils[g], 2 * HEADS_PER_GROUP + hd)

        def write_tail(g=g, kn=kn, yv=yv, n_tail=n_tail):
            for hd in range(HEADS_PER_GROUP):
                kv_refs[g][:, hd * HEAD_DIM:(hd + 1) * HEAD_DIM] = kn[hd][tm - n_tail:, :]
            kv_refs[g][:, GROUP_WIDTH:] = yv[tm - n_tail:, :]

        guarded(tail_start[g], write_tail)

    y = _dot(h, w_ref[:, OFF_XB:OFF_XB + LRU_WIDTH])
    xb_ref[...] = y.astype(BF16)

    def write_xb_tail():
        xbt_ref[...] = y[tm - xbt_ref.shape[0]:, :]

    guarded(tail_start[N_GROUPS], write_xb_tail)
    gy_ref[...] = jax.nn.gelu(_dot(h, w_ref[:, OFF_YB:OFF_YB + LRU_WIDTH])).astype(BF16)
    sga_ref[...] = jax.nn.sigmoid(_dot(h, w_ref[:, OFF_GA:OFF_GA + D_MODEL])).astype(BF16)
    sgb_ref[...] = jax.nn.sigmoid(_dot(h, w_ref[:, OFF_GB:OFF_GB + D_MODEL])).astype(BF16)


def _in_proj_prompt(x, g_mix, w_in, g_q, g_k):
    bsz, seq, _ = x.shape
    tm = ROW_TILE
    n_tiles = seq // tm
    gw = GROUP_WIDTH

    def nat(width):
        return pl.BlockSpec((None, tm, width), lambda b, i: (b, i, 0))

    def sds(shape, dt):
        return jax.ShapeDtypeStruct(shape, dt)

    qkv_shapes, qkv_specs = [], []
    for dil in GROUP_DILATIONS:
        if dil == 1:
            qkv_shapes.append(sds((bsz, seq, gw), BF16))
            qkv_specs.append(nat(gw))
        else:
            qkv_shapes.append(sds((bsz, dil, seq // dil, gw), BF16))
            qkv_specs.append(pl.BlockSpec((None, dil, tm // dil, gw), lambda b, i: (b, 0, i, 0)))
    tail_rows = [min(w, tm) for w in GROUP_WINDOWS]
    tail_start = [n_tiles - max(w // tm, 1) for w in GROUP_WINDOWS] + [n_tiles - 1]
    tail_specs = [pl.BlockSpec((None, r, 2 * gw), lambda b, i, s=s: (b, jnp.maximum(i - s, 0), 0))
                  for r, s in zip(tail_rows, tail_start)]
    out_shape = (qkv_shapes * 3 + [sds((bsz, seq, LRU_WIDTH), BF16)] * 2 + [sds((bsz, seq, D_MODEL), BF16)] * 2
                 + [sds((bsz, w, 2 * gw), F32) for w in GROUP_WINDOWS] + [sds((bsz, SUBLANES, LRU_WIDTH), F32)])
    out_specs = (qkv_specs * 3 + [nat(LRU_WIDTH)] * 2 + [nat(D_MODEL)] * 2 + tail_specs
                 + [pl.BlockSpec((None, SUBLANES, LRU_WIDTH), lambda b, i: (b, 0, 0))])
    return pl.pallas_call(
        functools.partial(_in_proj_kernel, dils=GROUP_DILATIONS, tail_start=tuple(tail_start), seq_axis=1),
        grid=(bsz, n_tiles),
        in_specs=[nat(D_MODEL), _resident(g_mix.shape), _resident(w_in.shape),
                  _resident(g_q.shape), _resident(g_k.shape)],
        out_specs=out_specs,
        out_shape=out_shape,
        scratch_shapes=[pltpu.VMEM((3 * HEADS_PER_GROUP, tm, HEAD_DIM), F32)],
        compiler_params=_cparams(2),
        name="in_proj_prompt",
    )(x, g_mix, w_in, g_q, g_k)


def _in_proj_sample(x, g_mix, w_in, g_q, g_k):
    n = x.shape[0]
    tm = SAMPLE_TILE
    gw = GROUP_WIDTH

    def rows(width):
        return pl.BlockSpec((tm, width), lambda i: (i, 0))

    def sds(width, dt):
        return jax.ShapeDtypeStruct((n, width), dt)

    out_shape = ([sds(gw, BF16)] * 9 + [sds(LRU_WIDTH, BF16)] * 2 + [sds(D_MODEL, BF16)] * 2
                 + [sds(2 * gw, F32)] * 3 + [sds(LRU_WIDTH, F32)])
    out_specs = ([rows(gw)] * 9 + [rows(LRU_WIDTH)] * 2 + [rows(D_MODEL)] * 2
                 + [rows(2 * gw)] * 3 + [rows(LRU_WIDTH)])
    return pl.pallas_call(
        functools.partial(_in_proj_kernel, dils=(1, 1, 1), tail_start=(None,) * 4, seq_axis=0),
        grid=(n // tm,),
        in_specs=[rows(D_MODEL), _resident(g_mix.shape), _resident(w_in.shape),
                  _resident(g_q.shape), _resident(g_k.shape)],
        out_specs=out_specs,
        out_shape=out_shape,
        scratch_shapes=[pltpu.VMEM((3 * HEADS_PER_GROUP, tm, HEAD_DIM), F32)],
        compiler_params=_cparams(1),
        name="in_proj_sample",
    )(x, g_mix, w_in, g_q, g_k)


def _swa_kernel(q_ref, k_ref, v_ref, kp_ref, vp_ref, bias_ref, o_ref, lse_ref):
    first = pl.program_id(1) == 0
    n_blk = q_ref.shape[0] // Q_BLOCK
    col = lax.broadcasted_iota(jnp.int32, (Q_BLOCK, 2 * Q_BLOCK), 1)
    lane = lax.broadcasted_iota(jnp.int32, (Q_BLOCK, HEAD_DIM), 1)
    for j in range(n_blk):
        rq = slice(j * Q_BLOCK, (j + 1) * Q_BLOCK)
        lse_tile = jnp.zeros((Q_BLOCK, HEAD_DIM), F32)
        for hd in range(HEADS_PER_GROUP):
            cs = slice(hd * HEAD_DIM, (hd + 1) * HEAD_DIM)
            q = q_ref[rq, cs]
            if j == 0:
                kk = jnp.concatenate([kp_ref[:, cs], k_ref[rq, cs]], axis=0)
                vv = jnp.concatenate([vp_ref[:, cs], v_ref[rq, cs]], axis=0)
            else:
                rk = slice((j - 1) * Q_BLOCK, (j + 1) * Q_BLOCK)
                kk = k_ref[rk, cs]
                vv = v_ref[rk, cs]
            s = _dot_nt(q, kk) + bias_ref[hd]
            if j == 0:
                s = jnp.where(col < jnp.where(first, Q_BLOCK, 0), NEG_INF, s)
            m = jnp.max(s, axis=-1, keepdims=True)
            p = jnp.exp(s - m)
            den = jnp.sum(p, axis=-1, keepdims=True)
            o = _dot(p.astype(BF16), vv) / den
            o_ref[rq, cs] = o.astype(BF16)
            lse_tile = jnp.where(lane == hd, m + jnp.log(den), lse_tile)
        lse_ref[rq, :] = lse_tile


def _swa(q, k, v, band_bias):
    n_seq, length, gw = q.shape
    tq = ROW_TILE
    ratio = tq // Q_BLOCK
    cur = pl.BlockSpec((None, tq, gw), lambda s, i: (s, i, 0))
    prev = pl.BlockSpec((None, Q_BLOCK, gw), lambda s, i: (s, jnp.maximum(i * ratio - 1, 0), 0))
    return pl.pallas_call(
        _swa_kernel,
        grid=(n_seq, length // tq),
        in_specs=[cur, cur, cur, prev, prev, _resident(band_bias.shape)],
        out_specs=[cur, pl.BlockSpec((None, tq, HEAD_DIM), lambda s, i: (s, i, 0))],
        out_shape=[jax.ShapeDtypeStruct((n_seq, length, gw), BF16),
                   jax.ShapeDtypeStruct((n_seq, length, HEAD_DIM), F32)],
        compiler_params=_cparams(2),
        name="swa",
    )(q, k, v, k, v, band_bias)


DEC_BATCH_TILE = 2


def _dec_attn_kernel(q_ref, c1, c2, c3, new_ref, bc_ref, bn_ref, att_ref):
    c_refs = (c1, c2, c3)
    n_b, _, n_t = q_ref.shape[:3]
    half = KV_ROWS // 2
    for bi in range(n_b):
        for t in range(n_t):
            accs, lses, dens = [], [], []
            for g in range(N_GROUPS):
                rows = slice(0, KV_ROWS) if GROUP_DILATIONS[g] == 1 else slice(t * KV_ROWS, (t + 1) * KV_ROWS)
                tile = c_refs[g][bi, :, rows, :]
                fresh = new_ref[bi, g]
                qv = q_ref[bi, g, t]
                lc = jnp.sum(tile * qv[None], axis=-1, keepdims=True) + bc_ref[g, t]
                ln = jnp.sum(fresh * qv[None], axis=-1, keepdims=True) + bn_ref[g, t]
                m = jnp.maximum(jnp.max(lc, axis=0), jnp.max(ln, axis=0))
                pc = jnp.exp(lc - m[None])
                pn = jnp.exp(ln - m[None])
                den = jnp.sum(pc, axis=0) + jnp.sum(pn, axis=0)
                acc = (jnp.sum(pc * pltpu.roll(tile, half, axis=1), axis=0)
                       + jnp.sum(pn * pltpu.roll(fresh, half, axis=1), axis=0))
                accs.append(acc)
                dens.append(den)
                lses.append(m + jnp.log(den))
            top = jnp.maximum(jnp.maximum(lses[0], lses[1]), lses[2])
            ws = [jnp.exp(l - top) for l in lses]
            tot = ws[0] + ws[1] + ws[2]
            out = (ws[0] / tot / dens[0]) * accs[0]
            for g in (1, 2):
                out = out + (ws[g] / tot / dens[g]) * accs[g]
            att_ref[bi, t] = out[0:half, :]


def _dec_attn(qv, caches, fresh, bias_cache, bias_new):
    bd, _, n_t = qv.shape[:3]
    bb = DEC_BATCH_TILE
    cspecs = []
    for g, dil in enumerate(GROUP_DILATIONS):
        rows = KV_ROWS if dil == 1 else n_t * KV_ROWS
        cspecs.append(pl.BlockSpec((bb, WINDOW_SLOTS, rows, HEAD_DIM), lambda i: (i, 0, 0, 0)))
    small = pl.BlockSpec((bb, N_GROUPS, n_t, KV_ROWS, HEAD_DIM), lambda i: (i, 0, 0, 0, 0))
    return pl.pallas_call(
        _dec_attn_kernel,
        grid=(bd // bb,),
        in_specs=[small] + cspecs + [small, _resident(bias_cache.shape), _resident(bias_new.shape)],
        out_specs=pl.BlockSpec((bb, n_t, HEADS_PER_GROUP, HEAD_DIM), lambda i: (i, 0, 0, 0)),
        out_shape=jax.ShapeDtypeStruct((bd, n_t, HEADS_PER_GROUP, HEAD_DIM), F32),
        compiler_params=_cparams(1),
        name="dec_attn",
    )(qv, *caches, fresh, bias_cache, bias_new)


def _lru_gates(xc, cs, blk, wa_ref, ba_ref, wi_ref, bi_ref, lam_ref):
    xcb = xc[:, cs].astype(BF16)
    r = jax.nn.sigmoid(_dot(xcb, wa_ref[blk]) + ba_ref[:, cs])
    ig = jax.nn.sigmoid(_dot(xcb, wi_ref[blk]) + bi_ref[:, cs])
    lam = lam_ref[:, cs]
    log_sig = jnp.minimum(lam, 0.0) - jnp.log1p(jnp.exp(-jnp.abs(lam)))
    log_a = RG_C * r * log_sig
    a = jnp.exp(log_a)
    b = jnp.sqrt(-jnp.tanh(log_a) * (1.0 + a * a)) * (ig * xc[:, cs])
    return a, b


def _lru_seq_kernel(xb_ref, gy_ref, wconv_ref, bconv_ref, wa_ref, ba_ref, wi_ref, bi_ref, lam_ref, wob_ref,
                    out_ref, hlast_ref, xcat, a_s, b_s, hc):
    tl, c = xb_ref.shape
    n_v = tl // SUBLANES

    @pl.when(pl.program_id(1) == 0)
    def _():
        xcat[0:SUBLANES, :] = jnp.zeros((SUBLANES, c), F32)
        hc[...] = jnp.zeros((SUBLANES, c), F32)

    xcat[SUBLANES:SUBLANES + tl, :] = xb_ref[...].astype(F32)
    xc = bconv_ref[...] + wconv_ref[CONV_WIDTH - 1:CONV_WIDTH, :] * xcat[SUBLANES:SUBLANES + tl, :]
    for j in range(CONV_WIDTH - 1):
        lag = CONV_WIDTH - 1 - j
        xc = xc + wconv_ref[j:j + 1, :] * xcat[SUBLANES - lag:SUBLANES - lag + tl, :]
    xcat[0:SUBLANES, :] = xcat[tl:tl + SUBLANES, :]

    row = lax.broadcasted_iota(jnp.int32, (n_v, SUBLANES, GATE_TILE), 1)
    for blk in range(c // GATE_TILE):
        cs = slice(blk * GATE_TILE, (blk + 1) * GATE_TILE)
        a, b = _lru_gates(xc, cs, blk, wa_ref, ba_ref, wi_ref, bi_ref, lam_ref)
        a = a.reshape(n_v, SUBLANES, GATE_TILE)
        b = b.reshape(n_v, SUBLANES, GATE_TILE)
        d = 1
        while d < SUBLANES:
            keep = row >= d
            b = jnp.where(keep, a * pltpu.roll(b, d, axis=1), 0.0) + b
            a = jnp.where(keep, a * pltpu.roll(a, d, axis=1), a)
            d *= 2
        a_s[:, :, cs] = a
        b_s[:, :, cs] = b

    def step(v, h):
        hv = a_s[v] * h + b_s[v]
        b_s[v] = hv
        return jnp.broadcast_to(hv[SUBLANES - 1:SUBLANES, :], hv.shape)

    h = lax.fori_loop(0, n_v, step, hc[...], unroll=8)
    hc[...] = h
    hlast_ref[...] = h[0:1, :]
    hs = b_s[...].reshape(tl, c)
    out_ref[...] = _dot((hs * gy_ref[...].astype(F32)).astype(BF16), wob_ref[...]).astype(BF16)


def _lru_seq(xb, gy, w_conv, b_conv, wa, ba, wi, bi, lam, w_ob):
    bsz, seq, c = xb.shape
    tl = ROW_TILE
    tile = pl.BlockSpec((None, tl, c), lambda b, i: (b, i, 0))
    consts = [w_conv, b_conv, wa, ba, wi, bi, lam, w_ob]
    return pl.pallas_call(
        _lru_seq_kernel,
        grid=(bsz, seq // tl),
        in_specs=[tile, tile] + [_resident(a.shape) for a in consts],
        out_specs=[pl.BlockSpec((None, tl, D_MODEL), lambda b, i: (b, i, 0)),
                   pl.BlockSpec((None, 1, c), lambda b, i: (b, 0, 0))],
        out_shape=[jax.ShapeDtypeStruct((bsz, seq, D_MODEL), BF16), jax.ShapeDtypeStruct((bsz, 1, c), F32)],
        scratch_shapes=[pltpu.VMEM((tl + SUBLANES, c), F32), pltpu.VMEM((tl // SUBLANES, SUBLANES, c), F32),
                        pltpu.VMEM((tl // SUBLANES, SUBLANES, c), F32), pltpu.VMEM((SUBLANES, c), F32)],
        compiler_params=_cparams(2),
        name="lru_seq",
    )(xb, gy, *consts)


def _lru_step_kernel(xb_ref, gy_ref, conv0_ref, h0_ref, wconv_ref, bconv_ref, wa_ref, ba_ref, wi_ref, bi_ref,
                     lam_ref, wob_ref, out_ref, hlast_ref, xcat, b_s, *, nb, tl):
    rows = nb * tl
    halo = (CONV_WIDTH - 1) * nb
    xcat[0:halo, :] = conv0_ref[...]
    xcat[halo:halo + rows, :] = xb_ref[...].astype(F32)
    xc = bconv_ref[...] + wconv_ref[0:1, :] * xcat[0:rows, :]
    for j in range(1, CONV_WIDTH):
        xc = xc + wconv_ref[j:j + 1, :] * xcat[j * nb:j * nb + rows, :]
    for blk in range(xc.shape[1] // GATE_TILE):
        cs = slice(blk * GATE_TILE, (blk + 1) * GATE_TILE)
        a, b = _lru_gates(xc, cs, blk, wa_ref, ba_ref, wi_ref, bi_ref, lam_ref)
        h = h0_ref[:, cs]
        for t in range(tl):
            rs = slice(t * nb, (t + 1) * nb)
            h = a[rs, :] * h + b[rs, :]
            b_s[rs, cs] = h
        hlast_ref[:, cs] = h
    out_ref[...] = _dot((b_s[...] * gy_ref[...].astype(F32)).astype(BF16), wob_ref[...]).astype(BF16)


def _lru_step(xb, gy, conv0, h0, w_conv, b_conv, wa, ba, wi, bi, lam, w_ob, *, nb, tl):
    n_rows, c = xb.shape
    consts = [conv0, h0, w_conv, b_conv, wa, ba, wi, bi, lam, w_ob]
    full = lambda shape: pl.BlockSpec(shape, lambda i: (0,) * len(shape))
    return pl.pallas_call(
        functools.partial(_lru_step_kernel, nb=nb, tl=tl),
        grid=(1,),
        in_specs=[full(xb.shape), full(gy.shape)] + [full(a.shape) for a in consts],
        out_specs=[full((n_rows, D_MODEL)), full((nb, c))],
        out_shape=[jax.ShapeDtypeStruct((n_rows, D_MODEL), BF16), jax.ShapeDtypeStruct((nb, c), F32)],
        scratch_shapes=[pltpu.VMEM((n_rows + (CONV_WIDTH - 1) * nb, c), F32), pltpu.VMEM((n_rows, c), F32)],
        compiler_params=_cparams(1),
        name="lru_step",
    )(xb, gy, *consts)


FF_CHUNK = 1024
N_O_SLABS = GROUP_WIDTH // LANES


def _mix_mlp_kernel(x_ref, o1, o2, o3, l1, l2, l3, bout_ref, sga_ref, sgb_ref, pe_ref,
                    woa_ref, wo_ref, gmlp_ref, wup_ref, wdown_ref, gple_ref, wpg_ref, wpe_ref, y_ref,
                    slab_ref, *, dils):
    tm = x_ref.shape[0]

    def natural(ref, n_slabs, dil, base):
        if dil == 1:
            return [ref[:, c * LANES:(c + 1) * LANES].astype(F32) for c in range(n_slabs)]
        n = tm // dil
        for c in range(n_slabs):
            for r in range(dil):
                slab_ref[base + c, pl.ds(r, n, stride=dil), :] = ref[r, :, c * LANES:(c + 1) * LANES].astype(F32)
        return [slab_ref[base + c] for c in range(n_slabs)]

    os_, lses = [], []
    for g, (o_ref, l_ref) in enumerate(((o1, l1), (o2, l2), (o3, l3))):
        base = g * (N_O_SLABS + 1)
        os_.append(natural(o_ref, N_O_SLABS, dils[g], base))
        lses.append(natural(l_ref, 1, dils[g], base + N_O_SLABS)[0])
    top = jnp.maximum(jnp.maximum(lses[0], lses[1]), lses[2])
    ws = [jnp.exp(l - top) for l in lses]
    tot = ws[0] + ws[1] + ws[2]
    ws = [w / tot for w in ws]
    heads = []
    for hd in range(HEADS_PER_GROUP):
        acc = ws[0][:, hd:hd + 1] * os_[0][hd]
        for g in (1, 2):
            acc = acc + ws[g][:, hd:hd + 1] * os_[g][hd]
        heads.append(acc.astype(BF16))
    att = jnp.concatenate(heads, axis=1)
    a_out = _dot(att, woa_ref[...])
    mix = sga_ref[...].astype(F32) * a_out + sgb_ref[...].astype(F32) * bout_ref[...].astype(F32)
    x = x_ref[...] + _dot(mix.astype(BF16), wo_ref[...])
    h = _rms(x, gmlp_ref[...]).astype(BF16)
    acc = jnp.zeros(x.shape, F32)
    for c in range(D_FF // FF_CHUNK):
        cs = slice(c * FF_CHUNK, (c + 1) * FF_CHUNK)
        u = jnp.square(jnp.maximum(_dot(h, wup_ref[:, cs]), 0.0))
        acc = acc + _dot(u.astype(BF16), wdown_ref[cs, :])
    x = x + acc
    h = _rms(x, gple_ref[...]).astype(BF16)
    gate = jax.nn.sigmoid(_dot(h, wpg_ref[...]))
    y_ref[...] = x + gate * _dot(pe_ref[...].astype(BF16), wpe_ref[...])


def _mix_mlp_prompt(x, os_, ls, bout, sga, sgb, pe, weights):
    bsz, seq, _ = x.shape
    tm = ROW_TILE

    def nat(width):
        return pl.BlockSpec((None, tm, width), lambda b, i: (b, i, 0))

    def grouped(width):
        return [nat(width) if dil == 1 else
                pl.BlockSpec((None, dil, tm // dil, width), lambda b, i: (b, 0, i, 0))
                for dil in GROUP_DILATIONS]

    ws = list(weights)
    specs = ([nat(D_MODEL)] + grouped(GROUP_WIDTH) + grouped(HEAD_DIM) + [nat(D_MODEL)] * 3 + [nat(PLE_DIM)]
             + [_resident(w.shape) for w in ws])
    return pl.pallas_call(
        functools.partial(_mix_mlp_kernel, dils=GROUP_DILATIONS),
        grid=(bsz, seq // tm),
        in_specs=specs,
        out_specs=nat(D_MODEL),
        out_shape=jax.ShapeDtypeStruct((bsz, seq, D_MODEL), F32),
        scratch_shapes=[pltpu.VMEM((N_GROUPS * (N_O_SLABS + 1), tm, LANES), F32)],
        compiler_params=_cparams(2),
        name="mix_mlp_prompt",
    )(x, *os_, *ls, bout, sga, sgb, pe, *ws)


def _mix_mlp_sample(x, os_, ls, bout, sga, sgb, pe, weights):
    n = x.shape[0]
    tm = SAMPLE_TILE

    def rows(width):
        return pl.BlockSpec((tm, width), lambda i: (i, 0))

    ws = list(weights)
    specs = ([rows(D_MODEL)] + [rows(GROUP_WIDTH)] * 3 + [rows(HEAD_DIM)] * 3 + [rows(D_MODEL)] * 3
             + [rows(PLE_DIM)] + [_resident(w.shape) for w in ws])
    return pl.pallas_call(
        functools.partial(_mix_mlp_kernel, dils=(1, 1, 1)),
        grid=(n // tm,),
        in_specs=specs,
        out_specs=rows(D_MODEL),
        out_shape=jax.ShapeDtypeStruct((n, D_MODEL), F32),
        scratch_shapes=[pltpu.VMEM((N_GROUPS * (N_O_SLABS + 1), tm, LANES), F32)],
        compiler_params=_cparams(1),
        name="mix_mlp_sample",
    )(x, *os_, *ls, bout, sga, sgb, pe, *ws)


def _t5_bucket(dist):
    max_exact = REL_BUCKETS // 2
    d = jnp.maximum(dist, 1).astype(F32)
    large = max_exact + (jnp.log(d / max_exact) / math.log(REL_MAX_DIST / max_exact)
                         * (REL_BUCKETS - max_exact)).astype(jnp.int32)
    large = jnp.minimum(large, REL_BUCKETS - 1)
    return jnp.where(dist < max_exact, dist, large)


def _slot_bias(rel_bias, g):
    dil = GROUP_DILATIONS[g]
    dist = dil * jnp.arange(WINDOW_SLOTS + 1, dtype=jnp.int32)
    hs = slice(g * HEADS_PER_GROUP, (g + 1) * HEADS_PER_GROUP)
    return rel_bias[_t5_bucket(dist)][:, hs].astype(F32)


def _band_bias(slot_bias):
    iq = jnp.arange(Q_BLOCK)[:, None]
    ik = jnp.arange(2 * Q_BLOCK)[None, :]
    slot = Q_BLOCK + iq - ik
    ok = (slot >= 0) & (slot <= WINDOW_SLOTS)
    band = jnp.transpose(slot_bias[jnp.clip(slot, 0, WINDOW_SLOTS)], (2, 0, 1))
    return jnp.where(ok[None], band, NEG_INF)


def _kv_rows_table(tbl):
    tbl = jnp.pad(tbl, [(0, 0)] * (tbl.ndim - 1) + [(0, KV_ROWS - tbl.shape[-1])])
    return jnp.broadcast_to(tbl[..., None], tbl.shape + (HEAD_DIM,))


def _dec_bias(slot_biases, n_t):
    c = jnp.arange(WINDOW_SLOTS)[None, :]
    t = jnp.arange(n_t)[:, None]
    tn = jnp.arange(n_t)[None, :]
    bc, bn = [], []
    for g in range(N_GROUPS):
        sb = slot_biases[g]
        if GROUP_DILATIONS[g] == 1:
            jc = WINDOW_SLOTS + t - c
            okc = jc <= WINDOW_SLOTS
            jn = t - tn
            okn = jn >= 0
        else:
            jc = jnp.broadcast_to(WINDOW_SLOTS - c, (n_t, WINDOW_SLOTS))
            okc = jnp.ones_like(jc, dtype=bool)
            jn = jnp.zeros((n_t, n_t), jnp.int32)
            okn = tn == t
        bc.append(_kv_rows_table(jnp.where(okc[..., None], sb[jnp.clip(jc, 0, WINDOW_SLOTS)], NEG_INF)))
        bn.append(_kv_rows_table(jnp.where(okn[..., None], sb[jnp.clip(jn, 0, WINDOW_SLOTS)], NEG_INF)))
    return jnp.stack(bc), jnp.stack(bn)


def _gate_tiles(w):
    per = GATE_TILE // LRU_BLOCK
    w = w.reshape(LRU_WIDTH // GATE_TILE, per, LRU_BLOCK, LRU_BLOCK)
    eye = jnp.eye(per, dtype=w.dtype)
    return jnp.einsum('npij,pq->npiqj', w, eye).reshape(-1, GATE_TILE, GATE_TILE).astype(BF16)


def kernel(x_prompt, x_sample, p_prompt, p_sample, cache_kv1, cache_kv2, cache_kv3, state_conv, state_lru,
           rel_bias, g_mix, w_in, g_q, g_k, w_oa, w_conv, b_conv, w_rg_a, b_rg_a, w_rg_i, b_rg_i, lam,
           w_ob, w_o, g_mlp, w_up, w_down, g_ple, w_ple_gate, w_ple_in):
    depth = w_in.shape[0]
    assert depth == 1
    bsz, seq, _ = x_prompt.shape
    bd, n_t, _ = x_sample.shape
    gw = GROUP_WIDTH
    row = lambda a: a.reshape(1, -1)

    slot_biases = [_slot_bias(rel_bias, g) for g in range(N_GROUPS)]
    band = [_band_bias(sb) for sb in slot_biases]
    bias_cache, bias_new = _dec_bias(slot_biases, n_t)

    i = 0
    w_in_b = w_in[i].astype(BF16)
    gm, gq, gk = row(g_mix[i]), row(g_q[i]), row(g_k[i])
    lru_w = (w_conv[i], row(b_conv[i]), _gate_tiles(w_rg_a[i]), row(b_rg_a[i]), _gate_tiles(w_rg_i[i]),
             row(b_rg_i[i]), row(lam[i]), w_ob[i].astype(BF16))
    mlp_w = (w_oa[i].astype(BF16), w_o[i].astype(BF16), row(g_mlp[i]), w_up[i].astype(BF16),
             w_down[i].astype(BF16), row(g_ple[i]), w_ple_gate[i].astype(BF16), w_ple_in[i].astype(BF16))

    (q1, q2, q3, k1, k2, k3, v1, v2, v3, xb, gy, sga, sgb, kvt1, kvt2, kvt3, xbt) = _in_proj_prompt(
        x_prompt, gm, w_in_b, gq, gk)
    os_, ls = [], []
    for g, (qq, kk, vv) in enumerate(((q1, k1, v1), (q2, k2, v2), (q3, k3, v3))):
        dil = GROUP_DILATIONS[g]
        ns, ln = bsz * dil, seq // dil
        o, l = _swa(qq.reshape(ns, ln, gw), kk.reshape(ns, ln, gw), vv.reshape(ns, ln, gw), band[g])
        os_.append(o if dil == 1 else o.reshape(bsz, dil, ln, gw))
        ls.append(l if dil == 1 else l.reshape(bsz, dil, ln, HEAD_DIM))
    bout, lru_p = _lru_seq(xb, gy, *lru_w)
    y_prompt = _mix_mlp_prompt(x_prompt, os_, ls, bout, sga, sgb, p_prompt[i], mlp_w)
    kv_p = [t.reshape(bsz, GROUP_WINDOWS[g], 2, HEADS_PER_GROUP, HEAD_DIM)[None]
            for g, t in enumerate((kvt1, kvt2, kvt3))]
    conv_p = xbt[:, SUBLANES - (CONV_WIDTH - 1):][None]
    lru_p = lru_p.reshape(1, bsz, LRU_WIDTH)

    n_s = bd * n_t
    (sq1, sq2, sq3, _, _, _, _, _, _, sxb, sgy, ssga, ssgb, skv1, skv2, skv3, sxbt) = _in_proj_sample(
        x_sample.reshape(n_s, D_MODEL), gm, w_in_b, gq, gk)
    qv = jnp.stack([q.astype(F32).reshape(bd, n_t, HEADS_PER_GROUP, HEAD_DIM) for q in (sq1, sq2, sq3)], axis=1)
    qv = jnp.pad(qv, ((0, 0), (0, 0), (0, 0), (0, KV_ROWS - HEADS_PER_GROUP), (0, 0)))
    fresh = jnp.stack([t.reshape(bd, n_t, KV_ROWS, HEAD_DIM) for t in (skv1, skv2, skv3)], axis=1)
    caches = [c.reshape(bd, WINDOW_SLOTS, dil * KV_ROWS, HEAD_DIM)
              for c, dil in zip((cache_kv1, cache_kv2, cache_kv3), GROUP_DILATIONS)]
    att_s = _dec_attn(qv, caches, fresh, bias_cache, bias_new).reshape(n_s, gw)

    def to_tb(a):
        return a.reshape(bd, n_t, -1).transpose(1, 0, 2).reshape(n_t * bd, -1)

    def from_tb(a):
        return a.reshape(n_t, bd, -1).transpose(1, 0, 2).reshape(bd * n_t, -1)

    conv0_s = state_conv[i].transpose(1, 0, 2).reshape((CONV_WIDTH - 1) * bd, LRU_WIDTH)
    bout_s, lru_s = _lru_step(to_tb(sxb), to_tb(sgy), conv0_s, state_lru[i].astype(F32), *lru_w, nb=bd, tl=n_t)
    zero_o = jnp.zeros((n_s, gw), BF16)
    zero_l = jnp.zeros((n_s, HEAD_DIM), F32)
    neg_l = jnp.full((n_s, HEAD_DIM), NEG_INF, F32)
    y_sample = _mix_mlp_sample(x_sample.reshape(n_s, D_MODEL), [att_s.astype(BF16), zero_o, zero_o],
                               [zero_l, neg_l, neg_l], from_tb(bout_s), ssga, ssgb,
                               p_sample[i].reshape(n_s, PLE_DIM), mlp_w).reshape(bd, n_t, D_MODEL)
    kv_s = [t.reshape(bd, n_t, 2, HEADS_PER_GROUP, HEAD_DIM)[None] for t in (skv1, skv2, skv3)]
    xcat_s = jnp.concatenate([state_conv[i], sxbt.reshape(bd, n_t, LRU_WIDTH)], axis=1)
    conv_s = xcat_s[:, n_t:][None]

    return (y_prompt, y_sample, kv_p[0], kv_p[1], kv_p[2], conv_p, lru_p,
            kv_s[0], kv_s[1], kv_s[2], conv_s, lru_s[None])
```

```python
import functools
import math

import jax
import jax.numpy as jnp
from jax import lax
from jax.experimental import pallas as pl
from jax.experimental.pallas import tpu as pltpu

F32 = jnp.float32
BF16 = jnp.bfloat16

D_MODEL = 1024
HEAD_DIM = 128
HEADS_PER_GROUP = 4
GROUP_WINDOWS = (128, 512, 2048)
GROUP_DILATIONS = (1, 4, 16)
N_GROUPS = 3
GROUP_WIDTH = HEADS_PER_GROUP * HEAD_DIM
ATT_WIDTH = N_GROUPS * GROUP_WIDTH
ATT_SCALE = HEAD_DIM ** -0.5
LRU_WIDTH = D_MODEL
LRU_BLOCKS = 16
LRU_BLOCK = LRU_WIDTH // LRU_BLOCKS
CONV_WIDTH = 4
RG_C = 8.0
D_FF = 4 * D_MODEL
PLE_DIM = 256
REL_BUCKETS = 32
REL_MAX_DIST = 2048
NORM_EPS = 1e-6
NEG_INF = -1e30
WINDOW_SLOTS = 128
KV_ROWS = 2 * HEADS_PER_GROUP

OFF_Q, OFF_K, OFF_V = 0, ATT_WIDTH, 2 * ATT_WIDTH
OFF_XB = 3 * ATT_WIDTH
OFF_YB = OFF_XB + LRU_WIDTH
OFF_GA = OFF_YB + LRU_WIDTH
OFF_GB = OFF_GA + D_MODEL

SUBLANES = 8
LANES = 128
ROW_TILE = 512
SAMPLE_TILE = 128
Q_BLOCK = 128
SWA_ROWS = 1024
LRU_STEPS = 128
GATE_TILE = 256
VMEM_LIMIT = 58 * 1024 * 1024


def _cparams(n_axes):
    return pltpu.CompilerParams(dimension_semantics=("arbitrary",) * n_axes,
                                vmem_limit_bytes=VMEM_LIMIT)


def _resident(shape):
    nd = len(shape)
    return pl.BlockSpec(shape, lambda *_: (0,) * nd, pipeline_mode=pl.Buffered(1))


def _rms(x, gain):
    return x * lax.rsqrt(jnp.mean(x * x, axis=-1, keepdims=True) + NORM_EPS) * gain


def _dot(a, b):
    return jnp.dot(a, b, preferred_element_type=F32)


def _dot_nt(a, b):
    return lax.dot_general(a, b, (((1,), (1,)), ((), ())), preferred_element_type=F32)


def _in_proj_kernel(x_ref, gmix_ref, w_ref, gq_ref, gk_ref,
                    q1, q2, q3, k1, k2, k3, v1, v2, v3, xb_ref, gy_ref, sga_ref, sgb_ref,
                    kv1, kv2, kv3, xbt_ref, slab_ref, *, dils):
    q_refs, k_refs, v_refs, kv_refs = (q1, q2, q3), (k1, k2, k3), (v1, v2, v3), (kv1, kv2, kv3)
    tm = x_ref.shape[0]
    h = _rms(x_ref[...], gmix_ref[...]).astype(BF16)
    gq = gq_ref[...] * ATT_SCALE
    gk = gk_ref[...]

    def put(ref, hd, val, dil, slab):
        cs = slice(hd * HEAD_DIM, (hd + 1) * HEAD_DIM)
        if dil == 1:
            ref[:, cs] = val.astype(BF16)
            return
        slab_ref[slab] = val
        n = tm // dil
        for r in range(dil):
            ref[r, :, cs] = slab_ref[slab, pl.ds(r, n, stride=dil), :].astype(BF16)

    for g in range(N_GROUPS):
        n_tail = kv_refs[g].shape[0] // KV_ROWS
        c0 = g * GROUP_WIDTH
        yq = _dot(h, w_ref[:, OFF_Q + c0:OFF_Q + c0 + GROUP_WIDTH])
        yk = _dot(h, w_ref[:, OFF_K + c0:OFF_K + c0 + GROUP_WIDTH])
        yv = _dot(h, w_ref[:, OFF_V + c0:OFF_V + c0 + GROUP_WIDTH])
        for hd in range(HEADS_PER_GROUP):
            cs = slice(hd * HEAD_DIM, (hd + 1) * HEAD_DIM)
            put(q_refs[g], hd, _rms(yq[:, cs], gq), dils[g], hd)
            kn = _rms(yk[:, cs], gk)
            put(k_refs[g], hd, kn, dils[g], HEADS_PER_GROUP + hd)
            put(v_refs[g], hd, yv[:, cs], dils[g], 2 * HEADS_PER_GROUP + hd)
            kv_refs[g][pl.ds(hd, n_tail, stride=KV_ROWS), :] = kn[tm - n_tail:, :]
            kv_refs[g][pl.ds(HEADS_PER_GROUP + hd, n_tail, stride=KV_ROWS), :] = yv[tm - n_tail:, cs]

    y = _dot(h, w_ref[:, OFF_XB:OFF_XB + LRU_WIDTH])
    xb_ref[...] = y.astype(BF16)
    xbt_ref[...] = y[tm - xbt_ref.shape[0]:, :]
    gy_ref[...] = jax.nn.gelu(_dot(h, w_ref[:, OFF_YB:OFF_YB + LRU_WIDTH])).astype(BF16)
    sga_ref[...] = jax.nn.sigmoid(_dot(h, w_ref[:, OFF_GA:OFF_GA + D_MODEL])).astype(BF16)
    sgb_ref[...] = jax.nn.sigmoid(_dot(h, w_ref[:, OFF_GB:OFF_GB + D_MODEL])).astype(BF16)


def _in_proj_prompt(x, g_mix, w_in, g_q, g_k):
    bsz, seq, _ = x.shape
    tm = ROW_TILE
    n_tiles = seq // tm
    gw = GROUP_WIDTH

    def nat(width):
        return pl.BlockSpec((None, tm, width), lambda b, i: (b, i, 0))

    def sds(shape, dt):
        return jax.ShapeDtypeStruct(shape, dt)

    qkv_shapes, qkv_specs = [], []
    for dil in GROUP_DILATIONS:
        if dil == 1:
            qkv_shapes.append(sds((bsz, seq, gw), BF16))
            qkv_specs.append(nat(gw))
        else:
            qkv_shapes.append(sds((bsz, dil, seq // dil, gw), BF16))
            qkv_specs.append(pl.BlockSpec((None, dil, tm // dil, gw), lambda b, i: (b, 0, i, 0)))
    tail_rows = [min(w, tm) for w in GROUP_WINDOWS]
    tail_start = [n_tiles - max(w // tm, 1) for w in GROUP_WINDOWS]
    tail_specs = [pl.BlockSpec((None, r * KV_ROWS, HEAD_DIM), lambda b, i, s=s: (b, jnp.maximum(i - s, 0), 0))
                  for r, s in zip(tail_rows, tail_start)]
    out_shape = (qkv_shapes * 3 + [sds((bsz, seq, LRU_WIDTH), BF16)] * 2 + [sds((bsz, seq, D_MODEL), BF16)] * 2
                 + [sds((bsz, w * KV_ROWS, HEAD_DIM), F32) for w in GROUP_WINDOWS]
                 + [sds((bsz, SUBLANES, LRU_WIDTH), F32)])
    out_specs = (qkv_specs * 3 + [nat(LRU_WIDTH)] * 2 + [nat(D_MODEL)] * 2 + tail_specs
                 + [pl.BlockSpec((None, SUBLANES, LRU_WIDTH), lambda b, i: (b, 0, 0))])
    return pl.pallas_call(
        functools.partial(_in_proj_kernel, dils=GROUP_DILATIONS),
        grid=(bsz, n_tiles),
        in_specs=[nat(D_MODEL), _resident(g_mix.shape), _resident(w_in.shape),
                  _resident(g_q.shape), _resident(g_k.shape)],
        out_specs=out_specs,
        out_shape=out_shape,
        scratch_shapes=[pltpu.VMEM((3 * HEADS_PER_GROUP, tm, HEAD_DIM), F32)],
        compiler_params=_cparams(2),
        name="in_proj_prompt",
    )(x, g_mix, w_in, g_q, g_k)


def _in_proj_sample(x, g_mix, w_in, g_q, g_k):
    n = x.shape[0]
    tm = SAMPLE_TILE
    gw = GROUP_WIDTH

    def rows(width):
        return pl.BlockSpec((tm, width), lambda i: (i, 0))

    def sds(width, dt):
        return jax.ShapeDtypeStruct((n, width), dt)

    out_shape = ([sds(gw, BF16)] * 9 + [sds(LRU_WIDTH, BF16)] * 2 + [sds(D_MODEL, BF16)] * 2
                 + [jax.ShapeDtypeStruct((n * KV_ROWS, HEAD_DIM), F32)] * 3 + [sds(LRU_WIDTH, F32)])
    out_specs = ([rows(gw)] * 9 + [rows(LRU_WIDTH)] * 2 + [rows(D_MODEL)] * 2
                 + [pl.BlockSpec((tm * KV_ROWS, HEAD_DIM), lambda i: (i, 0))] * 3 + [rows(LRU_WIDTH)])
    return pl.pallas_call(
        functools.partial(_in_proj_kernel, dils=(1, 1, 1)),
        grid=(n // tm,),
        in_specs=[rows(D_MODEL), _resident(g_mix.shape), _resident(w_in.shape),
                  _resident(g_q.shape), _resident(g_k.shape)],
        out_specs=out_specs,
        out_shape=out_shape,
        scratch_shapes=[pltpu.VMEM((3 * HEADS_PER_GROUP, tm, HEAD_DIM), F32)],
        compiler_params=_cparams(1),
        name="in_proj_sample",
    )(x, g_mix, w_in, g_q, g_k)


def _swa_kernel(q_ref, k_ref, v_ref, kp_ref, vp_ref, bias_ref, o_ref, lse_ref):
    first = pl.program_id(1) == 0
    n_seq = q_ref.shape[0]
    n_blk = q_ref.shape[1] // Q_BLOCK
    col = lax.broadcasted_iota(jnp.int32, (Q_BLOCK, 2 * Q_BLOCK), 1)
    lane = lax.broadcasted_iota(jnp.int32, (Q_BLOCK, HEAD_DIM), 1)
    for sq in range(n_seq):
        for j in range(n_blk):
            rq = slice(j * Q_BLOCK, (j + 1) * Q_BLOCK)
            lse_tile = jnp.zeros((Q_BLOCK, HEAD_DIM), F32)
            for hd in range(HEADS_PER_GROUP):
                cs = slice(hd * HEAD_DIM, (hd + 1) * HEAD_DIM)
                q = q_ref[sq, rq, cs]
                if j == 0:
                    kk = jnp.concatenate([kp_ref[sq, :, cs], k_ref[sq, rq, cs]], axis=0)
                    vv = jnp.concatenate([vp_ref[sq, :, cs], v_ref[sq, rq, cs]], axis=0)
                else:
                    rk = slice((j - 1) * Q_BLOCK, (j + 1) * Q_BLOCK)
                    kk = k_ref[sq, rk, cs]
                    vv = v_ref[sq, rk, cs]
                s = _dot_nt(q, kk) + bias_ref[hd]
                if j == 0:
                    s = jnp.where(col < jnp.where(first, Q_BLOCK, 0), NEG_INF, s)
                m = jnp.max(s, axis=-1, keepdims=True)
                p = jnp.exp(s - m)
                den = jnp.sum(p, axis=-1, keepdims=True)
                o = _dot(p.astype(BF16), vv) / den
                o_ref[sq, rq, cs] = o.astype(BF16)
                lse_tile = jnp.where(lane == hd, m + jnp.log(den), lse_tile)
            lse_ref[sq, rq, :] = lse_tile


def _swa(q, k, v, band_bias):
    n_seq, length, gw = q.shape
    tq = min(length, SWA_ROWS)
    sb = SWA_ROWS // tq
    ratio = tq // Q_BLOCK
    cur = pl.BlockSpec((sb, tq, gw), lambda s, i: (s, i, 0))
    prev = pl.BlockSpec((sb, Q_BLOCK, gw), lambda s, i: (s, jnp.maximum(i * ratio - 1, 0), 0))
    return pl.pallas_call(
        _swa_kernel,
        grid=(n_seq // sb, length // tq),
        in_specs=[cur, cur, cur, prev, prev, _resident(band_bias.shape)],
        out_specs=[cur, pl.BlockSpec((sb, tq, HEAD_DIM), lambda s, i: (s, i, 0))],
        out_shape=[jax.ShapeDtypeStruct((n_seq, length, gw), BF16),
                   jax.ShapeDtypeStruct((n_seq, length, HEAD_DIM), F32)],
        compiler_params=_cparams(2),
        name="swa",
    )(q, k, v, k, v, band_bias)


DEC_BATCH_TILE = 2


def _dec_attn_kernel(q_ref, c1, c2, c3, new_ref, bc_ref, bn_ref, att_ref):
    c_refs = (c1, c2, c3)
    n_b, _, n_t = q_ref.shape[:3]
    half = KV_ROWS // 2
    for bi in range(n_b):
        for t in range(n_t):
            accs, lses, dens = [], [], []
            for g in range(N_GROUPS):
                rows = slice(0, KV_ROWS) if GROUP_DILATIONS[g] == 1 else slice(t * KV_ROWS, (t + 1) * KV_ROWS)
                tile = c_refs[g][bi, :, rows, :]
                fresh = new_ref[bi, g]
                qv = q_ref[bi, g, t]
                lc = jnp.sum(tile * qv[None], axis=-1, keepdims=True) + bc_ref[g, t]
                ln = jnp.sum(fresh * qv[None], axis=-1, keepdims=True) + bn_ref[g, t]
                m = jnp.maximum(jnp.max(lc, axis=0), jnp.max(ln, axis=0))
                pc = jnp.exp(lc - m[None])
                pn = jnp.exp(ln - m[None])
                den = jnp.sum(pc, axis=0) + jnp.sum(pn, axis=0)
                acc = (jnp.sum(pc * pltpu.roll(tile, half, axis=1), axis=0)
                       + jnp.sum(pn * pltpu.roll(fresh, half, axis=1), axis=0))
                accs.append(acc)
                dens.append(den)
                lses.append(m + jnp.log(den))
            top = jnp.maximum(jnp.maximum(lses[0], lses[1]), lses[2])
            ws = [jnp.exp(l - top) for l in lses]
            tot = ws[0] + ws[1] + ws[2]
            out = (ws[0] / tot / dens[0]) * accs[0]
            for g in (1, 2):
                out = out + (ws[g] / tot / dens[g]) * accs[g]
            att_ref[bi, t] = out[0:half, :]


def _dec_attn(qv, caches, fresh, bias_cache, bias_new):
    bd, _, n_t = qv.shape[:3]
    bb = DEC_BATCH_TILE
    cspecs = []
    for g, dil in enumerate(GROUP_DILATIONS):
        rows = KV_ROWS if dil == 1 else n_t * KV_ROWS
        cspecs.append(pl.BlockSpec((bb, WINDOW_SLOTS, rows, HEAD_DIM), lambda i: (i, 0, 0, 0)))
    small = pl.BlockSpec((bb, N_GROUPS, n_t, KV_ROWS, HEAD_DIM), lambda i: (i, 0, 0, 0, 0))
    return pl.pallas_call(
        _dec_attn_kernel,
        grid=(bd // bb,),
        in_specs=[small] + cspecs + [small, _resident(bias_cache.shape), _resident(bias_new.shape)],
        out_specs=pl.BlockSpec((bb, n_t, HEADS_PER_GROUP, HEAD_DIM), lambda i: (i, 0, 0, 0)),
        out_shape=jax.ShapeDtypeStruct((bd, n_t, HEADS_PER_GROUP, HEAD_DIM), F32),
        compiler_params=_cparams(1),
        name="dec_attn",
    )(qv, *caches, fresh, bias_cache, bias_new)


def _lru_gates(xc, cs, blk, wa_ref, ba_ref, wi_ref, bi_ref, lam_ref):
    xcb = xc.astype(BF16)
    tr = jnp.tanh(_dot(xcb, wa_ref[blk]) + 0.5 * ba_ref[:, cs])
    ti = jnp.tanh(_dot(xcb, wi_ref[blk]) + 0.5 * bi_ref[:, cs])
    lam = lam_ref[:, cs]
    log_sig = jnp.minimum(lam, 0.0) - jnp.log1p(jnp.exp(-jnp.abs(lam)))
    half_c = (0.5 * RG_C) * log_sig
    log_a = half_c * tr + half_c
    a = jnp.exp(log_a)
    hx = 0.5 * xc
    b = jnp.sqrt(-jnp.tanh(log_a) * (1.0 + a * a)) * (hx * ti + hx)
    return a, b


def _lru_seq_kernel(xb_ref, gy_ref, wconv_ref, bconv_ref, wa_ref, ba_ref, wi_ref, bi_ref, lam_ref, wob_ref,
                    out_ref, hlast_ref, xcat, a_s, b_s, h_sl, hg, hc):
    nb, tl, c = xb_ref.shape
    rows = nb * tl
    halo = (CONV_WIDTH - 1) * nb
    n_sl = c // LANES
    per = GATE_TILE // LANES

    @pl.when(pl.program_id(0) == 0)
    def _():
        xcat[:, 0:halo, :] = jnp.zeros((n_sl, halo, LANES), F32)
        hc[...] = jnp.zeros((nb, c), F32)

    for b in range(nb):
        xf = xb_ref[b].astype(F32)
        for s in range(n_sl):
            xcat[s, pl.ds(halo + b, tl, stride=nb), :] = xf[:, s * LANES:(s + 1) * LANES]
    for blk in range(c // GATE_TILE):
        cs = slice(blk * GATE_TILE, (blk + 1) * GATE_TILE)
        parts = []
        for s in range(blk * per, (blk + 1) * per):
            ls = slice(s * LANES, (s + 1) * LANES)
            xc = bconv_ref[:, ls] + wconv_ref[0:1, ls] * xcat[s, 0:rows, :]
            for j in range(1, CONV_WIDTH):
                xc = xc + wconv_ref[j:j + 1, ls] * xcat[s, j * nb:j * nb + rows, :]
            parts.append(xc)
        a, b = _lru_gates(jnp.concatenate(parts, axis=1), cs, blk, wa_ref, ba_ref, wi_ref, bi_ref, lam_ref)
        a_s[:, cs] = a
        b_s[:, cs] = b
    for s in range(n_sl):
        xcat[s, 0:halo, :] = xcat[s, rows:rows + halo, :]

    def step(t, h):
        rs = pl.ds(pl.multiple_of(t * nb, nb), nb)
        h = a_s[rs, :] * h + b_s[rs, :]
        for s in range(n_sl):
            h_sl[s, rs, :] = h[:, s * LANES:(s + 1) * LANES]
        return h

    h = lax.fori_loop(0, tl, step, hc[...], unroll=8)
    hc[...] = h
    hlast_ref[...] = h
    for b in range(nb):
        gyb = gy_ref[b].astype(F32)
        for s in range(n_sl):
            ls = slice(s * LANES, (s + 1) * LANES)
            hg[b * tl:(b + 1) * tl, ls] = (h_sl[s, pl.ds(b, tl, stride=nb), :] * gyb[:, ls]).astype(BF16)
    out_ref[...] = _dot(hg[...], wob_ref[...]).astype(BF16).reshape(nb, tl, out_ref.shape[2])


def _lru_seq(xb, gy, w_conv, b_conv, wa, ba, wi, bi, lam, w_ob):
    bsz, seq, c = xb.shape
    tl = LRU_STEPS
    rows = bsz * tl
    tile = pl.BlockSpec((bsz, tl, c), lambda i: (0, i, 0))
    consts = [w_conv, b_conv, wa, ba, wi, bi, lam, w_ob]
    return pl.pallas_call(
        _lru_seq_kernel,
        grid=(seq // tl,),
        in_specs=[tile, tile] + [_resident(a.shape) for a in consts],
        out_specs=[pl.BlockSpec((bsz, tl, D_MODEL), lambda i: (0, i, 0)), pl.BlockSpec((bsz, c), lambda i: (0, 0))],
        out_shape=[jax.ShapeDtypeStruct((bsz, seq, D_MODEL), BF16), jax.ShapeDtypeStruct((bsz, c), F32)],
        scratch_shapes=[pltpu.VMEM((c // LANES, rows + (CONV_WIDTH - 1) * bsz, LANES), F32),
                        pltpu.VMEM((rows, c), F32), pltpu.VMEM((rows, c), F32),
                        pltpu.VMEM((c // LANES, rows, LANES), F32), pltpu.VMEM((rows, c), BF16),
                        pltpu.VMEM((bsz, c), F32)],
        compiler_params=_cparams(1),
        name="lru_seq",
    )(xb, gy, *consts)


def _lru_step_kernel(xb_ref, gy_ref, conv0_ref, h0_ref, wconv_ref, bconv_ref, wa_ref, ba_ref, wi_ref, bi_ref,
                     lam_ref, wob_ref, out_ref, hlast_ref, xcat, b_s, *, nb, tl):
    rows = nb * tl
    halo = (CONV_WIDTH - 1) * nb
    xcat[0:halo, :] = conv0_ref[...]
    xcat[halo:halo + rows, :] = xb_ref[...].astype(F32)
    xc = bconv_ref[...] + wconv_ref[0:1, :] * xcat[0:rows, :]
    for j in range(1, CONV_WIDTH):
        xc = xc + wconv_ref[j:j + 1, :] * xcat[j * nb:j * nb + rows, :]
    for blk in range(xc.shape[1] // GATE_TILE):
        cs = slice(blk * GATE_TILE, (blk + 1) * GATE_TILE)
        a, b = _lru_gates(xc[:, cs], cs, blk, wa_ref, ba_ref, wi_ref, bi_ref, lam_ref)
        h = h0_ref[:, cs]
        for t in range(tl):
            rs = slice(t * nb, (t + 1) * nb)
            h = a[rs, :] * h + b[rs, :]
            b_s[rs, cs] = h
        hlast_ref[:, cs] = h
    out_ref[...] = _dot((b_s[...] * gy_ref[...].astype(F32)).astype(BF16), wob_ref[...]).astype(BF16)


def _lru_step(xb, gy, conv0, h0, w_conv, b_conv, wa, ba, wi, bi, lam, w_ob, *, nb, tl):
    n_rows, c = xb.shape
    consts = [conv0, h0, w_conv, b_conv, wa, ba, wi, bi, lam, w_ob]
    full = lambda shape: pl.BlockSpec(shape, lambda i: (0,) * len(shape))
    return pl.pallas_call(
        functools.partial(_lru_step_kernel, nb=nb, tl=tl),
        grid=(1,),
        in_specs=[full(xb.shape), full(gy.shape)] + [full(a.shape) for a in consts],
        out_specs=[full((n_rows, D_MODEL)), full((nb, c))],
        out_shape=[jax.ShapeDtypeStruct((n_rows, D_MODEL), BF16), jax.ShapeDtypeStruct((nb, c), F32)],
        scratch_shapes=[pltpu.VMEM((n_rows + (CONV_WIDTH - 1) * nb, c), F32), pltpu.VMEM((n_rows, c), F32)],
        compiler_params=_cparams(1),
        name="lru_step",
    )(xb, gy, *consts)


FF_CHUNK = 1024
N_O_SLABS = GROUP_WIDTH // LANES


def _mix_mlp_kernel(x_ref, o1, o2, o3, l1, l2, l3, bout_ref, sga_ref, sgb_ref, pe_ref,
                    woa_ref, wo_ref, gmlp_ref, wup_ref, wdown_ref, gple_ref, wpg_ref, wpe_ref, y_ref,
                    slab_ref, *, dils):
    tm = x_ref.shape[0]

    def natural(ref, n_slabs, dil, base):
        if dil == 1:
            return [ref[:, c * LANES:(c + 1) * LANES].astype(F32) for c in range(n_slabs)]
        n = tm // dil
        for c in range(n_slabs):
            for r in range(dil):
                slab_ref[base + c, pl.ds(r, n, stride=dil), :] = ref[r, :, c * LANES:(c + 1) * LANES].astype(F32)
        return [slab_ref[base + c] for c in range(n_slabs)]

    os_, lses = [], []
    for g, (o_ref, l_ref) in enumerate(((o1, l1), (o2, l2), (o3, l3))):
        base = g * (N_O_SLABS + 1)
        os_.append(natural(o_ref, N_O_SLABS, dils[g], base))
        lses.append(natural(l_ref, 1, dils[g], base + N_O_SLABS)[0])
    top = jnp.maximum(jnp.maximum(lses[0], lses[1]), lses[2])
    ws = [jnp.exp(l - top) for l in lses]
    tot = ws[0] + ws[1] + ws[2]
    ws = [w / tot for w in ws]
    heads = []
    for hd in range(HEADS_PER_GROUP):
        acc = ws[0][:, hd:hd + 1] * os_[0][hd]
        for g in (1, 2):
            acc = acc + ws[g][:, hd:hd + 1] * os_[g][hd]
        heads.append(acc.astype(BF16))
    att = jnp.concatenate(heads, axis=1)
    a_out = _dot(att, woa_ref[...])
    mix = sga_ref[...].astype(F32) * a_out + sgb_ref[...].astype(F32) * bout_ref[...].astype(F32)
    x = x_ref[...] + _dot(mix.astype(BF16), wo_ref[...])
    h = _rms(x, gmlp_ref[...]).astype(BF16)
    acc = jnp.zeros(x.shape, F32)
    for c in range(D_FF // FF_CHUNK):
        cs = slice(c * FF_CHUNK, (c + 1) * FF_CHUNK)
        u = jnp.square(jnp.maximum(_dot(h, wup_ref[:, cs]), 0.0))
        acc = acc + _dot(u.astype(BF16), wdown_ref[cs, :])
    x = x + acc
    h = _rms(x, gple_ref[...]).astype(BF16)
    gate = jax.nn.sigmoid(_dot(h, wpg_ref[...]))
    y_ref[...] = x + gate * _dot(pe_ref[...].astype(BF16), wpe_ref[...])


def _mix_mlp_prompt(x, os_, ls, bout, sga, sgb, pe, weights):
    bsz, seq, _ = x.shape
    tm = ROW_TILE

    def nat(width):
        return pl.BlockSpec((None, tm, width), lambda b, i: (b, i, 0))

    def grouped(width):
        return [nat(width) if dil == 1 else
                pl.BlockSpec((None, dil, tm // dil, width), lambda b, i: (b, 0, i, 0))
                for dil in GROUP_DILATIONS]

    ws = list(weights)
    specs = ([nat(D_MODEL)] + grouped(GROUP_WIDTH) + grouped(HEAD_DIM) + [nat(D_MODEL)] * 3 + [nat(PLE_DIM)]
             + [_resident(w.shape) for w in ws])
    return pl.pallas_call(
        functools.partial(_mix_mlp_kernel, dils=GROUP_DILATIONS),
        grid=(bsz, seq // tm),
        in_specs=specs,
        out_specs=nat(D_MODEL),
        out_shape=jax.ShapeDtypeStruct((bsz, seq, D_MODEL), F32),
        scratch_shapes=[pltpu.VMEM((N_GROUPS * (N_O_SLABS + 1), tm, LANES), F32)],
        compiler_params=_cparams(2),
        name="mix_mlp_prompt",
    )(x, *os_, *ls, bout, sga, sgb, pe, *ws)


def _mix_mlp_sample(x, os_, ls, bout, sga, sgb, pe, weights):
    n = x.shape[0]
    tm = SAMPLE_TILE

    def rows(width):
        return pl.BlockSpec((tm, width), lambda i: (i, 0))

    ws = list(weights)
    specs = ([rows(D_MODEL)] + [rows(GROUP_WIDTH)] * 3 + [rows(HEAD_DIM)] * 3 + [rows(D_MODEL)] * 3
             + [rows(PLE_DIM)] + [_resident(w.shape) for w in ws])
    return pl.pallas_call(
        functools.partial(_mix_mlp_kernel, dils=(1, 1, 1)),
        grid=(n // tm,),
        in_specs=specs,
        out_specs=rows(D_MODEL),
        out_shape=jax.ShapeDtypeStruct((n, D_MODEL), F32),
        scratch_shapes=[pltpu.VMEM((N_GROUPS * (N_O_SLABS + 1), tm, LANES), F32)],
        compiler_params=_cparams(1),
        name="mix_mlp_sample",
    )(x, *os_, *ls, bout, sga, sgb, pe, *ws)


def _t5_bucket(dist):
    max_exact = REL_BUCKETS // 2
    d = jnp.maximum(dist, 1).astype(F32)
    large = max_exact + (jnp.log(d / max_exact) / math.log(REL_MAX_DIST / max_exact)
                         * (REL_BUCKETS - max_exact)).astype(jnp.int32)
    large = jnp.minimum(large, REL_BUCKETS - 1)
    return jnp.where(dist < max_exact, dist, large)


def _slot_bias(rel_bias, g):
    dil = GROUP_DILATIONS[g]
    dist = dil * jnp.arange(WINDOW_SLOTS + 1, dtype=jnp.int32)
    hs = slice(g * HEADS_PER_GROUP, (g + 1) * HEADS_PER_GROUP)
    return rel_bias[_t5_bucket(dist)][:, hs].astype(F32)


def _band_bias(slot_bias):
    n_h = slot_bias.shape[1]
    pad = jnp.full((n_h, Q_BLOCK - 1), NEG_INF, F32)
    ext = jnp.concatenate([pad, slot_bias[::-1].T, pad, jnp.full((n_h, 1), NEG_INF, F32)], axis=1)
    width = ext.shape[1]
    skew = jnp.broadcast_to(ext[:, None, :], (n_h, Q_BLOCK, width)).reshape(n_h, Q_BLOCK * width)
    skew = skew[:, :Q_BLOCK * (width - 1)].reshape(n_h, Q_BLOCK, width - 1)
    return skew[:, :, Q_BLOCK - 1:3 * Q_BLOCK - 1]


def _kv_rows_table(tbl):
    tbl = jnp.pad(tbl, [(0, 0)] * (tbl.ndim - 1) + [(0, KV_ROWS - tbl.shape[-1])])
    return jnp.broadcast_to(tbl[..., None], tbl.shape + (HEAD_DIM,))


def _dec_bias(slot_biases, n_t):
    n_h = HEADS_PER_GROUP
    neg = lambda n: jnp.full((n, n_h), NEG_INF, F32)
    bc, bn = [], []
    for g in range(N_GROUPS):
        sb = slot_biases[g]
        near_first = sb[::-1]
        rows_c, rows_n = [], []
        for t in range(n_t):
            if GROUP_DILATIONS[g] == 1:
                rows_c.append(jnp.concatenate([neg(t), near_first[:WINDOW_SLOTS - t]], axis=0))
                rows_n.append(jnp.concatenate([sb[:t + 1][::-1], neg(n_t - 1 - t)], axis=0))
            else:
                rows_c.append(near_first[:WINDOW_SLOTS])
                rows_n.append(jnp.concatenate([neg(t), sb[:1], neg(n_t - 1 - t)], axis=0))
        bc.append(_kv_rows_table(jnp.stack(rows_c)))
        bn.append(_kv_rows_table(jnp.stack(rows_n)))
    return jnp.stack(bc), jnp.stack(bn)


def _gate_tiles(w):
    per = GATE_TILE // LRU_BLOCK
    w = w.reshape(LRU_WIDTH // GATE_TILE, per, LRU_BLOCK, LRU_BLOCK)
    eye = jnp.eye(per, dtype=w.dtype)
    return jnp.einsum('npij,pq->npiqj', w, eye).reshape(-1, GATE_TILE, GATE_TILE).astype(BF16)


def kernel(x_prompt, x_sample, p_prompt, p_sample, cache_kv1, cache_kv2, cache_kv3, state_conv, state_lru,
           rel_bias, g_mix, w_in, g_q, g_k, w_oa, w_conv, b_conv, w_rg_a, b_rg_a, w_rg_i, b_rg_i, lam,
           w_ob, w_o, g_mlp, w_up, w_down, g_ple, w_ple_gate, w_ple_in):
    depth = w_in.shape[0]
    assert depth == 1
    bsz, seq, _ = x_prompt.shape
    bd, n_t, _ = x_sample.shape
    gw = GROUP_WIDTH
    row = lambda a: a.reshape(1, -1)

    slot_biases = [_slot_bias(rel_bias, g) for g in range(N_GROUPS)]
    band = [_band_bias(sb) for sb in slot_biases]
    bias_cache, bias_new = _dec_bias(slot_biases, n_t)

    i = 0
    w_in_b = w_in[i].astype(BF16)
    gm, gq, gk = row(g_mix[i]), row(g_q[i]), row(g_k[i])
    lru_w = (w_conv[i], row(b_conv[i]), _gate_tiles(0.5 * w_rg_a[i]), row(b_rg_a[i]), _gate_tiles(0.5 * w_rg_i[i]),
             row(b_rg_i[i]), row(lam[i]), w_ob[i].astype(BF16))
    mlp_w = (w_oa[i].astype(BF16), w_o[i].astype(BF16), row(g_mlp[i]), w_up[i].astype(BF16),
             w_down[i].astype(BF16), row(g_ple[i]), w_ple_gate[i].astype(BF16), w_ple_in[i].astype(BF16))

    (q1, q2, q3, k1, k2, k3, v1, v2, v3, xb, gy, sga, sgb, kvt1, kvt2, kvt3, xbt) = _in_proj_prompt(
        x_prompt, gm, w_in_b, gq, gk)
    os_, ls = [], []
    for g, (qq, kk, vv) in enumerate(((q1, k1, v1), (q2, k2, v2), (q3, k3, v3))):
        dil = GROUP_DILATIONS[g]
        ns, ln = bsz * dil, seq // dil
        o, l = _swa(qq.reshape(ns, ln, gw), kk.reshape(ns, ln, gw), vv.reshape(ns, ln, gw), band[g])
        os_.append(o if dil == 1 else o.reshape(bsz, dil, ln, gw))
        ls.append(l if dil == 1 else l.reshape(bsz, dil, ln, HEAD_DIM))
    bout, lru_p = _lru_seq(xb, gy, *lru_w)
    y_prompt = _mix_mlp_prompt(x_prompt, os_, ls, bout, sga, sgb, p_prompt[i], mlp_w)
    kv_p = [t.reshape(bsz, GROUP_WINDOWS[g], 2, HEADS_PER_GROUP, HEAD_DIM)[None]
            for g, t in enumerate((kvt1, kvt2, kvt3))]
    conv_p = xbt[:, SUBLANES - (CONV_WIDTH - 1):][None]
    lru_p = lru_p[None]

    n_s = bd * n_t
    (sq1, sq2, sq3, _, _, _, _, _, _, sxb, sgy, ssga, ssgb, skv1, skv2, skv3, sxbt) = _in_proj_sample(
        x_sample.reshape(n_s, D_MODEL), gm, w_in_b, gq, gk)
    qv = jnp.stack([q.astype(F32).reshape(bd, n_t, HEADS_PER_GROUP, HEAD_DIM) for q in (sq1, sq2, sq3)], axis=1)
    qv = jnp.pad(qv, ((0, 0), (0, 0), (0, 0), (0, KV_ROWS - HEADS_PER_GROUP), (0, 0)))
    fresh = jnp.stack([t.reshape(bd, n_t, KV_ROWS, HEAD_DIM) for t in (skv1, skv2, skv3)], axis=1)
    caches = [c.reshape(bd, WINDOW_SLOTS, dil * KV_ROWS, HEAD_DIM)
              for c, dil in zip((cache_kv1, cache_kv2, cache_kv3), GROUP_DILATIONS)]
    att_s = _dec_attn(qv, caches, fresh, bias_cache, bias_new).reshape(n_s, gw)

    def to_tb(a):
        return a.reshape(bd, n_t, -1).transpose(1, 0, 2).reshape(n_t * bd, -1)

    def from_tb(a):
        return a.reshape(n_t, bd, -1).transpose(1, 0, 2).reshape(bd * n_t, -1)

    conv0_s = state_conv[i].transpose(1, 0, 2).reshape((CONV_WIDTH - 1) * bd, LRU_WIDTH)
    bout_s, lru_s = _lru_step(to_tb(sxb), to_tb(sgy), conv0_s, state_lru[i].astype(F32), *lru_w, nb=bd, tl=n_t)
    zero_o = jnp.zeros((n_s, gw), BF16)
    zero_l = jnp.zeros((n_s, HEAD_DIM), F32)
    neg_l = jnp.full((n_s, HEAD_DIM), NEG_INF, F32)
    y_sample = _mix_mlp_sample(x_sample.reshape(n_s, D_MODEL), [att_s.astype(BF16), zero_o, zero_o],
                               [zero_l, neg_l, neg_l], from_tb(bout_s), ssga, ssgb,
                               p_sample[i].reshape(n_s, PLE_DIM), mlp_w).reshape(bd, n_t, D_MODEL)
    kv_s = [t.reshape(bd, n_t, 2, HEADS_PER_GROUP, HEAD_DIM)[None] for t in (skv1, skv2, skv3)]
    xcat_s = jnp.concatenate([state_conv[i], sxbt.reshape(bd, n_t, LRU_WIDTH)], axis=1)
    conv_s = xcat_s[:, n_t:][None]

    return (y_prompt, y_sample, kv_p[0], kv_p[1], kv_p[2], conv_p, lru_p,
            kv_s[0], kv_s[1], kv_s[2], conv_s, lru_s[None])
```

```python
import functools
import math

import jax
import jax.numpy as jnp
from jax import lax
from jax.experimental import pallas as pl
from jax.experimental.pallas import tpu as pltpu

F32 = jnp.float32
BF16 = jnp.bfloat16

D_MODEL = 1024
HEAD_DIM = 128
HEADS_PER_GROUP = 4
GROUP_WINDOWS = (128, 512, 2048)
GROUP_DILATIONS = (1, 4, 16)
N_GROUPS = 3
GROUP_WIDTH = HEADS_PER_GROUP * HEAD_DIM
ATT_WIDTH = N_GROUPS * GROUP_WIDTH
ATT_SCALE = HEAD_DIM ** -0.5
LRU_WIDTH = D_MODEL
LRU_BLOCKS = 16
LRU_BLOCK = LRU_WIDTH // LRU_BLOCKS
CONV_WIDTH = 4
RG_C = 8.0
D_FF = 4 * D_MODEL
PLE_DIM = 256
REL_BUCKETS = 32
REL_MAX_DIST = 2048
NORM_EPS = 1e-6
NEG_INF = -1e30
WINDOW_SLOTS = 128
KV_ROWS = 2 * HEADS_PER_GROUP

OFF_Q, OFF_K, OFF_V = 0, ATT_WIDTH, 2 * ATT_WIDTH
OFF_XB = 3 * ATT_WIDTH
OFF_YB = OFF_XB + LRU_WIDTH
OFF_GA = OFF_YB + LRU_WIDTH
OFF_GB = OFF_GA + D_MODEL

SUBLANES = 8
LANES = 128
ROW_TILE = 512
SAMPLE_TILE = 128
Q_BLOCK = 128
SWA_ROWS = 1024
LRU_STEPS = 128
GATE_TILE = 256
VMEM_LIMIT = 58 * 1024 * 1024


def _cparams(n_axes):
    return pltpu.CompilerParams(dimension_semantics=("arbitrary",) * n_axes,
                                vmem_limit_bytes=VMEM_LIMIT)


def _resident(shape):
    nd = len(shape)
    return pl.BlockSpec(shape, lambda *_: (0,) * nd, pipeline_mode=pl.Buffered(1))


def _rms(x, gain):
    return x * lax.rsqrt(jnp.mean(x * x, axis=-1, keepdims=True) + NORM_EPS) * gain


def _dot(a, b):
    return jnp.dot(a, b, preferred_element_type=F32)


def _dot_nt(a, b):
    return lax.dot_general(a, b, (((1,), (1,)), ((), ())), preferred_element_type=F32)


def _in_proj_kernel(x_ref, gmix_ref, w_ref, gq_ref, gk_ref,
                    q1, q2, q3, k1, k2, k3, v1, v2, v3, xb_ref, gy_ref, sga_ref, sgb_ref,
                    kv1, kv2, kv3, xbt_ref, slab_ref, *, dils, split):
    q_refs, k_refs, v_refs, kv_refs = (q1, q2, q3), (k1, k2, k3), (v1, v2, v3), (kv1, kv2, kv3)
    tm = x_ref.shape[0]
    tr = tm // split
    gq = gq_ref[...] * ATT_SCALE
    gk = gk_ref[...]

    def chunk(c):
        r0 = c * tr
        h = _rms(x_ref[r0:r0 + tr, :], gmix_ref[...]).astype(BF16)

        def put(ref, hd, val, dil, slab):
            cs = slice(hd * HEAD_DIM, (hd + 1) * HEAD_DIM)
            if dil == 1:
                ref[r0:r0 + tr, cs] = val.astype(BF16)
                return
            slab_ref[slab, r0:r0 + tr, :] = val
            n = tr // dil
            for r in range(dil):
                ref[r, c * n:(c + 1) * n, cs] = slab_ref[slab, pl.ds(r0 + r, n, stride=dil), :].astype(BF16)

        def put_tail(ref, row, val):
            n_tail = ref.shape[0] // KV_ROWS
            lo = max(r0, tm - n_tail)
            cnt = r0 + tr - lo
            if cnt > 0:
                ref[pl.ds((lo - (tm - n_tail)) * KV_ROWS + row, cnt, stride=KV_ROWS), :] = val[lo - r0:, :]

        def group(g):
            c0 = g * GROUP_WIDTH
            s0 = max(g - 1, 0) * 3 * HEADS_PER_GROUP
            yq = _dot(h, w_ref[:, OFF_Q + c0:OFF_Q + c0 + GROUP_WIDTH])
            yk = _dot(h, w_ref[:, OFF_K + c0:OFF_K + c0 + GROUP_WIDTH])
            yv = _dot(h, w_ref[:, OFF_V + c0:OFF_V + c0 + GROUP_WIDTH])
            for hd in range(HEADS_PER_GROUP):
                cs = slice(hd * HEAD_DIM, (hd + 1) * HEAD_DIM)
                put(q_refs[g], hd, _rms(yq[:, cs], gq), dils[g], s0 + hd)
                kn = _rms(yk[:, cs], gk)
                put(k_refs[g], hd, kn, dils[g], s0 + HEADS_PER_GROUP + hd)
                put(v_refs[g], hd, yv[:, cs], dils[g], s0 + 2 * HEADS_PER_GROUP + hd)
                put_tail(kv_refs[g], hd, kn)
                put_tail(kv_refs[g], HEADS_PER_GROUP + hd, yv[:, cs])

        def wide(off):
            return _dot(h, w_ref[:, off:off + D_MODEL])

        rs = slice(r0, r0 + tr)
        group(2)
        y = wide(OFF_XB)
        xb_ref[rs, :] = y.astype(BF16)
        n_x = xbt_ref.shape[0]
        if r0 + tr == tm:
            xbt_ref[...] = y[tr - n_x:, :]
        gy_ref[rs, :] = jax.nn.gelu(wide(OFF_YB)).astype(BF16)
        group(1)
        sga_ref[rs, :] = jax.nn.sigmoid(wide(OFF_GA)).astype(BF16)
        group(0)
        sgb_ref[rs, :] = jax.nn.sigmoid(wide(OFF_GB)).astype(BF16)

    for c in range(split):
        chunk(c)


def _in_proj_prompt(x, g_mix, w_in, g_q, g_k):
    bsz, seq, _ = x.shape
    tm = ROW_TILE
    n_tiles = seq // tm
    gw = GROUP_WIDTH

    def nat(width):
        return pl.BlockSpec((None, tm, width), lambda b, i: (b, i, 0))

    def sds(shape, dt):
        return jax.ShapeDtypeStruct(shape, dt)

    qkv_shapes, qkv_specs = [], []
    for dil in GROUP_DILATIONS:
        if dil == 1:
            qkv_shapes.append(sds((bsz, seq, gw), BF16))
            qkv_specs.append(nat(gw))
        else:
            qkv_shapes.append(sds((bsz, dil, seq // dil, gw), BF16))
            qkv_specs.append(pl.BlockSpec((None, dil, tm // dil, gw), lambda b, i: (b, 0, i, 0)))
    tail_rows = [min(w, tm) for w in GROUP_WINDOWS]
    tail_start = [n_tiles - max(w // tm, 1) for w in GROUP_WINDOWS]
    tail_specs = [pl.BlockSpec((None, r * KV_ROWS, HEAD_DIM), lambda b, i, s=s: (b, jnp.maximum(i - s, 0), 0))
                  for r, s in zip(tail_rows, tail_start)]
    out_shape = (qkv_shapes * 3 + [sds((bsz, seq, LRU_WIDTH), BF16)] * 2 + [sds((bsz, seq, D_MODEL), BF16)] * 2
                 + [sds((bsz, w * KV_ROWS, HEAD_DIM), F32) for w in GROUP_WINDOWS]
                 + [sds((bsz, SUBLANES, LRU_WIDTH), F32)])
    out_specs = (qkv_specs * 3 + [nat(LRU_WIDTH)] * 2 + [nat(D_MODEL)] * 2 + tail_specs
                 + [pl.BlockSpec((None, SUBLANES, LRU_WIDTH), lambda b, i: (b, 0, 0))])
    return pl.pallas_call(
        functools.partial(_in_proj_kernel, dils=GROUP_DILATIONS, split=IN_SPLIT),
        grid=(bsz, n_tiles),
        in_specs=[nat(D_MODEL), _resident(g_mix.shape), _resident(w_in.shape),
                  _resident(g_q.shape), _resident(g_k.shape)],
        out_specs=out_specs,
        out_shape=out_shape,
        scratch_shapes=[pltpu.VMEM(((N_GROUPS - 1) * 3 * HEADS_PER_GROUP, tm, HEAD_DIM), F32)],
        compiler_params=_cparams(2),
        name="in_proj_prompt",
    )(x, g_mix, w_in, g_q, g_k)


def _in_proj_sample(x, g_mix, w_in, g_q, g_k):
    n = x.shape[0]
    tm = SAMPLE_TILE
    gw = GROUP_WIDTH

    def rows(width):
        return pl.BlockSpec((tm, width), lambda i: (i, 0))

    def sds(width, dt):
        return jax.ShapeDtypeStruct((n, width), dt)

    out_shape = ([sds(gw, BF16)] * 9 + [sds(LRU_WIDTH, BF16)] * 2 + [sds(D_MODEL, BF16)] * 2
                 + [jax.ShapeDtypeStruct((n * KV_ROWS, HEAD_DIM), F32)] * 3 + [sds(LRU_WIDTH, F32)])
    out_specs = ([rows(gw)] * 9 + [rows(LRU_WIDTH)] * 2 + [rows(D_MODEL)] * 2
                 + [pl.BlockSpec((tm * KV_ROWS, HEAD_DIM), lambda i: (i, 0))] * 3 + [rows(LRU_WIDTH)])
    return pl.pallas_call(
        functools.partial(_in_proj_kernel, dils=(1, 1, 1), split=1),
        grid=(n // tm,),
        in_specs=[rows(D_MODEL), _resident(g_mix.shape), _resident(w_in.shape),
                  _resident(g_q.shape), _resident(g_k.shape)],
        out_specs=out_specs,
        out_shape=out_shape,
        scratch_shapes=[pltpu.VMEM(((N_GROUPS - 1) * 3 * HEADS_PER_GROUP, tm, HEAD_DIM), F32)],
        compiler_params=_cparams(1),
        name="in_proj_sample",
    )(x, g_mix, w_in, g_q, g_k)


def _swa_kernel(q_ref, k_ref, v_ref, kp_ref, vp_ref, bias_ref, o_ref, st_ref):
    first = pl.program_id(1) == 0
    n_seq = q_ref.shape[0]
    n_blk = q_ref.shape[1] // Q_BLOCK
    col = lax.broadcasted_iota(jnp.int32, (Q_BLOCK, 2 * Q_BLOCK), 1)
    lane = lax.broadcasted_iota(jnp.int32, (Q_BLOCK, HEAD_DIM), 1)
    ones = jnp.ones((2 * Q_BLOCK, HEAD_DIM), BF16)
    for sq in range(n_seq):
        for j in range(n_blk):
            rq = slice(j * Q_BLOCK, (j + 1) * Q_BLOCK)
            st_tile = jnp.zeros((Q_BLOCK, HEAD_DIM), F32)
            for hd in range(HEADS_PER_GROUP):
                cs = slice(hd * HEAD_DIM, (hd + 1) * HEAD_DIM)
                q = q_ref[sq, rq, cs]
                if j == 0:
                    kk = jnp.concatenate([kp_ref[sq, :, cs], k_ref[sq, rq, cs]], axis=0)
                    vv = jnp.concatenate([vp_ref[sq, :, cs], v_ref[sq, rq, cs]], axis=0)
                else:
                    rk = slice((j - 1) * Q_BLOCK, (j + 1) * Q_BLOCK)
                    kk = k_ref[sq, rk, cs]
                    vv = v_ref[sq, rk, cs]
                s = _dot_nt(q, kk) + bias_ref[hd]
                if j == 0:
                    s = jnp.where(col < jnp.where(first, Q_BLOCK, 0), NEG_INF, s)
                m = jnp.max(s, axis=-1, keepdims=True)
                p = jnp.exp(s - m).astype(BF16)
                oe = _dot(p, jnp.concatenate([vv, ones], axis=1))
                o_ref[sq, rq, cs] = oe[:, :HEAD_DIM].astype(BF16)
                st_tile = jnp.where(lane == hd, m, st_tile)
                st_tile = jnp.where(lane == HEADS_PER_GROUP + hd, oe[:, HEAD_DIM:], st_tile)
            st_ref[sq, rq, :] = st_tile


def _swa(q, k, v, band_bias):
    n_seq, length, gw = q.shape
    tq = min(length, SWA_ROWS)
    sb = SWA_ROWS // tq
    ratio = tq // Q_BLOCK
    cur = pl.BlockSpec((sb, tq, gw), lambda s, i: (s, i, 0))
    prev = pl.BlockSpec((sb, Q_BLOCK, gw), lambda s, i: (s, jnp.maximum(i * ratio - 1, 0), 0))
    return pl.pallas_call(
        _swa_kernel,
        grid=(n_seq // sb, length // tq),
        in_specs=[cur, cur, cur, prev, prev, _resident(band_bias.shape)],
        out_specs=[cur, pl.BlockSpec((sb, tq, HEAD_DIM), lambda s, i: (s, i, 0))],
        out_shape=[jax.ShapeDtypeStruct((n_seq, length, gw), BF16),
                   jax.ShapeDtypeStruct((n_seq, length, HEAD_DIM), F32)],
        compiler_params=_cparams(2),
        name="swa",
    )(q, k, v, k, v, band_bias)


DEC_BATCH_TILE = 2


def _dec_attn_kernel(q_ref, c1, c2, c3, new_ref, bc_ref, bn_ref, att_ref):
    c_refs = (c1, c2, c3)
    n_b, _, n_t = q_ref.shape[:3]
    half = KV_ROWS // 2
    for bi in range(n_b):
        for t in range(n_t):
            accs, lses, dens = [], [], []
            for g in range(N_GROUPS):
                rows = slice(0, KV_ROWS) if GROUP_DILATIONS[g] == 1 else slice(t * KV_ROWS, (t + 1) * KV_ROWS)
                tile = c_refs[g][bi, :, rows, :]
                fresh = new_ref[bi, g]
                qv = q_ref[bi, g, t]
                lc = jnp.sum(tile * qv[None], axis=-1, keepdims=True) + bc_ref[g, t]
                ln = jnp.sum(fresh * qv[None], axis=-1, keepdims=True) + bn_ref[g, t]
                m = jnp.maximum(jnp.max(lc, axis=0), jnp.max(ln, axis=0))
                pc = jnp.exp(lc - m[None])
                pn = jnp.exp(ln - m[None])
                den = jnp.sum(pc, axis=0) + jnp.sum(pn, axis=0)
                acc = (jnp.sum(pc * pltpu.roll(tile, half, axis=1), axis=0)
                       + jnp.sum(pn * pltpu.roll(fresh, half, axis=1), axis=0))
                accs.append(acc)
                dens.append(den)
                lses.append(m + jnp.log(den))
            top = jnp.maximum(jnp.maximum(lses[0], lses[1]), lses[2])
            ws = [jnp.exp(l - top) for l in lses]
            tot = ws[0] + ws[1] + ws[2]
            out = (ws[0] / tot / dens[0]) * accs[0]
            for g in (1, 2):
                out = out + (ws[g] / tot / dens[g]) * accs[g]
            att_ref[bi, t] = out[0:half, :]


def _dec_attn(qv, caches, fresh, bias_cache, bias_new):
    bd, _, n_t = qv.shape[:3]
    bb = DEC_BATCH_TILE
    cspecs = []
    for g, dil in enumerate(GROUP_DILATIONS):
        rows = KV_ROWS if dil == 1 else n_t * KV_ROWS
        cspecs.append(pl.BlockSpec((bb, WINDOW_SLOTS, rows, HEAD_DIM), lambda i: (i, 0, 0, 0)))
    small = pl.BlockSpec((bb, N_GROUPS, n_t, KV_ROWS, HEAD_DIM), lambda i: (i, 0, 0, 0, 0))
    return pl.pallas_call(
        _dec_attn_kernel,
        grid=(bd // bb,),
        in_specs=[small] + cspecs + [small, _resident(bias_cache.shape), _resident(bias_new.shape)],
        out_specs=pl.BlockSpec((bb, n_t, HEADS_PER_GROUP, HEAD_DIM), lambda i: (i, 0, 0, 0)),
        out_shape=jax.ShapeDtypeStruct((bd, n_t, HEADS_PER_GROUP, HEAD_DIM), F32),
        compiler_params=_cparams(1),
        name="dec_attn",
    )(qv, *caches, fresh, bias_cache, bias_new)


def _lru_gates(xc, cs, blk, wa_ref, ba_ref, wi_ref, bi_ref, lam_ref):
    xcb = xc.astype(BF16)
    tr = jnp.tanh(_dot(xcb, wa_ref[blk]) + 0.5 * ba_ref[:, cs])
    ti = jnp.tanh(_dot(xcb, wi_ref[blk]) + 0.5 * bi_ref[:, cs])
    lam = lam_ref[:, cs]
    log_sig = jnp.minimum(lam, 0.0) - jnp.log1p(jnp.exp(-jnp.abs(lam)))
    half_c = (0.5 * RG_C) * log_sig
    log_a = half_c * tr + half_c
    a = jnp.exp(log_a)
    hx = 0.5 * xc
    b = jnp.sqrt(-jnp.tanh(log_a) * (1.0 + a * a)) * (hx * ti + hx)
    return a, b


def _lru_seq_kernel(xb_ref, gy_ref, wconv_ref, bconv_ref, wa_ref, ba_ref, wi_ref, bi_ref, lam_ref, wob_ref,
                    out_ref, hlast_ref, xcat, a_s, b_s, h_sl, hg, hc):
    nb, tl, c = xb_ref.shape
    rows = nb * tl
    halo = (CONV_WIDTH - 1) * nb
    n_sl = c // LANES
    per = GATE_TILE // LANES

    @pl.when(pl.program_id(0) == 0)
    def _():
        xcat[:, 0:halo, :] = jnp.zeros((n_sl, halo, LANES), F32)
        hc[...] = jnp.zeros((nb, c), F32)

    for b in range(nb):
        xf = xb_ref[b].astype(F32)
        for s in range(n_sl):
            xcat[s, pl.ds(halo + b, tl, stride=nb), :] = xf[:, s * LANES:(s + 1) * LANES]
    for blk in range(c // GATE_TILE):
        cs = slice(blk * GATE_TILE, (blk + 1) * GATE_TILE)
        parts = []
        for s in range(blk * per, (blk + 1) * per):
            ls = slice(s * LANES, (s + 1) * LANES)
            xc = bconv_ref[:, ls] + wconv_ref[0:1, ls] * xcat[s, 0:rows, :]
            for j in range(1, CONV_WIDTH):
                xc = xc + wconv_ref[j:j + 1, ls] * xcat[s, j * nb:j * nb + rows, :]
            parts.append(xc)
        a, b = _lru_gates(jnp.concatenate(parts, axis=1), cs, blk, wa_ref, ba_ref, wi_ref, bi_ref, lam_ref)
        a_s[:, cs] = a
        b_s[:, cs] = b
    for s in range(n_sl):
        xcat[s, 0:halo, :] = xcat[s, rows:rows + halo, :]

    def step(t, h):
        rs = pl.ds(pl.multiple_of(t * nb, nb), nb)
        h = a_s[rs, :] * h + b_s[rs, :]
        for s in range(n_sl):
            h_sl[s, rs, :] = h[:, s * LANES:(s + 1) * LANES]
        return h

    h = lax.fori_loop(0, tl, step, hc[...], unroll=8)
    hc[...] = h
    hlast_ref[...] = h
    for b in range(nb):
        gyb = gy_ref[b].astype(F32)
        for s in range(n_sl):
            ls = slice(s * LANES, (s + 1) * LANES)
            hg[b * tl:(b + 1) * tl, ls] = (h_sl[s, pl.ds(b, tl, stride=nb), :] * gyb[:, ls]).astype(BF16)
    out_ref[...] = _dot(hg[...], wob_ref[...]).astype(BF16).reshape(nb, tl, out_ref.shape[2])


def _lru_seq(xb, gy, w_conv, b_conv, wa, ba, wi, bi, lam, w_ob):
    bsz, seq, c = xb.shape
    tl = LRU_STEPS
    rows = bsz * tl
    tile = pl.BlockSpec((bsz, tl, c), lambda i: (0, i, 0))
    consts = [w_conv, b_conv, wa, ba, wi, bi, lam, w_ob]
    return pl.pallas_call(
        _lru_seq_kernel,
        grid=(seq // tl,),
        in_specs=[tile, tile] + [_resident(a.shape) for a in consts],
        out_specs=[pl.BlockSpec((bsz, tl, D_MODEL), lambda i: (0, i, 0)), pl.BlockSpec((bsz, c), lambda i: (0, 0))],
        out_shape=[jax.ShapeDtypeStruct((bsz, seq, D_MODEL), BF16), jax.ShapeDtypeStruct((bsz, c), F32)],
        scratch_shapes=[pltpu.VMEM((c // LANES, rows + (CONV_WIDTH - 1) * bsz, LANES), F32),
                        pltpu.VMEM((rows, c), F32), pltpu.VMEM((rows, c), F32),
                        pltpu.VMEM((c // LANES, rows, LANES), F32), pltpu.VMEM((rows, c), BF16),
                        pltpu.VMEM((bsz, c), F32)],
        compiler_params=_cparams(1),
        name="lru_seq",
    )(xb, gy, *consts)


def _lru_step_kernel(xb_ref, gy_ref, conv0_ref, h0_ref, wconv_ref, bconv_ref, wa_ref, ba_ref, wi_ref, bi_ref,
                     lam_ref, wob_ref, out_ref, hlast_ref, xcat, b_s, *, nb, tl):
    rows = nb * tl
    halo = (CONV_WIDTH - 1) * nb
    xcat[0:halo, :] = conv0_ref[...]
    xcat[halo:halo + rows, :] = xb_ref[...].astype(F32)
    xc = bconv_ref[...] + wconv_ref[0:1, :] * xcat[0:rows, :]
    for j in range(1, CONV_WIDTH):
        xc = xc + wconv_ref[j:j + 1, :] * xcat[j * nb:j * nb + rows, :]
    for blk in range(xc.shape[1] // GATE_TILE):
        cs = slice(blk * GATE_TILE, (blk + 1) * GATE_TILE)
        a, b = _lru_gates(xc[:, cs], cs, blk, wa_ref, ba_ref, wi_ref, bi_ref, lam_ref)
        h = h0_ref[:, cs]
        for t in range(tl):
            rs = slice(t * nb, (t + 1) * nb)
            h = a[rs, :] * h + b[rs, :]
            b_s[rs, cs] = h
        hlast_ref[:, cs] = h
    out_ref[...] = _dot((b_s[...] * gy_ref[...].astype(F32)).astype(BF16), wob_ref[...]).astype(BF16)


def _lru_step(xb, gy, conv0, h0, w_conv, b_conv, wa, ba, wi, bi, lam, w_ob, *, nb, tl):
    n_rows, c = xb.shape
    consts = [conv0, h0, w_conv, b_conv, wa, ba, wi, bi, lam, w_ob]
    full = lambda shape: pl.BlockSpec(shape, lambda i: (0,) * len(shape))
    return pl.pallas_call(
        functools.partial(_lru_step_kernel, nb=nb, tl=tl),
        grid=(1,),
        in_specs=[full(xb.shape), full(gy.shape)] + [full(a.shape) for a in consts],
        out_specs=[full((n_rows, D_MODEL)), full((nb, c))],
        out_shape=[jax.ShapeDtypeStruct((n_rows, D_MODEL), BF16), jax.ShapeDtypeStruct((nb, c), F32)],
        scratch_shapes=[pltpu.VMEM((n_rows + (CONV_WIDTH - 1) * nb, c), F32), pltpu.VMEM((n_rows, c), F32)],
        compiler_params=_cparams(1),
        name="lru_step",
    )(xb, gy, *consts)


FF_CHUNK = 1024
IN_SPLIT = 2
MIX_SPLIT = 2
N_O_SLABS = GROUP_WIDTH // LANES


def _mix_mlp_kernel(x_ref, o1, o2, o3, l1, l2, l3, bout_ref, sga_ref, sgb_ref, pe_ref,
                    woa_ref, wo_ref, gmlp_ref, wup_ref, wdown_ref, gple_ref, wpg_ref, wpe_ref, y_ref,
                    slab_ref, *, dils, split):
    tm = x_ref.shape[0]

    def natural(ref, n_slabs, dil, base):
        if dil == 1:
            return lambda rs: [ref[rs, c * LANES:(c + 1) * LANES].astype(F32) for c in range(n_slabs)]
        n = tm // dil
        for c in range(n_slabs):
            for r in range(dil):
                slab_ref[base + c, pl.ds(r, n, stride=dil), :] = ref[r, :, c * LANES:(c + 1) * LANES].astype(F32)
        return lambda rs: [slab_ref[base + c, rs, :] for c in range(n_slabs)]

    o_get, l_get = [], []
    for g, (o_ref, l_ref) in enumerate(((o1, l1), (o2, l2), (o3, l3))):
        base = g * (N_O_SLABS + 1)
        o_get.append(natural(o_ref, N_O_SLABS, dils[g], base))
        l_get.append(natural(l_ref, 1, dils[g], base + N_O_SLABS))

    def rows_body(rs):
        os_ = [get(rs) for get in o_get]
        sts = [get(rs)[0] for get in l_get]
        dens = [pltpu.roll(st, LANES - HEADS_PER_GROUP, axis=1) for st in sts]
        top = jnp.maximum(jnp.maximum(sts[0], sts[1]), sts[2])
        ws = [jnp.exp(st - top) for st in sts]
        tot = ws[0] * dens[0] + ws[1] * dens[1] + ws[2] * dens[2]
        ws = [w / tot for w in ws]
        heads = []
        for hd in range(HEADS_PER_GROUP):
            acc = ws[0][:, hd:hd + 1] * os_[0][hd]
            for g in (1, 2):
                acc = acc + ws[g][:, hd:hd + 1] * os_[g][hd]
            heads.append(acc.astype(BF16))
        att = jnp.concatenate(heads, axis=1)
        yield
        a_out = _dot(att, woa_ref[...])
        mix = (sga_ref[rs, :].astype(F32) * a_out
               + sgb_ref[rs, :].astype(F32) * bout_ref[rs, :].astype(F32))
        yield
        x = x_ref[rs, :] + _dot(mix.astype(BF16), wo_ref[...])
        h = _rms(x, gmlp_ref[...]).astype(BF16)
        yield
        acc = jnp.zeros(x.shape, F32)
        for c in range(D_FF // FF_CHUNK):
            cs = slice(c * FF_CHUNK, (c + 1) * FF_CHUNK)
            u = jnp.square(jnp.maximum(_dot(h, wup_ref[:, cs]), 0.0))
            acc = acc + _dot(u.astype(BF16), wdown_ref[cs, :])
            yield
        x = x + acc
        h = _rms(x, gple_ref[...]).astype(BF16)
        yield
        gate = jax.nn.sigmoid(_dot(h, wpg_ref[...]))
        y_ref[rs, :] = x + gate * _dot(pe_ref[rs, :].astype(BF16), wpe_ref[...])

    n_rows = tm // split
    chunks = [rows_body(slice(part * n_rows, (part + 1) * n_rows)) for part in range(split)]
    while chunks:
        chunks = [c for c in chunks if next(c, True) is None]


def _mix_mlp_prompt(x, os_, ls, bout, sga, sgb, pe, weights):
    bsz, seq, _ = x.shape
    tm = ROW_TILE

    def nat(width):
        return pl.BlockSpec((None, tm, width), lambda b, i: (b, i, 0))

    def grouped(width):
        return [nat(width) if dil == 1 else
                pl.BlockSpec((None, dil, tm // dil, width), lambda b, i: (b, 0, i, 0))
                for dil in GROUP_DILATIONS]

    ws = list(weights)
    specs = ([nat(D_MODEL)] + grouped(GROUP_WIDTH) + grouped(HEAD_DIM) + [nat(D_MODEL)] * 3 + [nat(PLE_DIM)]
             + [_resident(w.shape) for w in ws])
    return pl.pallas_call(
        functools.partial(_mix_mlp_kernel, dils=GROUP_DILATIONS, split=MIX_SPLIT),
        grid=(bsz, seq // tm),
        in_specs=specs,
        out_specs=nat(D_MODEL),
        out_shape=jax.ShapeDtypeStruct((bsz, seq, D_MODEL), F32),
        scratch_shapes=[pltpu.VMEM((N_GROUPS * (N_O_SLABS + 1), tm, LANES), F32)],
        compiler_params=_cparams(2),
        name="mix_mlp_prompt",
    )(x, *os_, *ls, bout, sga, sgb, pe, *ws)


def _mix_mlp_sample(x, os_, ls, bout, sga, sgb, pe, weights):
    n = x.shape[0]
    tm = SAMPLE_TILE

    def rows(width):
        return pl.BlockSpec((tm, width), lambda i: (i, 0))

    ws = list(weights)
    specs = ([rows(D_MODEL)] + [rows(GROUP_WIDTH)] * 3 + [rows(HEAD_DIM)] * 3 + [rows(D_MODEL)] * 3
             + [rows(PLE_DIM)] + [_resident(w.shape) for w in ws])
    return pl.pallas_call(
        functools.partial(_mix_mlp_kernel, dils=(1, 1, 1), split=1),
        grid=(n // tm,),
        in_specs=specs,
        out_specs=rows(D_MODEL),
        out_shape=jax.ShapeDtypeStruct((n, D_MODEL), F32),
        scratch_shapes=[pltpu.VMEM((N_GROUPS * (N_O_SLABS + 1), tm, LANES), F32)],
        compiler_params=_cparams(1),
        name="mix_mlp_sample",
    )(x, *os_, *ls, bout, sga, sgb, pe, *ws)


def _t5_bucket(dist):
    max_exact = REL_BUCKETS // 2
    d = jnp.maximum(dist, 1).astype(F32)
    large = max_exact + (jnp.log(d / max_exact) / math.log(REL_MAX_DIST / max_exact)
                         * (REL_BUCKETS - max_exact)).astype(jnp.int32)
    large = jnp.minimum(large, REL_BUCKETS - 1)
    return jnp.where(dist < max_exact, dist, large)


def _slot_bias(rel_bias, g):
    dil = GROUP_DILATIONS[g]
    dist = dil * jnp.arange(WINDOW_SLOTS + 1, dtype=jnp.int32)
    hs = slice(g * HEADS_PER_GROUP, (g + 1) * HEADS_PER_GROUP)
    return rel_bias[_t5_bucket(dist)][:, hs].astype(F32)


def _band_bias(slot_bias):
    n_h = slot_bias.shape[1]
    pad = jnp.full((n_h, Q_BLOCK - 1), NEG_INF, F32)
    ext = jnp.concatenate([pad, slot_bias[::-1].T, pad, jnp.full((n_h, 1), NEG_INF, F32)], axis=1)
    width = ext.shape[1]
    skew = jnp.broadcast_to(ext[:, None, :], (n_h, Q_BLOCK, width)).reshape(n_h, Q_BLOCK * width)
    skew = skew[:, :Q_BLOCK * (width - 1)].reshape(n_h, Q_BLOCK, width - 1)
    return skew[:, :, Q_BLOCK - 1:3 * Q_BLOCK - 1]


def _kv_rows_table(tbl):
    tbl = jnp.pad(tbl, [(0, 0)] * (tbl.ndim - 1) + [(0, KV_ROWS - tbl.shape[-1])])
    return jnp.broadcast_to(tbl[..., None], tbl.shape + (HEAD_DIM,))


def _dec_bias(slot_biases, n_t):
    n_h = HEADS_PER_GROUP
    neg = lambda n: jnp.full((n, n_h), NEG_INF, F32)
    bc, bn = [], []
    for g in range(N_GROUPS):
        sb = slot_biases[g]
        near_first = sb[::-1]
        rows_c, rows_n = [], []
        for t in range(n_t):
            if GROUP_DILATIONS[g] == 1:
                rows_c.append(jnp.concatenate([neg(t), near_first[:WINDOW_SLOTS - t]], axis=0))
                rows_n.append(jnp.concatenate([sb[:t + 1][::-1], neg(n_t - 1 - t)], axis=0))
            else:
                rows_c.append(near_first[:WINDOW_SLOTS])
                rows_n.append(jnp.concatenate([neg(t), sb[:1], neg(n_t - 1 - t)], axis=0))
        bc.append(_kv_rows_table(jnp.stack(rows_c)))
        bn.append(_kv_rows_table(jnp.stack(rows_n)))
    return jnp.stack(bc), jnp.stack(bn)


def _gate_tiles(w):
    per = GATE_TILE // LRU_BLOCK
    w = w.reshape(LRU_WIDTH // GATE_TILE, per, LRU_BLOCK, LRU_BLOCK)
    eye = jnp.eye(per, dtype=w.dtype)
    return jnp.einsum('npij,pq->npiqj', w, eye).reshape(-1, GATE_TILE, GATE_TILE).astype(BF16)


def kernel(x_prompt, x_sample, p_prompt, p_sample, cache_kv1, cache_kv2, cache_kv3, state_conv, state_lru,
           rel_bias, g_mix, w_in, g_q, g_k, w_oa, w_conv, b_conv, w_rg_a, b_rg_a, w_rg_i, b_rg_i, lam,
           w_ob, w_o, g_mlp, w_up, w_down, g_ple, w_ple_gate, w_ple_in):
    depth = w_in.shape[0]
    assert depth == 1
    bsz, seq, _ = x_prompt.shape
    bd, n_t, _ = x_sample.shape
    gw = GROUP_WIDTH
    row = lambda a: a.reshape(1, -1)

    slot_biases = [_slot_bias(rel_bias, g) for g in range(N_GROUPS)]
    band = [_band_bias(sb) for sb in slot_biases]
    bias_cache, bias_new = _dec_bias(slot_biases, n_t)

    i = 0
    w_in_b = w_in[i].astype(BF16)
    gm, gq, gk = row(g_mix[i]), row(g_q[i]), row(g_k[i])
    lru_w = (w_conv[i], row(b_conv[i]), _gate_tiles(0.5 * w_rg_a[i]), row(b_rg_a[i]), _gate_tiles(0.5 * w_rg_i[i]),
             row(b_rg_i[i]), row(lam[i]), w_ob[i].astype(BF16))
    mlp_w = (w_oa[i].astype(BF16), w_o[i].astype(BF16), row(g_mlp[i]), w_up[i].astype(BF16),
             w_down[i].astype(BF16), row(g_ple[i]), w_ple_gate[i].astype(BF16), w_ple_in[i].astype(BF16))

    (q1, q2, q3, k1, k2, k3, v1, v2, v3, xb, gy, sga, sgb, kvt1, kvt2, kvt3, xbt) = _in_proj_prompt(
        x_prompt, gm, w_in_b, gq, gk)
    os_, ls = [], []
    for g, (qq, kk, vv) in enumerate(((q1, k1, v1), (q2, k2, v2), (q3, k3, v3))):
        dil = GROUP_DILATIONS[g]
        ns, ln = bsz * dil, seq // dil
        o, l = _swa(qq.reshape(ns, ln, gw), kk.reshape(ns, ln, gw), vv.reshape(ns, ln, gw), band[g])
        os_.append(o if dil == 1 else o.reshape(bsz, dil, ln, gw))
        ls.append(l if dil == 1 else l.reshape(bsz, dil, ln, HEAD_DIM))
    bout, lru_p = _lru_seq(xb, gy, *lru_w)
    y_prompt = _mix_mlp_prompt(x_prompt, os_, ls, bout, sga, sgb, p_prompt[i], mlp_w)
    kv_p = [t.reshape(bsz, GROUP_WINDOWS[g], 2, HEADS_PER_GROUP, HEAD_DIM)[None]
            for g, t in enumerate((kvt1, kvt2, kvt3))]
    conv_p = xbt[:, SUBLANES - (CONV_WIDTH - 1):][None]
    lru_p = lru_p[None]

    n_s = bd * n_t
    (sq1, sq2, sq3, _, _, _, _, _, _, sxb, sgy, ssga, ssgb, skv1, skv2, skv3, sxbt) = _in_proj_sample(
        x_sample.reshape(n_s, D_MODEL), gm, w_in_b, gq, gk)
    qv = jnp.stack([q.astype(F32).reshape(bd, n_t, HEADS_PER_GROUP, HEAD_DIM) for q in (sq1, sq2, sq3)], axis=1)
    qv = jnp.pad(qv, ((0, 0), (0, 0), (0, 0), (0, KV_ROWS - HEADS_PER_GROUP), (0, 0)))
    fresh = jnp.stack([t.reshape(bd, n_t, KV_ROWS, HEAD_DIM) for t in (skv1, skv2, skv3)], axis=1)
    caches = [c.reshape(bd, WINDOW_SLOTS, dil * KV_ROWS, HEAD_DIM)
              for c, dil in zip((cache_kv1, cache_kv2, cache_kv3), GROUP_DILATIONS)]
    att_s = _dec_attn(qv, caches, fresh, bias_cache, bias_new).reshape(n_s, gw)

    def to_tb(a):
        return a.reshape(bd, n_t, -1).transpose(1, 0, 2).reshape(n_t * bd, -1)

    def from_tb(a):
        return a.reshape(n_t, bd, -1).transpose(1, 0, 2).reshape(bd * n_t, -1)

    conv0_s = state_conv[i].transpose(1, 0, 2).reshape((CONV_WIDTH - 1) * bd, LRU_WIDTH)
    bout_s, lru_s = _lru_step(to_tb(sxb), to_tb(sgy), conv0_s, state_lru[i].astype(F32), *lru_w, nb=bd, tl=n_t)
    zero_o = jnp.zeros((n_s, gw), BF16)
    lane = jnp.arange(HEAD_DIM)[None, :]
    den_one = jnp.where(lane < HEADS_PER_GROUP, 0.0, 1.0)
    st_on = jnp.broadcast_to(den_one, (n_s, HEAD_DIM)).astype(F32)
    st_off = jnp.broadcast_to(jnp.where(lane < HEADS_PER_GROUP, NEG_INF, den_one), (n_s, HEAD_DIM)).astype(F32)
    y_sample = _mix_mlp_sample(x_sample.reshape(n_s, D_MODEL), [att_s.astype(BF16), zero_o, zero_o],
                               [st_on, st_off, st_off], from_tb(bout_s), ssga, ssgb,
                               p_sample[i].reshape(n_s, PLE_DIM), mlp_w).reshape(bd, n_t, D_MODEL)
    kv_s = [t.reshape(bd, n_t, 2, HEADS_PER_GROUP, HEAD_DIM)[None] for t in (skv1, skv2, skv3)]
    xcat_s = jnp.concatenate([state_conv[i], sxbt.reshape(bd, n_t, LRU_WIDTH)], axis=1)
    conv_s = xcat_s[:, n_t:][None]

    return (y_prompt, y_sample, kv_p[0], kv_p[1], kv_p[2], conv_p, lru_p,
            kv_s[0], kv_s[1], kv_s[2], conv_s, lru_s[None])
```

```python
import functools
import math

import jax
import jax.numpy as jnp
from jax import lax
from jax.experimental import pallas as pl
from jax.experimental.pallas import tpu as pltpu

F32 = jnp.float32
BF16 = jnp.bfloat16

D_MODEL = 1024
HEAD_DIM = 128
HEADS_PER_GROUP = 4
GROUP_WINDOWS = (128, 512, 2048)
GROUP_DILATIONS = (1, 4, 16)
N_GROUPS = 3
GROUP_WIDTH = HEADS_PER_GROUP * HEAD_DIM
ATT_WIDTH = N_GROUPS * GROUP_WIDTH
ATT_SCALE = HEAD_DIM ** -0.5
LRU_WIDTH = D_MODEL
LRU_BLOCKS = 16
LRU_BLOCK = LRU_WIDTH // LRU_BLOCKS
CONV_WIDTH = 4
RG_C = 8.0
D_FF = 4 * D_MODEL
PLE_DIM = 256
REL_BUCKETS = 32
REL_MAX_DIST = 2048
NORM_EPS = 1e-6
NEG_INF = -1e30
WINDOW_SLOTS = 128
KV_ROWS = 2 * HEADS_PER_GROUP

OFF_Q, OFF_K, OFF_V = 0, ATT_WIDTH, 2 * ATT_WIDTH
OFF_XB = 3 * ATT_WIDTH
OFF_YB = OFF_XB + LRU_WIDTH
OFF_GA = OFF_YB + LRU_WIDTH
OFF_GB = OFF_GA + D_MODEL

SUBLANES = 8
LANES = 128
ROW_TILE = 512
SAMPLE_TILE = 128
Q_BLOCK = 128
SWA_ROWS = 1024
LRU_STEPS = 128
GATE_TILE = 256
VMEM_LIMIT = 58 * 1024 * 1024


def _cparams(n_axes):
    return pltpu.CompilerParams(dimension_semantics=("arbitrary",) * n_axes,
                                vmem_limit_bytes=VMEM_LIMIT)


def _resident(shape):
    nd = len(shape)
    return pl.BlockSpec(shape, lambda *_: (0,) * nd, pipeline_mode=pl.Buffered(1))


def _rms(x, gain):
    return x * lax.rsqrt(jnp.mean(x * x, axis=-1, keepdims=True) + NORM_EPS) * gain


def _dot(a, b):
    return jnp.dot(a, b, preferred_element_type=F32)


def _dot_nt(a, b):
    return lax.dot_general(a, b, (((1,), (1,)), ((), ())), preferred_element_type=F32)


def _in_proj_kernel(x_ref, gmix_ref, w_ref, gq_ref, gk_ref,
                    q1, q2, q3, k1, k2, k3, v1, v2, v3, xb_ref, gy_ref, sga_ref, sgb_ref,
                    kv1, kv2, kv3, xbt_ref, slab_ref, *, dils, split):
    q_refs, k_refs, v_refs, kv_refs = (q1, q2, q3), (k1, k2, k3), (v1, v2, v3), (kv1, kv2, kv3)
    tm = x_ref.shape[0]
    tr = tm // split
    gq = gq_ref[...] * ATT_SCALE
    gk = gk_ref[...]

    def chunk(c):
        r0 = c * tr
        h = _rms(x_ref[r0:r0 + tr, :], gmix_ref[...]).astype(BF16)

        def put(ref, hd, val, dil, slab):
            cs = slice(hd * HEAD_DIM, (hd + 1) * HEAD_DIM)
            if dil == 1:
                ref[r0:r0 + tr, cs] = val.astype(BF16)
                return
            slab_ref[slab, r0:r0 + tr, :] = val
            n = tr // dil
            for r in range(dil):
                ref[r, c * n:(c + 1) * n, cs] = slab_ref[slab, pl.ds(r0 + r, n, stride=dil), :].astype(BF16)

        def put_tail(ref, row, val):
            n_tail = ref.shape[0] // KV_ROWS
            lo = max(r0, tm - n_tail)
            cnt = r0 + tr - lo
            if cnt > 0:
                ref[pl.ds((lo - (tm - n_tail)) * KV_ROWS + row, cnt, stride=KV_ROWS), :] = val[lo - r0:, :]

        def group(g):
            c0 = g * GROUP_WIDTH
            s0 = max(g - 1, 0) * 3 * HEADS_PER_GROUP
            yq = _dot(h, w_ref[:, OFF_Q + c0:OFF_Q + c0 + GROUP_WIDTH])
            yk = _dot(h, w_ref[:, OFF_K + c0:OFF_K + c0 + GROUP_WIDTH])
            yv = _dot(h, w_ref[:, OFF_V + c0:OFF_V + c0 + GROUP_WIDTH])
            for hd in range(HEADS_PER_GROUP):
                cs = slice(hd * HEAD_DIM, (hd + 1) * HEAD_DIM)
                put(q_refs[g], hd, _rms(yq[:, cs], gq), dils[g], s0 + hd)
                kn = _rms(yk[:, cs], gk)
                put(k_refs[g], hd, kn, dils[g], s0 + HEADS_PER_GROUP + hd)
                put(v_refs[g], hd, yv[:, cs], dils[g], s0 + 2 * HEADS_PER_GROUP + hd)
                put_tail(kv_refs[g], hd, kn)
                put_tail(kv_refs[g], HEADS_PER_GROUP + hd, yv[:, cs])

        def wide(off):
            return _dot(h, w_ref[:, off:off + D_MODEL])

        rs = slice(r0, r0 + tr)
        group(2)
        y = wide(OFF_XB)
        xb_ref[rs, :] = y.astype(BF16)
        n_x = xbt_ref.shape[0]
        if r0 + tr == tm:
            xbt_ref[...] = y[tr - n_x:, :]
        gy_ref[rs, :] = jax.nn.gelu(wide(OFF_YB)).astype(BF16)
        group(1)
        sga_ref[rs, :] = jax.nn.sigmoid(wide(OFF_GA)).astype(BF16)
        group(0)
        sgb_ref[rs, :] = jax.nn.sigmoid(wide(OFF_GB)).astype(BF16)

    for c in range(split):
        chunk(c)


def _in_proj_prompt(x, g_mix, w_in, g_q, g_k):
    bsz, seq, _ = x.shape
    tm = ROW_TILE
    n_tiles = seq // tm
    gw = GROUP_WIDTH

    def nat(width):
        return pl.BlockSpec((None, tm, width), lambda b, i: (b, i, 0))

    def sds(shape, dt):
        return jax.ShapeDtypeStruct(shape, dt)

    qkv_shapes, qkv_specs = [], []
    for dil in GROUP_DILATIONS:
        if dil == 1:
            qkv_shapes.append(sds((bsz, seq, gw), BF16))
            qkv_specs.append(nat(gw))
        else:
            qkv_shapes.append(sds((bsz, dil, seq // dil, gw), BF16))
            qkv_specs.append(pl.BlockSpec((None, dil, tm // dil, gw), lambda b, i: (b, 0, i, 0)))
    tail_rows = [min(w, tm) for w in GROUP_WINDOWS]
    tail_start = [n_tiles - max(w // tm, 1) for w in GROUP_WINDOWS]
    tail_specs = [pl.BlockSpec((None, r * KV_ROWS, HEAD_DIM), lambda b, i, s=s: (b, jnp.maximum(i - s, 0), 0))
                  for r, s in zip(tail_rows, tail_start)]
    out_shape = (qkv_shapes * 3 + [sds((bsz, seq, LRU_WIDTH), BF16)] * 2 + [sds((bsz, seq, D_MODEL), BF16)] * 2
                 + [sds((bsz, w * KV_ROWS, HEAD_DIM), F32) for w in GROUP_WINDOWS]
                 + [sds((bsz, SUBLANES, LRU_WIDTH), F32)])
    out_specs = (qkv_specs * 3 + [nat(LRU_WIDTH)] * 2 + [nat(D_MODEL)] * 2 + tail_specs
                 + [pl.BlockSpec((None, SUBLANES, LRU_WIDTH), lambda b, i: (b, 0, 0))])
    return pl.pallas_call(
        functools.partial(_in_proj_kernel, dils=GROUP_DILATIONS, split=IN_SPLIT),
        grid=(bsz, n_tiles),
        in_specs=[nat(D_MODEL), _resident(g_mix.shape), _resident(w_in.shape),
                  _resident(g_q.shape), _resident(g_k.shape)],
        out_specs=out_specs,
        out_shape=out_shape,
        scratch_shapes=[pltpu.VMEM(((N_GROUPS - 1) * 3 * HEADS_PER_GROUP, tm, HEAD_DIM), F32)],
        compiler_params=_cparams(2),
        name="in_proj_prompt",
    )(x, g_mix, w_in, g_q, g_k)


def _in_proj_sample(x, g_mix, w_in, g_q, g_k):
    n = x.shape[0]
    tm = SAMPLE_TILE
    gw = GROUP_WIDTH

    def rows(width):
        return pl.BlockSpec((tm, width), lambda i: (i, 0))

    def sds(width, dt):
        return jax.ShapeDtypeStruct((n, width), dt)

    out_shape = ([sds(gw, BF16)] * 9 + [sds(LRU_WIDTH, BF16)] * 2 + [sds(D_MODEL, BF16)] * 2
                 + [jax.ShapeDtypeStruct((n * KV_ROWS, HEAD_DIM), F32)] * 3 + [sds(LRU_WIDTH, F32)])
    out_specs = ([rows(gw)] * 9 + [rows(LRU_WIDTH)] * 2 + [rows(D_MODEL)] * 2
                 + [pl.BlockSpec((tm * KV_ROWS, HEAD_DIM), lambda i: (i, 0))] * 3 + [rows(LRU_WIDTH)])
    return pl.pallas_call(
        functools.partial(_in_proj_kernel, dils=(1, 1, 1), split=1),
        grid=(n // tm,),
        in_specs=[rows(D_MODEL), _resident(g_mix.shape), _resident(w_in.shape),
                  _resident(g_q.shape), _resident(g_k.shape)],
        out_specs=out_specs,
        out_shape=out_shape,
        scratch_shapes=[pltpu.VMEM(((N_GROUPS - 1) * 3 * HEADS_PER_GROUP, tm, HEAD_DIM), F32)],
        compiler_params=_cparams(1),
        name="in_proj_sample",
    )(x, g_mix, w_in, g_q, g_k)


def _swa_kernel(q_ref, k_ref, v_ref, kp_ref, vp_ref, bias_ref, o_ref, st_ref):
    first = pl.program_id(1) == 0
    n_seq = q_ref.shape[0]
    n_blk = q_ref.shape[1] // Q_BLOCK
    col = lax.broadcasted_iota(jnp.int32, (Q_BLOCK, 2 * Q_BLOCK), 1)
    lane = lax.broadcasted_iota(jnp.int32, (Q_BLOCK, HEAD_DIM), 1)
    ones = jnp.ones((2 * Q_BLOCK, HEAD_DIM), BF16)
    for sq in range(n_seq):
        for j in range(n_blk):
            rq = slice(j * Q_BLOCK, (j + 1) * Q_BLOCK)
            st_tile = jnp.zeros((Q_BLOCK, HEAD_DIM), F32)
            for hd in range(HEADS_PER_GROUP):
                cs = slice(hd * HEAD_DIM, (hd + 1) * HEAD_DIM)
                q = q_ref[sq, rq, cs]
                if j == 0:
                    kk = jnp.concatenate([kp_ref[sq, :, cs], k_ref[sq, rq, cs]], axis=0)
                    vv = jnp.concatenate([vp_ref[sq, :, cs], v_ref[sq, rq, cs]], axis=0)
                else:
                    rk = slice((j - 1) * Q_BLOCK, (j + 1) * Q_BLOCK)
                    kk = k_ref[sq, rk, cs]
                    vv = v_ref[sq, rk, cs]
                s = _dot_nt(q, kk) + bias_ref[hd]
                if j == 0:
                    s = jnp.where(col < jnp.where(first, Q_BLOCK, 0), NEG_INF, s)
                m = jnp.max(s, axis=-1, keepdims=True)
                p = jnp.exp(s - m).astype(BF16)
                oe = _dot(p, jnp.concatenate([vv, ones], axis=1))
                o_ref[sq, rq, cs] = oe[:, :HEAD_DIM].astype(BF16)
                st_tile = jnp.where(lane == hd, m, st_tile)
                st_tile = jnp.where(lane == HEADS_PER_GROUP + hd, oe[:, HEAD_DIM:], st_tile)
            st_ref[sq, rq, :] = st_tile


def _swa(q, k, v, band_bias):
    n_seq, length, gw = q.shape
    tq = min(length, SWA_ROWS)
    sb = SWA_ROWS // tq
    ratio = tq // Q_BLOCK
    cur = pl.BlockSpec((sb, tq, gw), lambda s, i: (s, i, 0))
    prev = pl.BlockSpec((sb, Q_BLOCK, gw), lambda s, i: (s, jnp.maximum(i * ratio - 1, 0), 0))
    return pl.pallas_call(
        _swa_kernel,
        grid=(n_seq // sb, length // tq),
        in_specs=[cur, cur, cur, prev, prev, _resident(band_bias.shape)],
        out_specs=[cur, pl.BlockSpec((sb, tq, HEAD_DIM), lambda s, i: (s, i, 0))],
        out_shape=[jax.ShapeDtypeStruct((n_seq, length, gw), BF16),
                   jax.ShapeDtypeStruct((n_seq, length, HEAD_DIM), F32)],
        compiler_params=_cparams(2),
        name="swa",
    )(q, k, v, k, v, band_bias)


DEC_BATCH_TILE = 4
T_PAIR = 2


def _dec_attn_kernel(q_ref, k1, v1, k2, v2, k3, v3, newp_ref, newd_ref, bc_ref, bn1_ref, bn_ref, att_ref):
    k_refs, v_refs = (k1, k2, k3), (v1, v2, v3)
    n_b, _, n_p = q_ref.shape[:3]
    for bi in range(n_b):
        near_k = jnp.concatenate([k1[bi]] * T_PAIR, axis=1)
        near_v = jnp.concatenate([v1[bi]] * T_PAIR, axis=1)
        for tp in range(n_p):
            accs, tops, dens = [], [], []
            for g in range(N_GROUPS):
                qv = q_ref[bi, g, tp]
                if GROUP_DILATIONS[g] == 1:
                    kt, vt = near_k, near_v
                    kn, vn = newd_ref[bi, 0], newd_ref[bi, 1]
                    bn = bn1_ref[tp]
                else:
                    steps = slice(tp * T_PAIR, (tp + 1) * T_PAIR)
                    kt = k_refs[g][bi, :, steps, :, :].reshape(WINDOW_SLOTS, KV_ROWS, HEAD_DIM)
                    vt = v_refs[g][bi, :, steps, :, :].reshape(WINDOW_SLOTS, KV_ROWS, HEAD_DIM)
                    kn, vn = newp_ref[bi, 0, g, tp][None], newp_ref[bi, 1, g, tp][None]
                    bn = bn_ref[g][None]
                lc = jnp.sum(kt * qv[None], axis=-1, keepdims=True) + bc_ref[g, tp]
                ln = jnp.sum(kn * qv[None], axis=-1, keepdims=True) + bn
                m = jnp.maximum(jnp.max(lc, axis=0), jnp.max(ln, axis=0))
                pc = jnp.exp(lc - m[None])
                pn = jnp.exp(ln - m[None])
                dens.append(jnp.sum(pc, axis=0) + jnp.sum(pn, axis=0))
                accs.append(jnp.sum(pc * vt, axis=0) + jnp.sum(pn * vn, axis=0))
                tops.append(m)
            top = jnp.maximum(jnp.maximum(tops[0], tops[1]), tops[2])
            ws = [jnp.exp(m - top) for m in tops]
            tot = ws[0] * dens[0] + ws[1] * dens[1] + ws[2] * dens[2]
            out = (ws[0] / tot) * accs[0]
            for g in (1, 2):
                out = out + (ws[g] / tot) * accs[g]
            att_ref[bi, tp] = out


def _dec_attn(qv, caches, newp, newd, bias_cache, bias_new1, bias_new):
    bd, _, n_p = qv.shape[:3]
    bb = DEC_BATCH_TILE
    hd = (HEADS_PER_GROUP, HEAD_DIM)
    cspecs, cargs = [], []
    for g, dil in enumerate(GROUP_DILATIONS):
        for kv in range(2):
            if dil == 1:
                cspecs.append(pl.BlockSpec((bb, WINDOW_SLOTS, None) + hd, lambda i, kv=kv: (i, 0, kv, 0, 0)))
            else:
                cspecs.append(pl.BlockSpec((bb, WINDOW_SLOTS, n_p * T_PAIR, None) + hd,
                                           lambda i, kv=kv: (i, 0, 0, kv, 0, 0)))
            cargs.append(caches[g])

    def whole(a):
        return pl.BlockSpec((bb,) + a.shape[1:], lambda i: (i,) + (0,) * (a.ndim - 1))

    return pl.pallas_call(
        _dec_attn_kernel,
        grid=(bd // bb,),
        in_specs=[whole(qv)] + cspecs + [whole(newp), whole(newd), _resident(bias_cache.shape),
                                         _resident(bias_new1.shape), _resident(bias_new.shape)],
        out_specs=pl.BlockSpec((bb, n_p, KV_ROWS, HEAD_DIM), lambda i: (i, 0, 0, 0)),
        out_shape=jax.ShapeDtypeStruct((bd, n_p, KV_ROWS, HEAD_DIM), F32),
        compiler_params=_cparams(1),
        name="dec_attn",
    )(qv, *cargs, newp, newd, bias_cache, bias_new1, bias_new)


def _lru_gates(xc, cs, blk, wa_ref, ba_ref, wi_ref, bi_ref, lam_ref):
    xcb = xc.astype(BF16)
    tr = jnp.tanh(_dot(xcb, wa_ref[blk]) + 0.5 * ba_ref[:, cs])
    ti = jnp.tanh(_dot(xcb, wi_ref[blk]) + 0.5 * bi_ref[:, cs])
    lam = lam_ref[:, cs]
    log_sig = jnp.minimum(lam, 0.0) - jnp.log1p(jnp.exp(-jnp.abs(lam)))
    half_c = (0.5 * RG_C) * log_sig
    log_a = half_c * tr + half_c
    a = jnp.exp(log_a)
    hx = 0.5 * xc
    y = jnp.tanh(log_a) * (-1.0 - a * a)
    root = jnp.where(y > 0.0, y * lax.rsqrt(y), 0.0)
    b = root * (hx * ti + hx)
    return a, b


def _lru_seq_kernel(xb_ref, gy_ref, wconv_ref, bconv_ref, wa_ref, ba_ref, wi_ref, bi_ref, lam_ref, wob_ref,
                    out_ref, hlast_ref, xcat, a_s, b_s, h_sl, hg, hc):
    nb, tl, c = xb_ref.shape
    rows = nb * tl
    halo = (CONV_WIDTH - 1) * nb
    n_sl = c // LANES
    per = GATE_TILE // LANES

    @pl.when(pl.program_id(0) == 0)
    def _():
        xcat[:, 0:halo, :] = jnp.zeros((n_sl, halo, LANES), F32)
        hc[...] = jnp.zeros((nb, c), F32)

    for b in range(nb):
        xf = xb_ref[b].astype(F32)
        for s in range(n_sl):
            xcat[s, pl.ds(halo + b, tl, stride=nb), :] = xf[:, s * LANES:(s + 1) * LANES]
    for blk in range(c // GATE_TILE):
        cs = slice(blk * GATE_TILE, (blk + 1) * GATE_TILE)
        parts = []
        for s in range(blk * per, (blk + 1) * per):
            ls = slice(s * LANES, (s + 1) * LANES)
            xc = bconv_ref[:, ls] + wconv_ref[0:1, ls] * xcat[s, 0:rows, :]
            for j in range(1, CONV_WIDTH):
                xc = xc + wconv_ref[j:j + 1, ls] * xcat[s, j * nb:j * nb + rows, :]
            parts.append(xc)
        a, b = _lru_gates(jnp.concatenate(parts, axis=1), cs, blk, wa_ref, ba_ref, wi_ref, bi_ref, lam_ref)
        a_s[:, cs] = a
        b_s[:, cs] = b
    for s in range(n_sl):
        xcat[s, 0:halo, :] = xcat[s, rows:rows + halo, :]

    def step(t, h):
        rs = pl.ds(pl.multiple_of(t * nb, nb), nb)
        h = a_s[rs, :] * h + b_s[rs, :]
        for s in range(n_sl):
            h_sl[s, rs, :] = h[:, s * LANES:(s + 1) * LANES]
        return h

    h = lax.fori_loop(0, tl, step, hc[...], unroll=8)
    hc[...] = h
    hlast_ref[...] = h
    for b in range(nb):
        for s in range(n_sl):
            ls = slice(s * LANES, (s + 1) * LANES)
            hg[b * tl:(b + 1) * tl, ls] = h_sl[s, pl.ds(b, tl, stride=nb), :].astype(BF16) * gy_ref[b, :, ls]
    out_ref[...] = _dot(hg[...], wob_ref[...]).astype(BF16).reshape(nb, tl, out_ref.shape[2])


def _lru_seq(xb, gy, w_conv, b_conv, wa, ba, wi, bi, lam, w_ob):
    bsz, seq, c = xb.shape
    tl = LRU_STEPS
    rows = bsz * tl
    tile = pl.BlockSpec((bsz, tl, c), lambda i: (0, i, 0))
    consts = [w_conv, b_conv, wa, ba, wi, bi, lam, w_ob]
    return pl.pallas_call(
        _lru_seq_kernel,
        grid=(seq // tl,),
        in_specs=[tile, tile] + [_resident(a.shape) for a in consts],
        out_specs=[pl.BlockSpec((bsz, tl, D_MODEL), lambda i: (0, i, 0)), pl.BlockSpec((bsz, c), lambda i: (0, 0))],
        out_shape=[jax.ShapeDtypeStruct((bsz, seq, D_MODEL), BF16), jax.ShapeDtypeStruct((bsz, c), F32)],
        scratch_shapes=[pltpu.VMEM((c // LANES, rows + (CONV_WIDTH - 1) * bsz, LANES), F32),
                        pltpu.VMEM((rows, c), F32), pltpu.VMEM((rows, c), F32),
                        pltpu.VMEM((c // LANES, rows, LANES), F32), pltpu.VMEM((rows, c), BF16),
                        pltpu.VMEM((bsz, c), F32)],
        compiler_params=_cparams(1),
        name="lru_seq",
    )(xb, gy, *consts)


def _lru_step_kernel(xb_ref, gy_ref, conv0_ref, h0_ref, wconv_ref, bconv_ref, wa_ref, ba_ref, wi_ref, bi_ref,
                     lam_ref, wob_ref, out_ref, hlast_ref, xcat, b_s, *, nb, tl):
    rows = nb * tl
    halo = (CONV_WIDTH - 1) * nb
    xcat[0:halo, :] = conv0_ref[...]
    xcat[halo:halo + rows, :] = xb_ref[...].astype(F32)
    xc = bconv_ref[...] + wconv_ref[0:1, :] * xcat[0:rows, :]
    for j in range(1, CONV_WIDTH):
        xc = xc + wconv_ref[j:j + 1, :] * xcat[j * nb:j * nb + rows, :]
    for blk in range(xc.shape[1] // GATE_TILE):
        cs = slice(blk * GATE_TILE, (blk + 1) * GATE_TILE)
        a, b = _lru_gates(xc[:, cs], cs, blk, wa_ref, ba_ref, wi_ref, bi_ref, lam_ref)
        h = h0_ref[:, cs]
        for t in range(tl):
            rs = slice(t * nb, (t + 1) * nb)
            h = a[rs, :] * h + b[rs, :]
            b_s[rs, cs] = h
        hlast_ref[:, cs] = h
    out_ref[...] = _dot((b_s[...] * gy_ref[...].astype(F32)).astype(BF16), wob_ref[...]).astype(BF16)


def _lru_step(xb, gy, conv0, h0, w_conv, b_conv, wa, ba, wi, bi, lam, w_ob, *, nb, tl):
    n_rows, c = xb.shape
    consts = [conv0, h0, w_conv, b_conv, wa, ba, wi, bi, lam, w_ob]
    full = lambda shape: pl.BlockSpec(shape, lambda i: (0,) * len(shape))
    return pl.pallas_call(
        functools.partial(_lru_step_kernel, nb=nb, tl=tl),
        grid=(1,),
        in_specs=[full(xb.shape), full(gy.shape)] + [full(a.shape) for a in consts],
        out_specs=[full((n_rows, D_MODEL)), full((nb, c))],
        out_shape=[jax.ShapeDtypeStruct((n_rows, D_MODEL), BF16), jax.ShapeDtypeStruct((nb, c), F32)],
        scratch_shapes=[pltpu.VMEM((n_rows + (CONV_WIDTH - 1) * nb, c), F32), pltpu.VMEM((n_rows, c), F32)],
        compiler_params=_cparams(1),
        name="lru_step",
    )(xb, gy, *consts)


FF_CHUNK = 1024
IN_SPLIT = 2
MIX_SPLIT = 2
N_O_SLABS = GROUP_WIDTH // LANES


def _mix_mlp_kernel(x_ref, o1, o2, o3, l1, l2, l3, bout_ref, sga_ref, sgb_ref, pe_ref,
                    woa_ref, wo_ref, gmlp_ref, wup_ref, wdown_ref, gple_ref, wpg_ref, wpe_ref, y_ref,
                    slab_ref, *, dils, split):
    tm = x_ref.shape[0]

    def natural(ref, n_slabs, dil, base):
        if dil == 1:
            return lambda rs: [ref[rs, c * LANES:(c + 1) * LANES].astype(F32) for c in range(n_slabs)]
        n = tm // dil
        for c in range(n_slabs):
            for r in range(dil):
                slab_ref[base + c, pl.ds(r, n, stride=dil), :] = ref[r, :, c * LANES:(c + 1) * LANES].astype(F32)
        return lambda rs: [slab_ref[base + c, rs, :] for c in range(n_slabs)]

    o_get, l_get = [], []
    for g, (o_ref, l_ref) in enumerate(((o1, l1), (o2, l2), (o3, l3))):
        base = g * (N_O_SLABS + 1)
        o_get.append(natural(o_ref, N_O_SLABS, dils[g], base))
        l_get.append(natural(l_ref, 1, dils[g], base + N_O_SLABS))

    def rows_body(rs):
        os_ = [get(rs) for get in o_get]
        sts = [get(rs)[0] for get in l_get]
        dens = [pltpu.roll(st, LANES - HEADS_PER_GROUP, axis=1) for st in sts]
        top = jnp.maximum(jnp.maximum(sts[0], sts[1]), sts[2])
        ws = [jnp.exp(st - top) for st in sts]
        tot = ws[0] * dens[0] + ws[1] * dens[1] + ws[2] * dens[2]
        ws = [w / tot for w in ws]
        heads = []
        for hd in range(HEADS_PER_GROUP):
            acc = ws[0][:, hd:hd + 1] * os_[0][hd]
            for g in (1, 2):
                acc = acc + ws[g][:, hd:hd + 1] * os_[g][hd]
            heads.append(acc.astype(BF16))
        att = jnp.concatenate(heads, axis=1)
        yield
        a_out = _dot(att, woa_ref[...])
        mix = (sga_ref[rs, :].astype(F32) * a_out
               + sgb_ref[rs, :].astype(F32) * bout_ref[rs, :].astype(F32))
        yield
        x = x_ref[rs, :] + _dot(mix.astype(BF16), wo_ref[...])
        h = _rms(x, gmlp_ref[...]).astype(BF16)
        yield
        acc = jnp.zeros(x.shape, F32)
        for c in range(D_FF // FF_CHUNK):
            cs = slice(c * FF_CHUNK, (c + 1) * FF_CHUNK)
            u = jnp.square(jnp.maximum(_dot(h, wup_ref[:, cs]), 0.0))
            acc = acc + _dot(u.astype(BF16), wdown_ref[cs, :])
            yield
        x = x + acc
        h = _rms(x, gple_ref[...]).astype(BF16)
        yield
        gate = jax.nn.sigmoid(_dot(h, wpg_ref[...]))
        y_ref[rs, :] = x + gate * _dot(pe_ref[rs, :].astype(BF16), wpe_ref[...])

    n_rows = tm // split
    chunks = [rows_body(slice(part * n_rows, (part + 1) * n_rows)) for part in range(split)]
    while chunks:
        chunks = [c for c in chunks if next(c, True) is None]


def _mix_mlp_prompt(x, os_, ls, bout, sga, sgb, pe, weights):
    bsz, seq, _ = x.shape
    tm = ROW_TILE

    def nat(width):
        return pl.BlockSpec((None, tm, width), lambda b, i: (b, i, 0))

    def grouped(width):
        return [nat(width) if dil == 1 else
                pl.BlockSpec((None, dil, tm // dil, width), lambda b, i: (b, 0, i, 0))
                for dil in GROUP_DILATIONS]

    ws = list(weights)
    specs = ([nat(D_MODEL)] + grouped(GROUP_WIDTH) + grouped(HEAD_DIM) + [nat(D_MODEL)] * 3 + [nat(PLE_DIM)]
             + [_resident(w.shape) for w in ws])
    return pl.pallas_call(
        functools.partial(_mix_mlp_kernel, dils=GROUP_DILATIONS, split=MIX_SPLIT),
        grid=(bsz, seq // tm),
        in_specs=specs,
        out_specs=nat(D_MODEL),
        out_shape=jax.ShapeDtypeStruct((bsz, seq, D_MODEL), F32),
        scratch_shapes=[pltpu.VMEM((N_GROUPS * (N_O_SLABS + 1), tm, LANES), F32)],
        compiler_params=_cparams(2),
        name="mix_mlp_prompt",
    )(x, *os_, *ls, bout, sga, sgb, pe, *ws)


def _mix_mlp_sample(x, os_, ls, bout, sga, sgb, pe, weights):
    n = x.shape[0]
    tm = SAMPLE_TILE

    def rows(width):
        return pl.BlockSpec((tm, width), lambda i: (i, 0))

    ws = list(weights)
    specs = ([rows(D_MODEL)] + [rows(GROUP_WIDTH)] * 3 + [rows(HEAD_DIM)] * 3 + [rows(D_MODEL)] * 3
             + [rows(PLE_DIM)] + [_resident(w.shape) for w in ws])
    return pl.pallas_call(
        functools.partial(_mix_mlp_kernel, dils=(1, 1, 1), split=1),
        grid=(n // tm,),
        in_specs=specs,
        out_specs=rows(D_MODEL),
        out_shape=jax.ShapeDtypeStruct((n, D_MODEL), F32),
        scratch_shapes=[pltpu.VMEM((N_GROUPS * (N_O_SLABS + 1), tm, LANES), F32)],
        compiler_params=_cparams(1),
        name="mix_mlp_sample",
    )(x, *os_, *ls, bout, sga, sgb, pe, *ws)


def _t5_bucket(dist):
    max_exact = REL_BUCKETS // 2
    d = jnp.maximum(dist, 1).astype(F32)
    large = max_exact + (jnp.log(d / max_exact) / math.log(REL_MAX_DIST / max_exact)
                         * (REL_BUCKETS - max_exact)).astype(jnp.int32)
    large = jnp.minimum(large, REL_BUCKETS - 1)
    return jnp.where(dist < max_exact, dist, large)


def _slot_bias(rel_bias, g):
    dil = GROUP_DILATIONS[g]
    dist = dil * jnp.arange(WINDOW_SLOTS + 1, dtype=jnp.int32)
    hs = slice(g * HEADS_PER_GROUP, (g + 1) * HEADS_PER_GROUP)
    return rel_bias[_t5_bucket(dist)][:, hs].astype(F32)


def _band_bias(slot_bias):
    n_h = slot_bias.shape[1]
    pad = jnp.full((n_h, Q_BLOCK - 1), NEG_INF, F32)
    ext = jnp.concatenate([pad, slot_bias[::-1].T, pad, jnp.full((n_h, 1), NEG_INF, F32)], axis=1)
    width = ext.shape[1]
    skew = jnp.broadcast_to(ext[:, None, :], (n_h, Q_BLOCK, width)).reshape(n_h, Q_BLOCK * width)
    skew = skew[:, :Q_BLOCK * (width - 1)].reshape(n_h, Q_BLOCK, width - 1)
    return skew[:, :, Q_BLOCK - 1:3 * Q_BLOCK - 1]


def _pair_rows_table(per_step):
    pairs = [jnp.concatenate(per_step[p * T_PAIR:(p + 1) * T_PAIR], axis=-1) for p in range(len(per_step) // T_PAIR)]
    tbl = jnp.stack(pairs)
    return jnp.broadcast_to(tbl[..., None], tbl.shape + (HEAD_DIM,))


def _dec_bias(slot_biases, n_t):
    n_h = HEADS_PER_GROUP
    neg = lambda n: jnp.full((n, n_h), NEG_INF, F32)
    bc = []
    for g in range(N_GROUPS):
        near_first = slot_biases[g][::-1]
        if GROUP_DILATIONS[g] == 1:
            per_step = [jnp.concatenate([neg(t), near_first[:WINDOW_SLOTS - t]], axis=0) for t in range(n_t)]
        else:
            per_step = [near_first[:WINDOW_SLOTS]] * n_t
        bc.append(_pair_rows_table(per_step))
    sb = slot_biases[0]
    bn1 = _pair_rows_table([jnp.concatenate([sb[:t + 1][::-1], neg(n_t - 1 - t)], axis=0) for t in range(n_t)])
    own = jnp.stack([jnp.concatenate([s[:1]] * T_PAIR, axis=-1)[0] for s in slot_biases])
    bn = jnp.broadcast_to(own[..., None], own.shape + (HEAD_DIM,))
    return jnp.stack(bc), bn1, bn


def _gate_tiles(w):
    per = GATE_TILE // LRU_BLOCK
    w = w.reshape(LRU_WIDTH // GATE_TILE, per, LRU_BLOCK, LRU_BLOCK)
    eye = jnp.eye(per, dtype=w.dtype)
    return jnp.einsum('npij,pq->npiqj', w, eye).reshape(-1, GATE_TILE, GATE_TILE).astype(BF16)


def kernel(x_prompt, x_sample, p_prompt, p_sample, cache_kv1, cache_kv2, cache_kv3, state_conv, state_lru,
           rel_bias, g_mix, w_in, g_q, g_k, w_oa, w_conv, b_conv, w_rg_a, b_rg_a, w_rg_i, b_rg_i, lam,
           w_ob, w_o, g_mlp, w_up, w_down, g_ple, w_ple_gate, w_ple_in):
    depth = w_in.shape[0]
    assert depth == 1
    bsz, seq, _ = x_prompt.shape
    bd, n_t, _ = x_sample.shape
    gw = GROUP_WIDTH
    row = lambda a: a.reshape(1, -1)

    slot_biases = [_slot_bias(rel_bias, g) for g in range(N_GROUPS)]
    band = [_band_bias(sb) for sb in slot_biases]
    bias_cache, bias_new1, bias_new = _dec_bias(slot_biases, n_t)

    i = 0
    w_in_b = w_in[i].astype(BF16)
    gm, gq, gk = row(g_mix[i]), row(g_q[i]), row(g_k[i])
    lru_w = (w_conv[i], row(b_conv[i]), _gate_tiles(0.5 * w_rg_a[i]), row(b_rg_a[i]), _gate_tiles(0.5 * w_rg_i[i]),
             row(b_rg_i[i]), row(lam[i]), w_ob[i].astype(BF16))
    mlp_w = (w_oa[i].astype(BF16), w_o[i].astype(BF16), row(g_mlp[i]), w_up[i].astype(BF16),
             w_down[i].astype(BF16), row(g_ple[i]), w_ple_gate[i].astype(BF16), w_ple_in[i].astype(BF16))

    (q1, q2, q3, k1, k2, k3, v1, v2, v3, xb, gy, sga, sgb, kvt1, kvt2, kvt3, xbt) = _in_proj_prompt(
        x_prompt, gm, w_in_b, gq, gk)
    os_, ls = [], []
    for g, (qq, kk, vv) in enumerate(((q1, k1, v1), (q2, k2, v2), (q3, k3, v3))):
        dil = GROUP_DILATIONS[g]
        ns, ln = bsz * dil, seq // dil
        o, l = _swa(qq.reshape(ns, ln, gw), kk.reshape(ns, ln, gw), vv.reshape(ns, ln, gw), band[g])
        os_.append(o if dil == 1 else o.reshape(bsz, dil, ln, gw))
        ls.append(l if dil == 1 else l.reshape(bsz, dil, ln, HEAD_DIM))
    bout, lru_p = _lru_seq(xb, gy, *lru_w)
    y_prompt = _mix_mlp_prompt(x_prompt, os_, ls, bout, sga, sgb, p_prompt[i], mlp_w)
    kv_p = [t.reshape(bsz, GROUP_WINDOWS[g], 2, HEADS_PER_GROUP, HEAD_DIM)[None]
            for g, t in enumerate((kvt1, kvt2, kvt3))]
    conv_p = xbt[:, SUBLANES - (CONV_WIDTH - 1):][None]
    lru_p = lru_p[None]

    n_s = bd * n_t
    (sq1, sq2, sq3, _, _, _, _, _, _, sxb, sgy, ssga, ssgb, skv1, skv2, skv3, sxbt) = _in_proj_sample(
        x_sample.reshape(n_s, D_MODEL), gm, w_in_b, gq, gk)
    n_p = n_t // T_PAIR
    qv = jnp.stack([q.astype(F32).reshape(bd, n_p, KV_ROWS, HEAD_DIM) for q in (sq1, sq2, sq3)], axis=1)
    kv_new = jnp.stack([t.reshape(bd, n_t, 2, HEADS_PER_GROUP, HEAD_DIM) for t in (skv1, skv2, skv3)], axis=1)
    kv_new = jnp.moveaxis(kv_new, 3, 1)
    newp = kv_new.reshape(bd, 2, N_GROUPS, n_p, KV_ROWS, HEAD_DIM)
    newd = jnp.concatenate([kv_new[:, :, 0]] * T_PAIR, axis=-2)
    caches = [c.reshape((bd, WINDOW_SLOTS) + ((dil,) if dil > 1 else ()) + (2, HEADS_PER_GROUP, HEAD_DIM))
              for c, dil in zip((cache_kv1, cache_kv2, cache_kv3), GROUP_DILATIONS)]
    att_s = _dec_attn(qv, caches, newp, newd, bias_cache, bias_new1, bias_new).reshape(n_s, gw)

    def to_tb(a):
        return a.reshape(bd, n_t, -1).transpose(1, 0, 2).reshape(n_t * bd, -1)

    def from_tb(a):
        return a.reshape(n_t, bd, -1).transpose(1, 0, 2).reshape(bd * n_t, -1)

    conv0_s = state_conv[i].transpose(1, 0, 2).reshape((CONV_WIDTH - 1) * bd, LRU_WIDTH)
    bout_s, lru_s = _lru_step(to_tb(sxb), to_tb(sgy), conv0_s, state_lru[i].astype(F32), *lru_w, nb=bd, tl=n_t)
    zero_o = jnp.zeros((n_s, gw), BF16)
    lane = jnp.arange(HEAD_DIM)[None, :]
    den_one = jnp.where(lane < HEADS_PER_GROUP, 0.0, 1.0)
    st_on = jnp.broadcast_to(den_one, (n_s, HEAD_DIM)).astype(F32)
    st_off = jnp.broadcast_to(jnp.where(lane < HEADS_PER_GROUP, NEG_INF, den_one), (n_s, HEAD_DIM)).astype(F32)
    y_sample = _mix_mlp_sample(x_sample.reshape(n_s, D_MODEL), [att_s.astype(BF16), zero_o, zero_o],
                               [st_on, st_off, st_off], from_tb(bout_s), ssga, ssgb,
                               p_sample[i].reshape(n_s, PLE_DIM), mlp_w).reshape(bd, n_t, D_MODEL)
    kv_s = [t.reshape(bd, n_t, 2, HEADS_PER_GROUP, HEAD_DIM)[None] for t in (skv1, skv2, skv3)]
    xcat_s = jnp.concatenate([state_conv[i], sxbt.reshape(bd, n_t, LRU_WIDTH)], axis=1)
    conv_s = xcat_s[:, n_t:][None]

    return (y_prompt, y_sample, kv_p[0], kv_p[1], kv_p[2], conv_p, lru_p,
            kv_s[0], kv_s[1], kv_s[2], conv_s, lru_s[None])
```

```python
import functools
import math

import jax
import jax.numpy as jnp
from jax import lax
from jax.experimental import pallas as pl
from jax.experimental.pallas import tpu as pltpu

F32 = jnp.float32
BF16 = jnp.bfloat16

D_MODEL = 1024
HEAD_DIM = 128
HEADS_PER_GROUP = 4
GROUP_WINDOWS = (128, 512, 2048)
GROUP_DILATIONS = (1, 4, 16)
N_GROUPS = 3
GROUP_WIDTH = HEADS_PER_GROUP * HEAD_DIM
ATT_WIDTH = N_GROUPS * GROUP_WIDTH
ATT_SCALE = HEAD_DIM ** -0.5
LRU_WIDTH = D_MODEL
LRU_BLOCKS = 16
LRU_BLOCK = LRU_WIDTH // LRU_BLOCKS
CONV_WIDTH = 4
RG_C = 8.0
D_FF = 4 * D_MODEL
PLE_DIM = 256
REL_BUCKETS = 32
REL_MAX_DIST = 2048
NORM_EPS = 1e-6
NEG_INF = -1e30
WINDOW_SLOTS = 128
KV_ROWS = 2 * HEADS_PER_GROUP

OFF_Q, OFF_K, OFF_V = 0, ATT_WIDTH, 2 * ATT_WIDTH
OFF_XB = 3 * ATT_WIDTH
OFF_YB = OFF_XB + LRU_WIDTH
OFF_GA = OFF_YB + LRU_WIDTH
OFF_GB = OFF_GA + D_MODEL

SUBLANES = 8
LANES = 128
ROW_TILE = 512
SAMPLE_TILE = 128
Q_BLOCK = 128
SWA_ROWS = 1024
LRU_STEPS = 128
GATE_TILE = 256
VMEM_LIMIT = 58 * 1024 * 1024


def _cparams(n_axes):
    return pltpu.CompilerParams(dimension_semantics=("arbitrary",) * n_axes,
                                vmem_limit_bytes=VMEM_LIMIT)


def _resident(shape):
    nd = len(shape)
    return pl.BlockSpec(shape, lambda *_: (0,) * nd, pipeline_mode=pl.Buffered(1))


def _rms(x, gain):
    return x * lax.rsqrt(jnp.mean(x * x, axis=-1, keepdims=True) + NORM_EPS) * gain


def _dot(a, b):
    return jnp.dot(a, b, preferred_element_type=F32)


def _dot_nt(a, b):
    return lax.dot_general(a, b, (((1,), (1,)), ((), ())), preferred_element_type=F32)


def _in_proj_kernel(x_ref, gmix_ref, w_ref, gq_ref, gk_ref,
                    q1, q2, q3, k1, k2, k3, v1, v2, v3, xb_ref, gy_ref, sga_ref, sgb_ref,
                    kv1, kv2, kv3, xbt_ref, slab_ref, *, dils, split):
    q_refs, k_refs, v_refs, kv_refs = (q1, q2, q3), (k1, k2, k3), (v1, v2, v3), (kv1, kv2, kv3)
    tm = x_ref.shape[0]
    tr = tm // split
    gq = gq_ref[...] * ATT_SCALE
    gk = gk_ref[...]

    def chunk(c):
        r0 = c * tr
        h = _rms(x_ref[r0:r0 + tr, :], gmix_ref[...]).astype(BF16)

        def put(ref, hd, val, dil, slab):
            cs = slice(hd * HEAD_DIM, (hd + 1) * HEAD_DIM)
            if dil == 1:
                ref[r0:r0 + tr, cs] = val.astype(BF16)
                return
            slab_ref[slab, r0:r0 + tr, :] = val
            n = tr // dil
            if dil == 16:
                spare = slab - 3 * HEADS_PER_GROUP
                q = tr // 4
                for r4 in range(4):
                    slab_ref[spare, r0 + r4 * q:r0 + (r4 + 1) * q, :] = slab_ref[slab, pl.ds(r0 + r4, q, stride=4), :]
                for r in range(dil):
                    src = r0 + (r % 4) * q + r // 4
                    ref[r, c * n:(c + 1) * n, cs] = slab_ref[spare, pl.ds(src, n, stride=4), :].astype(BF16)
                return
            for r in range(dil):
                ref[r, c * n:(c + 1) * n, cs] = slab_ref[slab, pl.ds(r0 + r, n, stride=dil), :].astype(BF16)

        def put_tail(ref, row, val):
            n_tail = ref.shape[0] // KV_ROWS
            lo = max(r0, tm - n_tail)
            cnt = r0 + tr - lo
            if cnt > 0:
                ref[pl.ds((lo - (tm - n_tail)) * KV_ROWS + row, cnt, stride=KV_ROWS), :] = val[lo - r0:, :]

        def group(g):
            c0 = g * GROUP_WIDTH
            s0 = max(g - 1, 0) * 3 * HEADS_PER_GROUP
            yq = _dot(h, w_ref[:, OFF_Q + c0:OFF_Q + c0 + GROUP_WIDTH])
            yk = _dot(h, w_ref[:, OFF_K + c0:OFF_K + c0 + GROUP_WIDTH])
            yv = _dot(h, w_ref[:, OFF_V + c0:OFF_V + c0 + GROUP_WIDTH])
            for hd in range(HEADS_PER_GROUP):
                cs = slice(hd * HEAD_DIM, (hd + 1) * HEAD_DIM)
                put(q_refs[g], hd, _rms(yq[:, cs], gq), dils[g], s0 + hd)
                kn = _rms(yk[:, cs], gk)
                put(k_refs[g], hd, kn, dils[g], s0 + HEADS_PER_GROUP + hd)
                put(v_refs[g], hd, yv[:, cs], dils[g], s0 + 2 * HEADS_PER_GROUP + hd)
                put_tail(kv_refs[g], hd, kn)
                put_tail(kv_refs[g], HEADS_PER_GROUP + hd, yv[:, cs])

        def wide(off):
            return _dot(h, w_ref[:, off:off + D_MODEL])

        rs = slice(r0, r0 + tr)
        group(2)
        y = wide(OFF_XB)
        xb_ref[rs, :] = y.astype(BF16)
        n_x = xbt_ref.shape[0]
        if r0 + tr == tm:
            xbt_ref[...] = y[tr - n_x:, :]
        gy_ref[rs, :] = jax.nn.gelu(wide(OFF_YB)).astype(BF16)
        group(1)
        sga_ref[rs, :] = jax.nn.sigmoid(wide(OFF_GA)).astype(BF16)
        group(0)
        sgb_ref[rs, :] = jax.nn.sigmoid(wide(OFF_GB)).astype(BF16)

    for c in range(split):
        chunk(c)


def _in_proj_prompt(x, g_mix, w_in, g_q, g_k):
    bsz, seq, _ = x.shape
    tm = ROW_TILE
    n_tiles = seq // tm
    gw = GROUP_WIDTH

    def nat(width):
        return pl.BlockSpec((None, tm, width), lambda b, i: (b, i, 0))

    def sds(shape, dt):
        return jax.ShapeDtypeStruct(shape, dt)

    qkv_shapes, qkv_specs = [], []
    for dil in GROUP_DILATIONS:
        if dil == 1:
            qkv_shapes.append(sds((bsz, seq, gw), BF16))
            qkv_specs.append(nat(gw))
        else:
            qkv_shapes.append(sds((bsz, dil, seq // dil, gw), BF16))
            qkv_specs.append(pl.BlockSpec((None, dil, tm // dil, gw), lambda b, i: (b, 0, i, 0)))
    tail_rows = [min(w, tm) for w in GROUP_WINDOWS]
    tail_start = [n_tiles - max(w // tm, 1) for w in GROUP_WINDOWS]
    tail_specs = [pl.BlockSpec((None, r * KV_ROWS, HEAD_DIM), lambda b, i, s=s: (b, jnp.maximum(i - s, 0), 0))
                  for r, s in zip(tail_rows, tail_start)]
    out_shape = (qkv_shapes * 3 + [sds((bsz, seq, LRU_WIDTH), BF16)] * 2 + [sds((bsz, seq, D_MODEL), BF16)] * 2
                 + [sds((bsz, w * KV_ROWS, HEAD_DIM), F32) for w in GROUP_WINDOWS]
                 + [sds((bsz, SUBLANES, LRU_WIDTH), F32)])
    out_specs = (qkv_specs * 3 + [nat(LRU_WIDTH)] * 2 + [nat(D_MODEL)] * 2 + tail_specs
                 + [pl.BlockSpec((None, SUBLANES, LRU_WIDTH), lambda b, i: (b, 0, 0))])
    return pl.pallas_call(
        functools.partial(_in_proj_kernel, dils=GROUP_DILATIONS, split=IN_SPLIT),
        grid=(bsz, n_tiles),
        in_specs=[nat(D_MODEL), _resident(g_mix.shape), _resident(w_in.shape),
                  _resident(g_q.shape), _resident(g_k.shape)],
        out_specs=out_specs,
        out_shape=out_shape,
        scratch_shapes=[pltpu.VMEM(((N_GROUPS - 1) * 3 * HEADS_PER_GROUP, tm, HEAD_DIM), F32)],
        compiler_params=_cparams(2),
        name="in_proj_prompt",
    )(x, g_mix, w_in, g_q, g_k)


def _in_proj_sample(x, g_mix, w_in, g_q, g_k):
    n = x.shape[0]
    tm = SAMPLE_TILE
    gw = GROUP_WIDTH

    def rows(width):
        return pl.BlockSpec((tm, width), lambda i: (i, 0))

    def sds(width, dt):
        return jax.ShapeDtypeStruct((n, width), dt)

    out_shape = ([sds(gw, BF16)] * 9 + [sds(LRU_WIDTH, BF16)] * 2 + [sds(D_MODEL, BF16)] * 2
                 + [jax.ShapeDtypeStruct((n * KV_ROWS, HEAD_DIM), F32)] * 3 + [sds(LRU_WIDTH, F32)])
    out_specs = ([rows(gw)] * 9 + [rows(LRU_WIDTH)] * 2 + [rows(D_MODEL)] * 2
                 + [pl.BlockSpec((tm * KV_ROWS, HEAD_DIM), lambda i: (i, 0))] * 3 + [rows(LRU_WIDTH)])
    return pl.pallas_call(
        functools.partial(_in_proj_kernel, dils=(1, 1, 1), split=1),
        grid=(n // tm,),
        in_specs=[rows(D_MODEL), _resident(g_mix.shape), _resident(w_in.shape),
                  _resident(g_q.shape), _resident(g_k.shape)],
        out_specs=out_specs,
        out_shape=out_shape,
        scratch_shapes=[pltpu.VMEM(((N_GROUPS - 1) * 3 * HEADS_PER_GROUP, tm, HEAD_DIM), F32)],
        compiler_params=_cparams(1),
        name="in_proj_sample",
    )(x, g_mix, w_in, g_q, g_k)


def _swa_kernel(q_ref, k_ref, v_ref, kp_ref, vp_ref, bias_ref, o_ref, st_ref):
    first = pl.program_id(1) == 0
    n_seq = q_ref.shape[0]
    n_blk = q_ref.shape[1] // Q_BLOCK
    col = lax.broadcasted_iota(jnp.int32, (Q_BLOCK, 2 * Q_BLOCK), 1)
    lane = lax.broadcasted_iota(jnp.int32, (Q_BLOCK, HEAD_DIM), 1)
    ones = jnp.ones((2 * Q_BLOCK, HEAD_DIM), BF16)
    for sq in range(n_seq):
        for j in range(n_blk):
            rq = slice(j * Q_BLOCK, (j + 1) * Q_BLOCK)
            st_tile = jnp.zeros((Q_BLOCK, HEAD_DIM), F32)
            for hd in range(HEADS_PER_GROUP):
                cs = slice(hd * HEAD_DIM, (hd + 1) * HEAD_DIM)
                q = q_ref[sq, rq, cs]
                if j == 0:
                    kk = jnp.concatenate([kp_ref[sq, :, cs], k_ref[sq, rq, cs]], axis=0)
                    vv = jnp.concatenate([vp_ref[sq, :, cs], v_ref[sq, rq, cs]], axis=0)
                else:
                    rk = slice((j - 1) * Q_BLOCK, (j + 1) * Q_BLOCK)
                    kk = k_ref[sq, rk, cs]
                    vv = v_ref[sq, rk, cs]
                s = _dot_nt(q, kk) + bias_ref[hd]
                if j == 0:
                    s = jnp.where(col < jnp.where(first, Q_BLOCK, 0), NEG_INF, s)
                m = jnp.max(s, axis=-1, keepdims=True)
                p = jnp.exp(s - m).astype(BF16)
                oe = _dot(p, jnp.concatenate([vv, ones], axis=1))
                o_ref[sq, rq, cs] = oe[:, :HEAD_DIM].astype(BF16)
                st_tile = jnp.where(lane == hd, m, st_tile)
                st_tile = jnp.where(lane == HEADS_PER_GROUP + hd, oe[:, HEAD_DIM:], st_tile)
            st_ref[sq, rq, :] = st_tile


def _swa(q, k, v, band_bias):
    n_seq, length, gw = q.shape
    tq = min(length, SWA_ROWS)
    sb = SWA_ROWS // tq
    ratio = tq // Q_BLOCK
    cur = pl.BlockSpec((sb, tq, gw), lambda s, i: (s, i, 0))
    prev = pl.BlockSpec((sb, Q_BLOCK, gw), lambda s, i: (s, jnp.maximum(i * ratio - 1, 0), 0))
    return pl.pallas_call(
        _swa_kernel,
        grid=(n_seq // sb, length // tq),
        in_specs=[cur, cur, cur, prev, prev, _resident(band_bias.shape)],
        out_specs=[cur, pl.BlockSpec((sb, tq, HEAD_DIM), lambda s, i: (s, i, 0))],
        out_shape=[jax.ShapeDtypeStruct((n_seq, length, gw), BF16),
                   jax.ShapeDtypeStruct((n_seq, length, HEAD_DIM), F32)],
        compiler_params=_cparams(2),
        name="swa",
    )(q, k, v, k, v, band_bias)


DEC_BATCH_TILE = 4
T_PAIR = 2


def _dec_attn_kernel(q_ref, k1, v1, k2, v2, k3, v3, newp_ref, newd_ref, bc_ref, bn1_ref, bn_ref, att_ref):
    k_refs, v_refs = (k1, k2, k3), (v1, v2, v3)
    n_b, _, n_p = q_ref.shape[:3]
    for bi in range(n_b):
        near_k = jnp.concatenate([k1[bi]] * T_PAIR, axis=1)
        near_v = jnp.concatenate([v1[bi]] * T_PAIR, axis=1)
        for tp in range(n_p):
            accs, tops, dens = [], [], []
            for g in range(N_GROUPS):
                qv = q_ref[bi, g, tp]
                if GROUP_DILATIONS[g] == 1:
                    kt, vt = near_k, near_v
                    kn, vn = newd_ref[bi, 0], newd_ref[bi, 1]
                    bn = bn1_ref[tp]
                else:
                    steps = slice(tp * T_PAIR, (tp + 1) * T_PAIR)
                    kt = k_refs[g][bi, :, steps, :, :].reshape(WINDOW_SLOTS, KV_ROWS, HEAD_DIM)
                    vt = v_refs[g][bi, :, steps, :, :].reshape(WINDOW_SLOTS, KV_ROWS, HEAD_DIM)
                    kn, vn = newp_ref[bi, 0, g, tp][None], newp_ref[bi, 1, g, tp][None]
                    bn = bn_ref[g][None]
                lc = jnp.sum(kt * qv[None], axis=-1, keepdims=True) + bc_ref[g, tp]
                ln = jnp.sum(kn * qv[None], axis=-1, keepdims=True) + bn
                m = jnp.maximum(jnp.max(lc, axis=0), jnp.max(ln, axis=0))
                pc = jnp.exp(lc - m[None])
                pn = jnp.exp(ln - m[None])
                dens.append(jnp.sum(pc, axis=0) + jnp.sum(pn, axis=0))
                accs.append(jnp.sum(pc * vt, axis=0) + jnp.sum(pn * vn, axis=0))
                tops.append(m)
            top = jnp.maximum(jnp.maximum(tops[0], tops[1]), tops[2])
            ws = [jnp.exp(m - top) for m in tops]
            tot = ws[0] * dens[0] + ws[1] * dens[1] + ws[2] * dens[2]
            out = (ws[0] / tot) * accs[0]
            for g in (1, 2):
                out = out + (ws[g] / tot) * accs[g]
            att_ref[bi, tp] = out


def _dec_attn(qv, caches, newp, newd, bias_cache, bias_new1, bias_new):
    bd, _, n_p = qv.shape[:3]
    bb = DEC_BATCH_TILE
    hd = (HEADS_PER_GROUP, HEAD_DIM)
    cspecs, cargs = [], []
    for g, dil in enumerate(GROUP_DILATIONS):
        for kv in range(2):
            if dil == 1:
                cspecs.append(pl.BlockSpec((bb, WINDOW_SLOTS, None) + hd, lambda i, kv=kv: (i, 0, kv, 0, 0)))
            else:
                cspecs.append(pl.BlockSpec((bb, WINDOW_SLOTS, n_p * T_PAIR, None) + hd,
                                           lambda i, kv=kv: (i, 0, 0, kv, 0, 0)))
            cargs.append(caches[g])

    def whole(a):
        return pl.BlockSpec((bb,) + a.shape[1:], lambda i: (i,) + (0,) * (a.ndim - 1))

    return pl.pallas_call(
        _dec_attn_kernel,
        grid=(bd // bb,),
        in_specs=[whole(qv)] + cspecs + [whole(newp), whole(newd), _resident(bias_cache.shape),
                                         _resident(bias_new1.shape), _resident(bias_new.shape)],
        out_specs=pl.BlockSpec((bb, n_p, KV_ROWS, HEAD_DIM), lambda i: (i, 0, 0, 0)),
        out_shape=jax.ShapeDtypeStruct((bd, n_p, KV_ROWS, HEAD_DIM), F32),
        compiler_params=_cparams(1),
        name="dec_attn",
    )(qv, *cargs, newp, newd, bias_cache, bias_new1, bias_new)


def _lru_gates(xc, cs, blk, wa_ref, ba_ref, wi_ref, bi_ref, lam_ref):
    xcb = xc.astype(BF16)
    tr = jnp.tanh(_dot(xcb, wa_ref[blk]) + 0.5 * ba_ref[:, cs])
    ti = jnp.tanh(_dot(xcb, wi_ref[blk]) + 0.5 * bi_ref[:, cs])
    lam = lam_ref[:, cs]
    log_sig = jnp.minimum(lam, 0.0) - jnp.log1p(jnp.exp(-jnp.abs(lam)))
    half_c = (0.5 * RG_C) * log_sig
    log_a = half_c * tr + half_c
    a = jnp.exp(log_a)
    hx = 0.5 * xc
    y = jnp.tanh(log_a) * (-1.0 - a * a)
    root = jnp.where(y > 0.0, y * lax.rsqrt(y), 0.0)
    b = root * (hx * ti + hx)
    return a, b


def _lru_seq_kernel(xb_ref, gy_ref, wconv_ref, bconv_ref, wa_ref, ba_ref, wi_ref, bi_ref, lam_ref, wob_ref,
                    out_ref, hlast_ref, xcat, a_s, b_s, h_sl, hg, hc):
    nb, tl, c = xb_ref.shape
    rows = nb * tl
    halo = (CONV_WIDTH - 1) * nb
    n_sl = c // LANES
    per = GATE_TILE // LANES

    @pl.when(pl.program_id(0) == 0)
    def _():
        xcat[:, 0:halo, :] = jnp.zeros((n_sl, halo, LANES), F32)
        hc[...] = jnp.zeros((nb, c), F32)

    for b in range(nb):
        xf = xb_ref[b].astype(F32)
        for s in range(n_sl):
            xcat[s, pl.ds(halo + b, tl, stride=nb), :] = xf[:, s * LANES:(s + 1) * LANES]
    for blk in range(c // GATE_TILE):
        cs = slice(blk * GATE_TILE, (blk + 1) * GATE_TILE)
        parts = []
        for s in range(blk * per, (blk + 1) * per):
            ls = slice(s * LANES, (s + 1) * LANES)
            xc = bconv_ref[:, ls] + wconv_ref[0:1, ls] * xcat[s, 0:rows, :]
            for j in range(1, CONV_WIDTH):
                xc = xc + wconv_ref[j:j + 1, ls] * xcat[s, j * nb:j * nb + rows, :]
            parts.append(xc)
        a, b = _lru_gates(jnp.concatenate(parts, axis=1), cs, blk, wa_ref, ba_ref, wi_ref, bi_ref, lam_ref)
        a_s[:, cs] = a
        b_s[:, cs] = b
    for s in range(n_sl):
        xcat[s, 0:halo, :] = xcat[s, rows:rows + halo, :]

    def step(t, h):
        rs = pl.ds(pl.multiple_of(t * nb, nb), nb)
        h = a_s[rs, :] * h + b_s[rs, :]
        for s in range(n_sl):
            h_sl[s, rs, :] = h[:, s * LANES:(s + 1) * LANES]
        return h

    h = lax.fori_loop(0, tl, step, hc[...], unroll=8)
    hc[...] = h
    hlast_ref[...] = h
    for b in range(nb):
        for s in range(n_sl):
            ls = slice(s * LANES, (s + 1) * LANES)
            hg[b * tl:(b + 1) * tl, ls] = h_sl[s, pl.ds(b, tl, stride=nb), :].astype(BF16) * gy_ref[b, :, ls]
    out_ref[...] = _dot(hg[...], wob_ref[...]).astype(BF16).reshape(nb, tl, out_ref.shape[2])


def _lru_seq(xb, gy, w_conv, b_conv, wa, ba, wi, bi, lam, w_ob):
    bsz, seq, c = xb.shape
    tl = LRU_STEPS
    rows = bsz * tl
    tile = pl.BlockSpec((bsz, tl, c), lambda i: (0, i, 0))
    consts = [w_conv, b_conv, wa, ba, wi, bi, lam, w_ob]
    return pl.pallas_call(
        _lru_seq_kernel,
        grid=(seq // tl,),
        in_specs=[tile, tile] + [_resident(a.shape) for a in consts],
        out_specs=[pl.BlockSpec((bsz, tl, D_MODEL), lambda i: (0, i, 0)), pl.BlockSpec((bsz, c), lambda i: (0, 0))],
        out_shape=[jax.ShapeDtypeStruct((bsz, seq, D_MODEL), BF16), jax.ShapeDtypeStruct((bsz, c), F32)],
        scratch_shapes=[pltpu.VMEM((c // LANES, rows + (CONV_WIDTH - 1) * bsz, LANES), F32),
                        pltpu.VMEM((rows, c), F32), pltpu.VMEM((rows, c), F32),
                        pltpu.VMEM((c // LANES, rows, LANES), F32), pltpu.VMEM((rows, c), BF16),
                        pltpu.VMEM((bsz, c), F32)],
        compiler_params=_cparams(1),
        name="lru_seq",
    )(xb, gy, *consts)


def _lru_step_kernel(xb_ref, gy_ref, conv0_ref, h0_ref, wconv_ref, bconv_ref, wa_ref, ba_ref, wi_ref, bi_ref,
                     lam_ref, wob_ref, out_ref, hlast_ref, xcat, b_s, *, nb, tl):
    rows = nb * tl
    halo = (CONV_WIDTH - 1) * nb
    xcat[0:halo, :] = conv0_ref[...]
    xcat[halo:halo + rows, :] = xb_ref[...].astype(F32)
    xc = bconv_ref[...] + wconv_ref[0:1, :] * xcat[0:rows, :]
    for j in range(1, CONV_WIDTH):
        xc = xc + wconv_ref[j:j + 1, :] * xcat[j * nb:j * nb + rows, :]
    for blk in range(xc.shape[1] // GATE_TILE):
        cs = slice(blk * GATE_TILE, (blk + 1) * GATE_TILE)
        a, b = _lru_gates(xc[:, cs], cs, blk, wa_ref, ba_ref, wi_ref, bi_ref, lam_ref)
        h = h0_ref[:, cs]
        for t in range(tl):
            rs = slice(t * nb, (t + 1) * nb)
            h = a[rs, :] * h + b[rs, :]
            b_s[rs, cs] = h
        hlast_ref[:, cs] = h
    out_ref[...] = _dot((b_s[...] * gy_ref[...].astype(F32)).astype(BF16), wob_ref[...]).astype(BF16)


def _lru_step(xb, gy, conv0, h0, w_conv, b_conv, wa, ba, wi, bi, lam, w_ob, *, nb, tl):
    n_rows, c = xb.shape
    consts = [conv0, h0, w_conv, b_conv, wa, ba, wi, bi, lam, w_ob]
    full = lambda shape: pl.BlockSpec(shape, lambda i: (0,) * len(shape))
    return pl.pallas_call(
        functools.partial(_lru_step_kernel, nb=nb, tl=tl),
        grid=(1,),
        in_specs=[full(xb.shape), full(gy.shape)] + [full(a.shape) for a in consts],
        out_specs=[full((n_rows, D_MODEL)), full((nb, c))],
        out_shape=[jax.ShapeDtypeStruct((n_rows, D_MODEL), BF16), jax.ShapeDtypeStruct((nb, c), F32)],
        scratch_shapes=[pltpu.VMEM((n_rows + (CONV_WIDTH - 1) * nb, c), F32), pltpu.VMEM((n_rows, c), F32)],
        compiler_params=_cparams(1),
        name="lru_step",
    )(xb, gy, *consts)


FF_CHUNK = 1024
IN_SPLIT = 2
MIX_SPLIT = 2
N_O_SLABS = GROUP_WIDTH // LANES


def _mix_mlp_kernel(x_ref, o1, o2, o3, l1, l2, l3, bout_ref, sga_ref, sgb_ref, pe_ref,
                    woa_ref, wo_ref, gmlp_ref, wup_ref, wdown_ref, gple_ref, wpg_ref, wpe_ref, y_ref,
                    slab_ref, *, dils, split):
    tm = x_ref.shape[0]

    def natural(ref, n_slabs, dil, base):
        if dil == 1:
            return lambda rs: [ref[rs, c * LANES:(c + 1) * LANES].astype(F32) for c in range(n_slabs)]

        def get(rs):
            n = (rs.stop - rs.start) // dil
            m0 = rs.start // dil
            for c in range(n_slabs):
                cols = slice(c * LANES, (c + 1) * LANES)
                if dil == 16:
                    spare = base + c - 2 * (N_O_SLABS + 1)
                    q = (rs.stop - rs.start) // 4
                    for r in range(dil):
                        dst = rs.start + (r % 4) * q + r // 4
                        slab_ref[spare, pl.ds(dst, n, stride=4), :] = ref[r, m0:m0 + n, cols].astype(F32)
                    for r4 in range(4):
                        slab_ref[base + c, pl.ds(rs.start + r4, q, stride=4), :] = (
                            slab_ref[spare, rs.start + r4 * q:rs.start + (r4 + 1) * q, :])
                else:
                    for r in range(dil):
                        slab_ref[base + c, pl.ds(rs.start + r, n, stride=dil), :] = ref[r, m0:m0 + n, cols].astype(F32)
            return [slab_ref[base + c, rs, :] for c in range(n_slabs)]

        return get

    o_get, l_get = [], []
    for g, (o_ref, l_ref) in enumerate(((o1, l1), (o2, l2), (o3, l3))):
        base = g * (N_O_SLABS + 1)
        o_get.append(natural(o_ref, N_O_SLABS, dils[g], base))
        l_get.append(natural(l_ref, 1, dils[g], base + N_O_SLABS))

    def rows_body(rs):
        os_ = [get(rs) for get in o_get]
        sts = [get(rs)[0] for get in l_get]
        dens = [pltpu.roll(st, LANES - HEADS_PER_GROUP, axis=1) for st in sts]
        top = jnp.maximum(jnp.maximum(sts[0], sts[1]), sts[2])
        ws = [jnp.exp(st - top) for st in sts]
        tot = ws[0] * dens[0] + ws[1] * dens[1] + ws[2] * dens[2]
        ws = [w / tot for w in ws]
        heads = []
        for hd in range(HEADS_PER_GROUP):
            acc = ws[0][:, hd:hd + 1] * os_[0][hd]
            for g in (1, 2):
                acc = acc + ws[g][:, hd:hd + 1] * os_[g][hd]
            heads.append(acc.astype(BF16))
        att = jnp.concatenate(heads, axis=1)
        yield
        a_out = _dot(att, woa_ref[...])
        mix = (sga_ref[rs, :].astype(F32) * a_out
               + sgb_ref[rs, :].astype(F32) * bout_ref[rs, :].astype(F32))
        yield
        x = x_ref[rs, :] + _dot(mix.astype(BF16), wo_ref[...])
        h = _rms(x, gmlp_ref[...]).astype(BF16)
        yield
        acc = jnp.zeros(x.shape, F32)
        for c in range(D_FF // FF_CHUNK):
            cs = slice(c * FF_CHUNK, (c + 1) * FF_CHUNK)
            u = jnp.square(jnp.maximum(_dot(h, wup_ref[:, cs]), 0.0))
            acc = acc + _dot(u.astype(BF16), wdown_ref[cs, :])
            yield
        x = x + acc
        h = _rms(x, gple_ref[...]).astype(BF16)
        yield
        gate = jax.nn.sigmoid(_dot(h, wpg_ref[...]))
        y_ref[rs, :] = x + gate * _dot(pe_ref[rs, :].astype(BF16), wpe_ref[...])

    n_rows = tm // split
    chunks = [rows_body(slice(part * n_rows, (part + 1) * n_rows)) for part in range(split)]
    while chunks:
        chunks = [c for c in chunks if next(c, True) is None]


def _mix_mlp_prompt(x, os_, ls, bout, sga, sgb, pe, weights):
    bsz, seq, _ = x.shape
    tm = ROW_TILE

    def nat(width):
        return pl.BlockSpec((None, tm, width), lambda b, i: (b, i, 0))

    def grouped(width):
        return [nat(width) if dil == 1 else
                pl.BlockSpec((None, dil, tm // dil, width), lambda b, i: (b, 0, i, 0))
                for dil in GROUP_DILATIONS]

    ws = list(weights)
    specs = ([nat(D_MODEL)] + grouped(GROUP_WIDTH) + grouped(HEAD_DIM) + [nat(D_MODEL)] * 3 + [nat(PLE_DIM)]
             + [_resident(w.shape) for w in ws])
    return pl.pallas_call(
        functools.partial(_mix_mlp_kernel, dils=GROUP_DILATIONS, split=MIX_SPLIT),
        grid=(bsz, seq // tm),
        in_specs=specs,
        out_specs=nat(D_MODEL),
        out_shape=jax.ShapeDtypeStruct((bsz, seq, D_MODEL), F32),
        scratch_shapes=[pltpu.VMEM((N_GROUPS * (N_O_SLABS + 1), tm, LANES), F32)],
        compiler_params=_cparams(2),
        name="mix_mlp_prompt",
    )(x, *os_, *ls, bout, sga, sgb, pe, *ws)


def _mix_mlp_sample(x, os_, ls, bout, sga, sgb, pe, weights):
    n = x.shape[0]
    tm = SAMPLE_TILE

    def rows(width):
        return pl.BlockSpec((tm, width), lambda i: (i, 0))

    ws = list(weights)
    specs = ([rows(D_MODEL)] + [rows(GROUP_WIDTH)] * 3 + [rows(HEAD_DIM)] * 3 + [rows(D_MODEL)] * 3
             + [rows(PLE_DIM)] + [_resident(w.shape) for w in ws])
    return pl.pallas_call(
        functools.partial(_mix_mlp_kernel, dils=(1, 1, 1), split=1),
        grid=(n // tm,),
        in_specs=specs,
        out_specs=rows(D_MODEL),
        out_shape=jax.ShapeDtypeStruct((n, D_MODEL), F32),
        scratch_shapes=[pltpu.VMEM((N_GROUPS * (N_O_SLABS + 1), tm, LANES), F32)],
        compiler_params=_cparams(1),
        name="mix_mlp_sample",
    )(x, *os_, *ls, bout, sga, sgb, pe, *ws)


def _t5_bucket(dist):
    max_exact = REL_BUCKETS // 2
    d = jnp.maximum(dist, 1).astype(F32)
    large = max_exact + (jnp.log(d / max_exact) / math.log(REL_MAX_DIST / max_exact)
                         * (REL_BUCKETS - max_exact)).astype(jnp.int32)
    large = jnp.minimum(large, REL_BUCKETS - 1)
    return jnp.where(dist < max_exact, dist, large)


def _slot_bias(rel_bias, g):
    dil = GROUP_DILATIONS[g]
    dist = dil * jnp.arange(WINDOW_SLOTS + 1, dtype=jnp.int32)
    hs = slice(g * HEADS_PER_GROUP, (g + 1) * HEADS_PER_GROUP)
    return rel_bias[_t5_bucket(dist)][:, hs].astype(F32)


def _band_bias(slot_bias):
    n_h = slot_bias.shape[1]
    pad = jnp.full((n_h, Q_BLOCK - 1), NEG_INF, F32)
    ext = jnp.concatenate([pad, slot_bias[::-1].T, pad, jnp.full((n_h, 1), NEG_INF, F32)], axis=1)
    width = ext.shape[1]
    skew = jnp.broadcast_to(ext[:, None, :], (n_h, Q_BLOCK, width)).reshape(n_h, Q_BLOCK * width)
    skew = skew[:, :Q_BLOCK * (width - 1)].reshape(n_h, Q_BLOCK, width - 1)
    return skew[:, :, Q_BLOCK - 1:3 * Q_BLOCK - 1]


def _pair_rows_table(per_step):
    pairs = [jnp.concatenate(per_step[p * T_PAIR:(p + 1) * T_PAIR], axis=-1) for p in range(len(per_step) // T_PAIR)]
    tbl = jnp.stack(pairs)
    return jnp.broadcast_to(tbl[..., None], tbl.shape + (HEAD_DIM,))


def _dec_bias(slot_biases, n_t):
    n_h = HEADS_PER_GROUP
    neg = lambda n: jnp.full((n, n_h), NEG_INF, F32)
    bc = []
    for g in range(N_GROUPS):
        near_first = slot_biases[g][::-1]
        if GROUP_DILATIONS[g] == 1:
            per_step = [jnp.concatenate([neg(t), near_first[:WINDOW_SLOTS - t]], axis=0) for t in range(n_t)]
        else:
            per_step = [near_first[:WINDOW_SLOTS]] * n_t
        bc.append(_pair_rows_table(per_step))
    sb = slot_biases[0]
    bn1 = _pair_rows_table([jnp.concatenate([sb[:t + 1][::-1], neg(n_t - 1 - t)], axis=0) for t in range(n_t)])
    own = jnp.stack([jnp.concatenate([s[:1]] * T_PAIR, axis=-1)[0] for s in slot_biases])
    bn = jnp.broadcast_to(own[..., None], own.shape + (HEAD_DIM,))
    return jnp.stack(bc), bn1, bn


def _gate_tiles(w):
    per = GATE_TILE // LRU_BLOCK
    w = w.reshape(LRU_WIDTH // GATE_TILE, per, LRU_BLOCK, LRU_BLOCK)
    eye = jnp.eye(per, dtype=w.dtype)
    return jnp.einsum('npij,pq->npiqj', w, eye).reshape(-1, GATE_TILE, GATE_TILE).astype(BF16)


def kernel(x_prompt, x_sample, p_prompt, p_sample, cache_kv1, cache_kv2, cache_kv3, state_conv, state_lru,
           rel_bias, g_mix, w_in, g_q, g_k, w_oa, w_conv, b_conv, w_rg_a, b_rg_a, w_rg_i, b_rg_i, lam,
           w_ob, w_o, g_mlp, w_up, w_down, g_ple, w_ple_gate, w_ple_in):
    depth = w_in.shape[0]
    assert depth == 1
    bsz, seq, _ = x_prompt.shape
    bd, n_t, _ = x_sample.shape
    gw = GROUP_WIDTH
    row = lambda a: a.reshape(1, -1)

    slot_biases = [_slot_bias(rel_bias, g) for g in range(N_GROUPS)]
    band = [_band_bias(sb) for sb in slot_biases]
    bias_cache, bias_new1, bias_new = _dec_bias(slot_biases, n_t)

    i = 0
    w_in_b = w_in[i].astype(BF16)
    gm, gq, gk = row(g_mix[i]), row(g_q[i]), row(g_k[i])
    lru_w = (w_conv[i], row(b_conv[i]), _gate_tiles(0.5 * w_rg_a[i]), row(b_rg_a[i]), _gate_tiles(0.5 * w_rg_i[i]),
             row(b_rg_i[i]), row(lam[i]), w_ob[i].astype(BF16))
    mlp_w = (w_oa[i].astype(BF16), w_o[i].astype(BF16), row(g_mlp[i]), w_up[i].astype(BF16),
             w_down[i].astype(BF16), row(g_ple[i]), w_ple_gate[i].astype(BF16), w_ple_in[i].astype(BF16))

    (q1, q2, q3, k1, k2, k3, v1, v2, v3, xb, gy, sga, sgb, kvt1, kvt2, kvt3, xbt) = _in_proj_prompt(
        x_prompt, gm, w_in_b, gq, gk)
    os_, ls = [], []
    for g, (qq, kk, vv) in enumerate(((q1, k1, v1), (q2, k2, v2), (q3, k3, v3))):
        dil = GROUP_DILATIONS[g]
        ns, ln = bsz * dil, seq // dil
        o, l = _swa(qq.reshape(ns, ln, gw), kk.reshape(ns, ln, gw), vv.reshape(ns, ln, gw), band[g])
        os_.append(o if dil == 1 else o.reshape(bsz, dil, ln, gw))
        ls.append(l if dil == 1 else l.reshape(bsz, dil, ln, HEAD_DIM))
    bout, lru_p = _lru_seq(xb, gy, *lru_w)
    y_prompt = _mix_mlp_prompt(x_prompt, os_, ls, bout, sga, sgb, p_prompt[i], mlp_w)
    kv_p = [t.reshape(bsz, GROUP_WINDOWS[g], 2, HEADS_PER_GROUP, HEAD_DIM)[None]
            for g, t in enumerate((kvt1, kvt2, kvt3))]
    conv_p = xbt[:, SUBLANES - (CONV_WIDTH - 1):][None]
    lru_p = lru_p[None]

    n_s = bd * n_t
    (sq1, sq2, sq3, _, _, _, _, _, _, sxb, sgy, ssga, ssgb, skv1, skv2, skv3, sxbt) = _in_proj_sample(
        x_sample.reshape(n_s, D_MODEL), gm, w_in_b, gq, gk)
    n_p = n_t // T_PAIR
    qv = jnp.stack([q.astype(F32).reshape(bd, n_p, KV_ROWS, HEAD_DIM) for q in (sq1, sq2, sq3)], axis=1)
    kv_new = jnp.stack([t.reshape(bd, n_t, 2, HEADS_PER_GROUP, HEAD_DIM) for t in (skv1, skv2, skv3)], axis=1)
    kv_new = jnp.moveaxis(kv_new, 3, 1)
    newp = kv_new.reshape(bd, 2, N_GROUPS, n_p, KV_ROWS, HEAD_DIM)
    newd = jnp.concatenate([kv_new[:, :, 0]] * T_PAIR, axis=-2)
    caches = [c.reshape((bd, WINDOW_SLOTS) + ((dil,) if dil > 1 else ()) + (2, HEADS_PER_GROUP, HEAD_DIM))
              for c, dil in zip((cache_kv1, cache_kv2, cache_kv3), GROUP_DILATIONS)]
    att_s = _dec_attn(qv, caches, newp, newd, bias_cache, bias_new1, bias_new).reshape(n_s, gw)

    def to_tb(a):
        return a.reshape(bd, n_t, -1).transpose(1, 0, 2).reshape(n_t * bd, -1)

    def from_tb(a):
        return a.reshape(n_t, bd, -1).transpose(1, 0, 2).reshape(bd * n_t, -1)

    conv0_s = state_conv[i].transpose(1, 0, 2).reshape((CONV_WIDTH - 1) * bd, LRU_WIDTH)
    bout_s, lru_s = _lru_step(to_tb(sxb), to_tb(sgy), conv0_s, state_lru[i].astype(F32), *lru_w, nb=bd, tl=n_t)
    zero_o = jnp.zeros((n_s, gw), BF16)
    lane = jnp.arange(HEAD_DIM)[None, :]
    den_one = jnp.where(lane < HEADS_PER_GROUP, 0.0, 1.0)
    st_on = jnp.broadcast_to(den_one, (n_s, HEAD_DIM)).astype(F32)
    st_off = jnp.broadcast_to(jnp.where(lane < HEADS_PER_GROUP, NEG_INF, den_one), (n_s, HEAD_DIM)).astype(F32)
    y_sample = _mix_mlp_sample(x_sample.reshape(n_s, D_MODEL), [att_s.astype(BF16), zero_o, zero_o],
                               [st_on, st_off, st_off], from_tb(bout_s), ssga, ssgb,
                               p_sample[i].reshape(n_s, PLE_DIM), mlp_w).reshape(bd, n_t, D_MODEL)
    kv_s = [t.reshape(bd, n_t, 2, HEADS_PER_GROUP, HEAD_DIM)[None] for t in (skv1, skv2, skv3)]
    xcat_s = jnp.concatenate([state_conv[i], sxbt.reshape(bd, n_t, LRU_WIDTH)], axis=1)
    conv_s = xcat_s[:, n_t:][None]

    return (y_prompt, y_sample, kv_p[0], kv_p[1], kv_p[2], conv_p, lru_p,
            kv_s[0], kv_s[1], kv_s[2], conv_s, lru_s[None])
```

```python
import functools
import math

import jax
import jax.numpy as jnp
from jax import lax
from jax.experimental import pallas as pl
from jax.experimental.pallas import tpu as pltpu

F32 = jnp.float32
BF16 = jnp.bfloat16

D_MODEL = 1024
HEAD_DIM = 128
HEADS_PER_GROUP = 4
GROUP_WINDOWS = (128, 512, 2048)
GROUP_DILATIONS = (1, 4, 16)
N_GROUPS = 3
GROUP_WIDTH = HEADS_PER_GROUP * HEAD_DIM
ATT_WIDTH = N_GROUPS * GROUP_WIDTH
ATT_SCALE = HEAD_DIM ** -0.5
LRU_WIDTH = D_MODEL
LRU_BLOCKS = 16
LRU_BLOCK = LRU_WIDTH // LRU_BLOCKS
CONV_WIDTH = 4
RG_C = 8.0
D_FF = 4 * D_MODEL
PLE_DIM = 256
REL_BUCKETS = 32
REL_MAX_DIST = 2048
NORM_EPS = 1e-6
NEG_INF = -1e30
WINDOW_SLOTS = 128
KV_ROWS = 2 * HEADS_PER_GROUP

OFF_Q, OFF_K, OFF_V = 0, ATT_WIDTH, 2 * ATT_WIDTH
OFF_XB = 3 * ATT_WIDTH
OFF_YB = OFF_XB + LRU_WIDTH
OFF_GA = OFF_YB + LRU_WIDTH
OFF_GB = OFF_GA + D_MODEL

SUBLANES = 8
LANES = 128
ROW_TILE = 512
SAMPLE_TILE = 256
Q_BLOCK = 128
SWA_ROWS = 2048
LRU_STEPS = 128
GATE_TILE = 256
VMEM_LIMIT = 58 * 1024 * 1024


def _cparams(n_axes):
    return pltpu.CompilerParams(dimension_semantics=("arbitrary",) * n_axes,
                                vmem_limit_bytes=VMEM_LIMIT)


def _resident(shape):
    nd = len(shape)
    return pl.BlockSpec(shape, lambda *_: (0,) * nd, pipeline_mode=pl.Buffered(1))


def _rms(x, gain):
    return x * lax.rsqrt(jnp.mean(x * x, axis=-1, keepdims=True) + NORM_EPS) * gain


def _dot(a, b):
    return jnp.dot(a, b, preferred_element_type=F32)


def _dot_nt(a, b):
    return lax.dot_general(a, b, (((1,), (1,)), ((), ())), preferred_element_type=F32)


def _in_proj_kernel(x_ref, gmix_ref, w_ref, gq_ref, gk_ref,
                    q1, q2, q3, k1, k2, k3, v1, v2, v3, xb_ref, gy_ref, sga_ref, sgb_ref,
                    kv1, kv2, kv3, xbt_ref, slab_ref, *, dils, split):
    q_refs, k_refs, v_refs, kv_refs = (q1, q2, q3), (k1, k2, k3), (v1, v2, v3), (kv1, kv2, kv3)
    tm = x_ref.shape[0]
    tr = tm // split
    gq = gq_ref[...] * ATT_SCALE
    gk = gk_ref[...]

    def chunk(c):
        r0 = c * tr
        h = _rms(x_ref[r0:r0 + tr, :], gmix_ref[...]).astype(BF16)

        def put(ref, hd, val, dil, slab):
            cs = slice(hd * HEAD_DIM, (hd + 1) * HEAD_DIM)
            if dil == 1:
                ref[r0:r0 + tr, cs] = val.astype(BF16)
                return
            slab_ref[slab, r0:r0 + tr, :] = val
            n = tr // dil
            if dil == 16:
                spare = slab - 3 * HEADS_PER_GROUP
                q = tr // 4
                for r4 in range(4):
                    slab_ref[spare, r0 + r4 * q:r0 + (r4 + 1) * q, :] = slab_ref[slab, pl.ds(r0 + r4, q, stride=4), :]
                for r in range(dil):
                    src = r0 + (r % 4) * q + r // 4
                    ref[r, c * n:(c + 1) * n, cs] = slab_ref[spare, pl.ds(src, n, stride=4), :].astype(BF16)
                return
            for r in range(dil):
                ref[r, c * n:(c + 1) * n, cs] = slab_ref[slab, pl.ds(r0 + r, n, stride=dil), :].astype(BF16)

        def put_tail(ref, row, val):
            n_tail = ref.shape[0] // KV_ROWS
            lo = max(r0, tm - n_tail)
            cnt = r0 + tr - lo
            if cnt > 0:
                ref[pl.ds((lo - (tm - n_tail)) * KV_ROWS + row, cnt, stride=KV_ROWS), :] = val[lo - r0:, :]

        def group(g):
            c0 = g * GROUP_WIDTH
            s0 = max(g - 1, 0) * 3 * HEADS_PER_GROUP
            yq = _dot(h, w_ref[:, OFF_Q + c0:OFF_Q + c0 + GROUP_WIDTH])
            yk = _dot(h, w_ref[:, OFF_K + c0:OFF_K + c0 + GROUP_WIDTH])
            yv = _dot(h, w_ref[:, OFF_V + c0:OFF_V + c0 + GROUP_WIDTH])
            for hd in range(HEADS_PER_GROUP):
                cs = slice(hd * HEAD_DIM, (hd + 1) * HEAD_DIM)
                put(q_refs[g], hd, _rms(yq[:, cs], gq), dils[g], s0 + hd)
                kn = _rms(yk[:, cs], gk)
                put(k_refs[g], hd, kn, dils[g], s0 + HEADS_PER_GROUP + hd)
                put(v_refs[g], hd, yv[:, cs], dils[g], s0 + 2 * HEADS_PER_GROUP + hd)
                put_tail(kv_refs[g], hd, kn)
                put_tail(kv_refs[g], HEADS_PER_GROUP + hd, yv[:, cs])

        def wide(off):
            return _dot(h, w_ref[:, off:off + D_MODEL])

        rs = slice(r0, r0 + tr)
        group(2)
        y = wide(OFF_XB)
        xb_ref[rs, :] = y.astype(BF16)
        n_x = xbt_ref.shape[0]
        if r0 + tr == tm:
            xbt_ref[...] = y[tr - n_x:, :]
        gy_ref[rs, :] = jax.nn.gelu(wide(OFF_YB)).astype(BF16)
        group(1)
        sga_ref[rs, :] = jax.nn.sigmoid(wide(OFF_GA)).astype(BF16)
        group(0)
        sgb_ref[rs, :] = jax.nn.sigmoid(wide(OFF_GB)).astype(BF16)

    for c in range(split):
        chunk(c)


def _in_proj_prompt(x, g_mix, w_in, g_q, g_k):
    bsz, seq, _ = x.shape
    tm = ROW_TILE
    n_tiles = seq // tm
    gw = GROUP_WIDTH

    def nat(width):
        return pl.BlockSpec((None, tm, width), lambda b, i: (b, i, 0))

    def sds(shape, dt):
        return jax.ShapeDtypeStruct(shape, dt)

    qkv_shapes, qkv_specs = [], []
    for dil in GROUP_DILATIONS:
        if dil == 1:
            qkv_shapes.append(sds((bsz, seq, gw), BF16))
            qkv_specs.append(nat(gw))
        else:
            qkv_shapes.append(sds((bsz, dil, seq // dil, gw), BF16))
            qkv_specs.append(pl.BlockSpec((None, dil, tm // dil, gw), lambda b, i: (b, 0, i, 0)))
    tail_rows = [min(w, tm) for w in GROUP_WINDOWS]
    tail_start = [n_tiles - max(w // tm, 1) for w in GROUP_WINDOWS]
    tail_specs = [pl.BlockSpec((None, r * KV_ROWS, HEAD_DIM), lambda b, i, s=s: (b, jnp.maximum(i - s, 0), 0))
                  for r, s in zip(tail_rows, tail_start)]
    out_shape = (qkv_shapes * 3 + [sds((bsz, seq, LRU_WIDTH), BF16)] * 2 + [sds((bsz, seq, D_MODEL), BF16)] * 2
                 + [sds((bsz, w * KV_ROWS, HEAD_DIM), F32) for w in GROUP_WINDOWS]
                 + [sds((bsz, SUBLANES, LRU_WIDTH), F32)])
    out_specs = (qkv_specs * 3 + [nat(LRU_WIDTH)] * 2 + [nat(D_MODEL)] * 2 + tail_specs
                 + [pl.BlockSpec((None, SUBLANES, LRU_WIDTH), lambda b, i: (b, 0, 0))])
    return pl.pallas_call(
        functools.partial(_in_proj_kernel, dils=GROUP_DILATIONS, split=IN_SPLIT),
        grid=(bsz, n_tiles),
        in_specs=[nat(D_MODEL), _resident(g_mix.shape), _resident(w_in.shape),
                  _resident(g_q.shape), _resident(g_k.shape)],
        out_specs=out_specs,
        out_shape=out_shape,
        scratch_shapes=[pltpu.VMEM(((N_GROUPS - 1) * 3 * HEADS_PER_GROUP, tm, HEAD_DIM), F32)],
        compiler_params=_cparams(2),
        name="in_proj_prompt",
    )(x, g_mix, w_in, g_q, g_k)


def _in_proj_sample(x, g_mix, w_in, g_q, g_k):
    n = x.shape[0]
    tm = SAMPLE_TILE
    gw = GROUP_WIDTH

    def rows(width):
        return pl.BlockSpec((tm, width), lambda i: (i, 0))

    def sds(width, dt):
        return jax.ShapeDtypeStruct((n, width), dt)

    out_shape = ([sds(gw, BF16)] * 9 + [sds(LRU_WIDTH, BF16)] * 2 + [sds(D_MODEL, BF16)] * 2
                 + [jax.ShapeDtypeStruct((n * KV_ROWS, HEAD_DIM), F32)] * 3 + [sds(LRU_WIDTH, F32)])
    out_specs = ([rows(gw)] * 9 + [rows(LRU_WIDTH)] * 2 + [rows(D_MODEL)] * 2
                 + [pl.BlockSpec((tm * KV_ROWS, HEAD_DIM), lambda i: (i, 0))] * 3 + [rows(LRU_WIDTH)])
    return pl.pallas_call(
        functools.partial(_in_proj_kernel, dils=(1, 1, 1), split=1),
        grid=(n // tm,),
        in_specs=[rows(D_MODEL), _resident(g_mix.shape), _resident(w_in.shape),
                  _resident(g_q.shape), _resident(g_k.shape)],
        out_specs=out_specs,
        out_shape=out_shape,
        scratch_shapes=[pltpu.VMEM(((N_GROUPS - 1) * 3 * HEADS_PER_GROUP, tm, HEAD_DIM), F32)],
        compiler_params=_cparams(1),
        name="in_proj_sample",
    )(x, g_mix, w_in, g_q, g_k)


def _swa_kernel(q_ref, k_ref, v_ref, kp_ref, vp_ref, bias_ref, o_ref, st_ref):
    first = pl.program_id(1) == 0
    n_seq = q_ref.shape[0]
    n_blk = q_ref.shape[1] // Q_BLOCK
    col = lax.broadcasted_iota(jnp.int32, (Q_BLOCK, 2 * Q_BLOCK), 1)
    lane = lax.broadcasted_iota(jnp.int32, (Q_BLOCK, HEAD_DIM), 1)
    ones = jnp.ones((2 * Q_BLOCK, HEAD_DIM), BF16)
    for sq in range(n_seq):
        for j in range(n_blk):
            rq = slice(j * Q_BLOCK, (j + 1) * Q_BLOCK)
            st_tile = jnp.zeros((Q_BLOCK, HEAD_DIM), F32)
            for hd in range(HEADS_PER_GROUP):
                cs = slice(hd * HEAD_DIM, (hd + 1) * HEAD_DIM)
                q = q_ref[sq, rq, cs]
                if j == 0:
                    kk = jnp.concatenate([kp_ref[sq, :, cs], k_ref[sq, rq, cs]], axis=0)
                    vv = jnp.concatenate([vp_ref[sq, :, cs], v_ref[sq, rq, cs]], axis=0)
                else:
                    rk = slice((j - 1) * Q_BLOCK, (j + 1) * Q_BLOCK)
                    kk = k_ref[sq, rk, cs]
                    vv = v_ref[sq, rk, cs]
                s = _dot_nt(q, kk) + bias_ref[hd]
                if j == 0:
                    s = jnp.where(col < jnp.where(first, Q_BLOCK, 0), NEG_INF, s)
                m = jnp.max(s, axis=-1, keepdims=True)
                p = jnp.exp(s - m).astype(BF16)
                oe = _dot(p, jnp.concatenate([vv, ones], axis=1))
                o_ref[sq, rq, cs] = oe[:, :HEAD_DIM].astype(BF16)
                st_tile = jnp.where(lane == hd, m, st_tile)
                st_tile = jnp.where(lane == HEADS_PER_GROUP + hd, oe[:, HEAD_DIM:], st_tile)
            st_ref[sq, rq, :] = st_tile


def _swa(q, k, v, band_bias):
    n_seq, length, gw = q.shape
    tq = min(length, SWA_ROWS)
    sb = SWA_ROWS // tq
    ratio = tq // Q_BLOCK
    cur = pl.BlockSpec((sb, tq, gw), lambda s, i: (s, i, 0))
    prev = pl.BlockSpec((sb, Q_BLOCK, gw), lambda s, i: (s, jnp.maximum(i * ratio - 1, 0), 0))
    return pl.pallas_call(
        _swa_kernel,
        grid=(n_seq // sb, length // tq),
        in_specs=[cur, cur, cur, prev, prev, _resident(band_bias.shape)],
        out_specs=[cur, pl.BlockSpec((sb, tq, HEAD_DIM), lambda s, i: (s, i, 0))],
        out_shape=[jax.ShapeDtypeStruct((n_seq, length, gw), BF16),
                   jax.ShapeDtypeStruct((n_seq, length, HEAD_DIM), F32)],
        compiler_params=_cparams(2),
        name="swa",
    )(q, k, v, k, v, band_bias)


DEC_BATCH_TILE = 4
T_PAIR = 2


def _dec_attn_kernel(q_ref, k1, v1, k2, v2, k3, v3, newp_ref, newd_ref, bc_ref, bn1_ref, bn_ref, att_ref):
    k_refs, v_refs = (k1, k2, k3), (v1, v2, v3)
    n_b, _, n_p = q_ref.shape[:3]
    for bi in range(n_b):
        near_k = jnp.concatenate([k1[bi]] * T_PAIR, axis=1)
        near_v = jnp.concatenate([v1[bi]] * T_PAIR, axis=1)
        for tp in range(n_p):
            accs, tops, dens = [], [], []
            for g in range(N_GROUPS):
                qv = q_ref[bi, g, tp]
                if GROUP_DILATIONS[g] == 1:
                    kt, vt = near_k, near_v
                    kn, vn = newd_ref[bi, 0], newd_ref[bi, 1]
                    bn = bn1_ref[tp]
                else:
                    steps = slice(tp * T_PAIR, (tp + 1) * T_PAIR)
                    kt = k_refs[g][bi, :, steps, :, :].reshape(WINDOW_SLOTS, KV_ROWS, HEAD_DIM)
                    vt = v_refs[g][bi, :, steps, :, :].reshape(WINDOW_SLOTS, KV_ROWS, HEAD_DIM)
                    kn, vn = newp_ref[bi, 0, g, tp][None], newp_ref[bi, 1, g, tp][None]
                    bn = bn_ref[g][None]
                lc = jnp.sum(kt * qv[None], axis=-1, keepdims=True) + bc_ref[g, tp]
                ln = jnp.sum(kn * qv[None], axis=-1, keepdims=True) + bn
                m = jnp.maximum(jnp.max(lc, axis=0), jnp.max(ln, axis=0))
                pc = jnp.exp(lc - m[None])
                pn = jnp.exp(ln - m[None])
                dens.append(jnp.sum(pc, axis=0) + jnp.sum(pn, axis=0))
                accs.append(jnp.sum(pc * vt, axis=0) + jnp.sum(pn * vn, axis=0))
                tops.append(m)
            top = jnp.maximum(jnp.maximum(tops[0], tops[1]), tops[2])
            ws = [jnp.exp(m - top) for m in tops]
            tot = ws[0] * dens[0] + ws[1] * dens[1] + ws[2] * dens[2]
            out = (ws[0] / tot) * accs[0]
            for g in (1, 2):
                out = out + (ws[g] / tot) * accs[g]
            att_ref[bi, tp] = out


def _dec_attn(qv, caches, newp, newd, bias_cache, bias_new1, bias_new):
    bd, _, n_p = qv.shape[:3]
    bb = DEC_BATCH_TILE
    hd = (HEADS_PER_GROUP, HEAD_DIM)
    cspecs, cargs = [], []
    for g, dil in enumerate(GROUP_DILATIONS):
        for kv in range(2):
            if dil == 1:
                cspecs.append(pl.BlockSpec((bb, WINDOW_SLOTS, None) + hd, lambda i, kv=kv: (i, 0, kv, 0, 0)))
            else:
                cspecs.append(pl.BlockSpec((bb, WINDOW_SLOTS, n_p * T_PAIR, None) + hd,
                                           lambda i, kv=kv: (i, 0, 0, kv, 0, 0)))
            cargs.append(caches[g])

    def whole(a):
        return pl.BlockSpec((bb,) + a.shape[1:], lambda i: (i,) + (0,) * (a.ndim - 1))

    return pl.pallas_call(
        _dec_attn_kernel,
        grid=(bd // bb,),
        in_specs=[whole(qv)] + cspecs + [whole(newp), whole(newd), _resident(bias_cache.shape),
                                         _resident(bias_new1.shape), _resident(bias_new.shape)],
        out_specs=pl.BlockSpec((bb, n_p, KV_ROWS, HEAD_DIM), lambda i: (i, 0, 0, 0)),
        out_shape=jax.ShapeDtypeStruct((bd, n_p, KV_ROWS, HEAD_DIM), F32),
        compiler_params=_cparams(1),
        name="dec_attn",
    )(qv, *cargs, newp, newd, bias_cache, bias_new1, bias_new)


def _lru_gates(xc, cs, blk, wa_ref, ba_ref, wi_ref, bi_ref, lam_ref):
    xcb = xc.astype(BF16)
    tr = jnp.tanh(_dot(xcb, wa_ref[blk]) + 0.5 * ba_ref[:, cs])
    ti = jnp.tanh(_dot(xcb, wi_ref[blk]) + 0.5 * bi_ref[:, cs])
    lam = lam_ref[:, cs]
    log_sig = jnp.minimum(lam, 0.0) - jnp.log1p(jnp.exp(-jnp.abs(lam)))
    half_c = (0.5 * RG_C) * log_sig
    log_a = half_c * tr + half_c
    a = jnp.exp(log_a)
    hx = 0.5 * xc
    y = jnp.tanh(log_a) * (-1.0 - a * a)
    root = jnp.where(y > 0.0, y * lax.rsqrt(y), 0.0)
    b = root * (hx * ti + hx)
    return a, b


def _lru_seq_kernel(xb_ref, gy_ref, wconv_ref, bconv_ref, wa_ref, ba_ref, wi_ref, bi_ref, lam_ref, wob_ref,
                    out_ref, hlast_ref, xcat, a_s, b_s, h_sl, hg, hc):
    nb, tl, c = xb_ref.shape
    rows = nb * tl
    halo = (CONV_WIDTH - 1) * nb
    n_sl = c // LANES
    per = GATE_TILE // LANES

    @pl.when(pl.program_id(0) == 0)
    def _():
        xcat[:, 0:halo, :] = jnp.zeros((n_sl, halo, LANES), F32)
        hc[...] = jnp.zeros((nb, c), F32)

    for b in range(nb):
        xf = xb_ref[b].astype(F32)
        for s in range(n_sl):
            xcat[s, pl.ds(halo + b, tl, stride=nb), :] = xf[:, s * LANES:(s + 1) * LANES]
    for blk in range(c // GATE_TILE):
        cs = slice(blk * GATE_TILE, (blk + 1) * GATE_TILE)
        parts = []
        for s in range(blk * per, (blk + 1) * per):
            ls = slice(s * LANES, (s + 1) * LANES)
            xc = bconv_ref[:, ls] + wconv_ref[0:1, ls] * xcat[s, 0:rows, :]
            for j in range(1, CONV_WIDTH):
                xc = xc + wconv_ref[j:j + 1, ls] * xcat[s, j * nb:j * nb + rows, :]
            parts.append(xc)
        a, b = _lru_gates(jnp.concatenate(parts, axis=1), cs, blk, wa_ref, ba_ref, wi_ref, bi_ref, lam_ref)
        a_s[:, cs] = a
        b_s[:, cs] = b
    for s in range(n_sl):
        xcat[s, 0:halo, :] = xcat[s, rows:rows + halo, :]

    def step(t, h):
        rs = pl.ds(pl.multiple_of(t * nb, nb), nb)
        h = a_s[rs, :] * h + b_s[rs, :]
        for s in range(n_sl):
            h_sl[s, rs, :] = h[:, s * LANES:(s + 1) * LANES]
        return h

    h = lax.fori_loop(0, tl, step, hc[...], unroll=8)
    hc[...] = h
    hlast_ref[...] = h
    for b in range(nb):
        for s in range(n_sl):
            ls = slice(s * LANES, (s + 1) * LANES)
            hg[b * tl:(b + 1) * tl, ls] = h_sl[s, pl.ds(b, tl, stride=nb), :].astype(BF16) * gy_ref[b, :, ls]
    out_ref[...] = _dot(hg[...], wob_ref[...]).astype(BF16).reshape(nb, tl, out_ref.shape[2])


def _lru_seq(xb, gy, w_conv, b_conv, wa, ba, wi, bi, lam, w_ob):
    bsz, seq, c = xb.shape
    tl = LRU_STEPS
    rows = bsz * tl
    tile = pl.BlockSpec((bsz, tl, c), lambda i: (0, i, 0))
    consts = [w_conv, b_conv, wa, ba, wi, bi, lam, w_ob]
    return pl.pallas_call(
        _lru_seq_kernel,
        grid=(seq // tl,),
        in_specs=[tile, tile] + [_resident(a.shape) for a in consts],
        out_specs=[pl.BlockSpec((bsz, tl, D_MODEL), lambda i: (0, i, 0)), pl.BlockSpec((bsz, c), lambda i: (0, 0))],
        out_shape=[jax.ShapeDtypeStruct((bsz, seq, D_MODEL), BF16), jax.ShapeDtypeStruct((bsz, c), F32)],
        scratch_shapes=[pltpu.VMEM((c // LANES, rows + (CONV_WIDTH - 1) * bsz, LANES), F32),
                        pltpu.VMEM((rows, c), F32), pltpu.VMEM((rows, c), F32),
                        pltpu.VMEM((c // LANES, rows, LANES), F32), pltpu.VMEM((rows, c), BF16),
                        pltpu.VMEM((bsz, c), F32)],
        compiler_params=_cparams(1),
        name="lru_seq",
    )(xb, gy, *consts)


def _lru_step_kernel(xb_ref, gy_ref, conv0_ref, h0_ref, wconv_ref, bconv_ref, wa_ref, ba_ref, wi_ref, bi_ref,
                     lam_ref, wob_ref, out_ref, hlast_ref, xcat, b_s, *, nb, tl):
    rows = nb * tl
    halo = (CONV_WIDTH - 1) * nb
    xcat[0:halo, :] = conv0_ref[...]
    xcat[halo:halo + rows, :] = xb_ref[...].astype(F32)
    xc = bconv_ref[...] + wconv_ref[0:1, :] * xcat[0:rows, :]
    for j in range(1, CONV_WIDTH):
        xc = xc + wconv_ref[j:j + 1, :] * xcat[j * nb:j * nb + rows, :]
    for blk in range(xc.shape[1] // GATE_TILE):
        cs = slice(blk * GATE_TILE, (blk + 1) * GATE_TILE)
        a, b = _lru_gates(xc[:, cs], cs, blk, wa_ref, ba_ref, wi_ref, bi_ref, lam_ref)
        h = h0_ref[:, cs]
        for t in range(tl):
            rs = slice(t * nb, (t + 1) * nb)
            h = a[rs, :] * h + b[rs, :]
            b_s[rs, cs] = h
        hlast_ref[:, cs] = h
    out_ref[...] = _dot((b_s[...] * gy_ref[...].astype(F32)).astype(BF16), wob_ref[...]).astype(BF16)


def _lru_step(xb, gy, conv0, h0, w_conv, b_conv, wa, ba, wi, bi, lam, w_ob, *, nb, tl):
    n_rows, c = xb.shape
    consts = [conv0, h0, w_conv, b_conv, wa, ba, wi, bi, lam, w_ob]
    full = lambda shape: pl.BlockSpec(shape, lambda i: (0,) * len(shape))
    return pl.pallas_call(
        functools.partial(_lru_step_kernel, nb=nb, tl=tl),
        grid=(1,),
        in_specs=[full(xb.shape), full(gy.shape)] + [full(a.shape) for a in consts],
        out_specs=[full((n_rows, D_MODEL)), full((nb, c))],
        out_shape=[jax.ShapeDtypeStruct((n_rows, D_MODEL), BF16), jax.ShapeDtypeStruct((nb, c), F32)],
        scratch_shapes=[pltpu.VMEM((n_rows + (CONV_WIDTH - 1) * nb, c), F32), pltpu.VMEM((n_rows, c), F32)],
        compiler_params=_cparams(1),
        name="lru_step",
    )(xb, gy, *consts)


FF_CHUNK = 1024
IN_SPLIT = 2
MIX_SPLIT = 2
N_O_SLABS = GROUP_WIDTH // LANES


def _mix_mlp_kernel(x_ref, o1, o2, o3, l1, l2, l3, bout_ref, sga_ref, sgb_ref, pe_ref,
                    woa_ref, wo_ref, gmlp_ref, wup_ref, wdown_ref, gple_ref, wpg_ref, wpe_ref, y_ref,
                    slab_ref, att_ref, *, dils, split):
    tm = x_ref.shape[0]

    @pl.when(pl.program_id(0) == 0)
    def _():
        att_ref[...] = jnp.zeros(att_ref.shape, BF16)

    att_prev = att_ref[...]

    def natural(ref, n_slabs, dil, base):
        if dil == 1:
            return lambda rs: [ref[rs, c * LANES:(c + 1) * LANES].astype(F32) for c in range(n_slabs)]

        def get(rs):
            n = (rs.stop - rs.start) // dil
            m0 = rs.start // dil
            for c in range(n_slabs):
                cols = slice(c * LANES, (c + 1) * LANES)
                if dil == 16:
                    spare = base + c - 2 * (N_O_SLABS + 1)
                    q = (rs.stop - rs.start) // 4
                    for r in range(dil):
                        dst = rs.start + (r % 4) * q + r // 4
                        slab_ref[spare, pl.ds(dst, n, stride=4), :] = ref[r, m0:m0 + n, cols].astype(F32)
                    for r4 in range(4):
                        slab_ref[base + c, pl.ds(rs.start + r4, q, stride=4), :] = (
                            slab_ref[spare, rs.start + r4 * q:rs.start + (r4 + 1) * q, :])
                else:
                    for r in range(dil):
                        slab_ref[base + c, pl.ds(rs.start + r, n, stride=dil), :] = ref[r, m0:m0 + n, cols].astype(F32)
            return [slab_ref[base + c, rs, :] for c in range(n_slabs)]

        return get

    o_get, l_get = [], []
    for g, (o_ref, l_ref) in enumerate(((o1, l1), (o2, l2), (o3, l3))):
        base = g * (N_O_SLABS + 1)
        o_get.append(natural(o_ref, N_O_SLABS, dils[g], base))
        l_get.append(natural(l_ref, 1, dils[g], base + N_O_SLABS))

    def next_att():
        n_rows = tm // split
        for part in range(split):
            rs = slice(part * n_rows, (part + 1) * n_rows)
            os_ = [get(rs) for get in o_get]
            sts = [get(rs)[0] for get in l_get]
            dens = [pltpu.roll(st, LANES - HEADS_PER_GROUP, axis=1) for st in sts]
            top = jnp.maximum(jnp.maximum(sts[0], sts[1]), sts[2])
            ws = [jnp.exp(st - top) for st in sts]
            tot = ws[0] * dens[0] + ws[1] * dens[1] + ws[2] * dens[2]
            ws = [w / tot for w in ws]
            for hd in range(HEADS_PER_GROUP):
                acc = ws[0][:, hd:hd + 1] * os_[0][hd]
                for g in (1, 2):
                    acc = acc + ws[g][:, hd:hd + 1] * os_[g][hd]
                att_ref[rs, hd * HEAD_DIM:(hd + 1) * HEAD_DIM] = acc.astype(BF16)
            yield

    def rows_body(rs):
        att = att_prev[rs, :]
        a_out = _dot(att, woa_ref[...])
        mix = (sga_ref[rs, :].astype(F32) * a_out
               + sgb_ref[rs, :].astype(F32) * bout_ref[rs, :].astype(F32))
        yield
        x = x_ref[rs, :] + _dot(mix.astype(BF16), wo_ref[...])
        h = _rms(x, gmlp_ref[...]).astype(BF16)
        yield
        acc = jnp.zeros(x.shape, F32)
        for c in range(D_FF // FF_CHUNK):
            cs = slice(c * FF_CHUNK, (c + 1) * FF_CHUNK)
            u = jnp.square(jnp.maximum(_dot(h, wup_ref[:, cs]), 0.0))
            acc = acc + _dot(u.astype(BF16), wdown_ref[cs, :])
            yield
        x = x + acc
        h = _rms(x, gple_ref[...]).astype(BF16)
        yield
        gate = jax.nn.sigmoid(_dot(h, wpg_ref[...]))
        y_ref[rs, :] = x + gate * _dot(pe_ref[rs, :].astype(BF16), wpe_ref[...])

    n_rows = tm // split
    chunks = [rows_body(slice(part * n_rows, (part + 1) * n_rows)) for part in range(split)] + [next_att()]
    while chunks:
        chunks = [c for c in chunks if next(c, True) is None]


def _mix_mlp_prompt(x, os_, ls, bout, sga, sgb, pe, weights):
    bsz, seq, _ = x.shape
    tm = ROW_TILE

    per_seq = seq // tm
    n_tiles = bsz * per_seq

    def nat(width, ahead):
        def imap(s):
            t = jnp.minimum(s, n_tiles - 1) if ahead else jnp.maximum(s - 1, 0)
            return t // per_seq, t % per_seq, 0
        return pl.BlockSpec((None, tm, width), imap)

    def grouped(width):
        def imap(s):
            t = jnp.minimum(s, n_tiles - 1)
            return t // per_seq, 0, t % per_seq, 0
        return [nat(width, True) if dil == 1 else pl.BlockSpec((None, dil, tm // dil, width), imap)
                for dil in GROUP_DILATIONS]

    ws = list(weights)
    specs = ([nat(D_MODEL, False)] + grouped(GROUP_WIDTH) + grouped(HEAD_DIM) + [nat(D_MODEL, False)] * 3
             + [nat(PLE_DIM, False)] + [_resident(w.shape) for w in ws])
    return pl.pallas_call(
        functools.partial(_mix_mlp_kernel, dils=GROUP_DILATIONS, split=MIX_SPLIT),
        grid=(n_tiles + 1,),
        in_specs=specs,
        out_specs=nat(D_MODEL, False),
        out_shape=jax.ShapeDtypeStruct((bsz, seq, D_MODEL), F32),
        scratch_shapes=[pltpu.VMEM((N_GROUPS * (N_O_SLABS + 1), tm, LANES), F32),
                        pltpu.VMEM((tm, GROUP_WIDTH), BF16)],
        compiler_params=_cparams(1),
        name="mix_mlp_prompt",
    )(x, *os_, *ls, bout, sga, sgb, pe, *ws)


def _mix_mlp_sample(x, os_, ls, bout, sga, sgb, pe, weights):
    n = x.shape[0]
    tm = SAMPLE_TILE
    n_tiles = n // tm

    def rows(width, ahead):
        if ahead:
            return pl.BlockSpec((tm, width), lambda s: (jnp.minimum(s, n_tiles - 1), 0))
        return pl.BlockSpec((tm, width), lambda s: (jnp.maximum(s - 1, 0), 0))

    ws = list(weights)
    specs = ([rows(D_MODEL, False)] + [rows(GROUP_WIDTH, True)] * 3 + [rows(HEAD_DIM, True)] * 3
             + [rows(D_MODEL, False)] * 3 + [rows(PLE_DIM, False)] + [_resident(w.shape) for w in ws])
    return pl.pallas_call(
        functools.partial(_mix_mlp_kernel, dils=(1, 1, 1), split=1),
        grid=(n_tiles + 1,),
        in_specs=specs,
        out_specs=rows(D_MODEL, False),
        out_shape=jax.ShapeDtypeStruct((n, D_MODEL), F32),
        scratch_shapes=[pltpu.VMEM((N_GROUPS * (N_O_SLABS + 1), tm, LANES), F32),
                        pltpu.VMEM((tm, GROUP_WIDTH), BF16)],
        compiler_params=_cparams(1),
        name="mix_mlp_sample",
    )(x, *os_, *ls, bout, sga, sgb, pe, *ws)


def _t5_bucket(dist):
    max_exact = REL_BUCKETS // 2
    d = jnp.maximum(dist, 1).astype(F32)
    large = max_exact + (jnp.log(d / max_exact) / math.log(REL_MAX_DIST / max_exact)
                         * (REL_BUCKETS - max_exact)).astype(jnp.int32)
    large = jnp.minimum(large, REL_BUCKETS - 1)
    return jnp.where(dist < max_exact, dist, large)


def _slot_bias(rel_bias, g):
    dil = GROUP_DILATIONS[g]
    dist = dil * jnp.arange(WINDOW_SLOTS + 1, dtype=jnp.int32)
    hs = slice(g * HEADS_PER_GROUP, (g + 1) * HEADS_PER_GROUP)
    return rel_bias[_t5_bucket(dist)][:, hs].astype(F32)


def _band_bias(slot_bias):
    n_h = slot_bias.shape[1]
    pad = jnp.full((n_h, Q_BLOCK - 1), NEG_INF, F32)
    ext = jnp.concatenate([pad, slot_bias[::-1].T, pad, jnp.full((n_h, 1), NEG_INF, F32)], axis=1)
    width = ext.shape[1]
    skew = jnp.broadcast_to(ext[:, None, :], (n_h, Q_BLOCK, width)).reshape(n_h, Q_BLOCK * width)
    skew = skew[:, :Q_BLOCK * (width - 1)].reshape(n_h, Q_BLOCK, width - 1)
    return skew[:, :, Q_BLOCK - 1:3 * Q_BLOCK - 1]


def _pair_rows_table(per_step):
    pairs = [jnp.concatenate(per_step[p * T_PAIR:(p + 1) * T_PAIR], axis=-1) for p in range(len(per_step) // T_PAIR)]
    tbl = jnp.stack(pairs)
    return jnp.broadcast_to(tbl[..., None], tbl.shape + (HEAD_DIM,))


def _dec_bias(slot_biases, n_t):
    n_h = HEADS_PER_GROUP
    neg = lambda n: jnp.full((n, n_h), NEG_INF, F32)
    bc = []
    for g in range(N_GROUPS):
        near_first = slot_biases[g][::-1]
        if GROUP_DILATIONS[g] == 1:
            per_step = [jnp.concatenate([neg(t), near_first[:WINDOW_SLOTS - t]], axis=0) for t in range(n_t)]
        else:
            per_step = [near_first[:WINDOW_SLOTS]] * n_t
        bc.append(_pair_rows_table(per_step))
    sb = slot_biases[0]
    bn1 = _pair_rows_table([jnp.concatenate([sb[:t + 1][::-1], neg(n_t - 1 - t)], axis=0) for t in range(n_t)])
    own = jnp.stack([jnp.concatenate([s[:1]] * T_PAIR, axis=-1)[0] for s in slot_biases])
    bn = jnp.broadcast_to(own[..., None], own.shape + (HEAD_DIM,))
    return jnp.stack(bc), bn1, bn


def _gate_tiles(w):
    per = GATE_TILE // LRU_BLOCK
    w = w.reshape(LRU_WIDTH // GATE_TILE, per, LRU_BLOCK, LRU_BLOCK)
    eye = jnp.eye(per, dtype=w.dtype)
    return jnp.einsum('npij,pq->npiqj', w, eye).reshape(-1, GATE_TILE, GATE_TILE).astype(BF16)


def kernel(x_prompt, x_sample, p_prompt, p_sample, cache_kv1, cache_kv2, cache_kv3, state_conv, state_lru,
           rel_bias, g_mix, w_in, g_q, g_k, w_oa, w_conv, b_conv, w_rg_a, b_rg_a, w_rg_i, b_rg_i, lam,
           w_ob, w_o, g_mlp, w_up, w_down, g_ple, w_ple_gate, w_ple_in):
    depth = w_in.shape[0]
    assert depth == 1
    bsz, seq, _ = x_prompt.shape
    bd, n_t, _ = x_sample.shape
    gw = GROUP_WIDTH
    row = lambda a: a.reshape(1, -1)

    slot_biases = [_slot_bias(rel_bias, g) for g in range(N_GROUPS)]
    band = [_band_bias(sb) for sb in slot_biases]
    bias_cache, bias_new1, bias_new = _dec_bias(slot_biases, n_t)

    i = 0
    w_in_b = w_in[i].astype(BF16)
    gm, gq, gk = row(g_mix[i]), row(g_q[i]), row(g_k[i])
    lru_w = (w_conv[i], row(b_conv[i]), _gate_tiles(0.5 * w_rg_a[i]), row(b_rg_a[i]), _gate_tiles(0.5 * w_rg_i[i]),
             row(b_rg_i[i]), row(lam[i]), w_ob[i].astype(BF16))
    mlp_w = (w_oa[i].astype(BF16), w_o[i].astype(BF16), row(g_mlp[i]), w_up[i].astype(BF16),
             w_down[i].astype(BF16), row(g_ple[i]), w_ple_gate[i].astype(BF16), w_ple_in[i].astype(BF16))

    (q1, q2, q3, k1, k2, k3, v1, v2, v3, xb, gy, sga, sgb, kvt1, kvt2, kvt3, xbt) = _in_proj_prompt(
        x_prompt, gm, w_in_b, gq, gk)
    os_, ls = [], []
    for g, (qq, kk, vv) in enumerate(((q1, k1, v1), (q2, k2, v2), (q3, k3, v3))):
        dil = GROUP_DILATIONS[g]
        ns, ln = bsz * dil, seq // dil
        o, l = _swa(qq.reshape(ns, ln, gw), kk.reshape(ns, ln, gw), vv.reshape(ns, ln, gw), band[g])
        os_.append(o if dil == 1 else o.reshape(bsz, dil, ln, gw))
        ls.append(l if dil == 1 else l.reshape(bsz, dil, ln, HEAD_DIM))
    bout, lru_p = _lru_seq(xb, gy, *lru_w)
    y_prompt = _mix_mlp_prompt(x_prompt, os_, ls, bout, sga, sgb, p_prompt[i], mlp_w)
    kv_p = [t.reshape(bsz, GROUP_WINDOWS[g], 2, HEADS_PER_GROUP, HEAD_DIM)[None]
            for g, t in enumerate((kvt1, kvt2, kvt3))]
    conv_p = xbt[:, SUBLANES - (CONV_WIDTH - 1):][None]
    lru_p = lru_p[None]

    n_s = bd * n_t
    (sq1, sq2, sq3, _, _, _, _, _, _, sxb, sgy, ssga, ssgb, skv1, skv2, skv3, sxbt) = _in_proj_sample(
        x_sample.reshape(n_s, D_MODEL), gm, w_in_b, gq, gk)
    n_p = n_t // T_PAIR
    qv = jnp.stack([q.astype(F32).reshape(bd, n_p, KV_ROWS, HEAD_DIM) for q in (sq1, sq2, sq3)], axis=1)
    kv_new = jnp.stack([t.reshape(bd, n_t, 2, HEADS_PER_GROUP, HEAD_DIM) for t in (skv1, skv2, skv3)], axis=1)
    kv_new = jnp.moveaxis(kv_new, 3, 1)
    newp = kv_new.reshape(bd, 2, N_GROUPS, n_p, KV_ROWS, HEAD_DIM)
    newd = jnp.concatenate([kv_new[:, :, 0]] * T_PAIR, axis=-2)
    caches = [c.reshape((bd, WINDOW_SLOTS) + ((dil,) if dil > 1 else ()) + (2, HEADS_PER_GROUP, HEAD_DIM))
              for c, dil in zip((cache_kv1, cache_kv2, cache_kv3), GROUP_DILATIONS)]
    att_s = _dec_attn(qv, caches, newp, newd, bias_cache, bias_new1, bias_new).reshape(n_s, gw)

    def to_tb(a):
        return a.reshape(bd, n_t, -1).transpose(1, 0, 2).reshape(n_t * bd, -1)

    def from_tb(a):
        return a.reshape(n_t, bd, -1).transpose(1, 0, 2).reshape(bd * n_t, -1)

    conv0_s = state_conv[i].transpose(1, 0, 2).reshape((CONV_WIDTH - 1) * bd, LRU_WIDTH)
    bout_s, lru_s = _lru_step(to_tb(sxb), to_tb(sgy), conv0_s, state_lru[i].astype(F32), *lru_w, nb=bd, tl=n_t)
    zero_o = jnp.zeros((n_s, gw), BF16)
    lane = jnp.arange(HEAD_DIM)[None, :]
    den_one = jnp.where(lane < HEADS_PER_GROUP, 0.0, 1.0)
    st_on = jnp.broadcast_to(den_one, (n_s, HEAD_DIM)).astype(F32)
    st_off = jnp.broadcast_to(jnp.where(lane < HEADS_PER_GROUP, NEG_INF, den_one), (n_s, HEAD_DIM)).astype(F32)
    y_sample = _mix_mlp_sample(x_sample.reshape(n_s, D_MODEL), [att_s.astype(BF16), zero_o, zero_o],
                               [st_on, st_off, st_off], from_tb(bout_s), ssga, ssgb,
                               p_sample[i].reshape(n_s, PLE_DIM), mlp_w).reshape(bd, n_t, D_MODEL)
    kv_s = [t.reshape(bd, n_t, 2, HEADS_PER_GROUP, HEAD_DIM)[None] for t in (skv1, skv2, skv3)]
    xcat_s = jnp.concatenate([state_conv[i], sxbt.reshape(bd, n_t, LRU_WIDTH)], axis=1)
    conv_s = xcat_s[:, n_t:][None]

    return (y_prompt, y_sample, kv_p[0], kv_p[1], kv_p[2], conv_p, lru_p,
            kv_s[0], kv_s[1], kv_s[2], conv_s, lru_s[None])
```

```python
import functools
import math

import jax
import jax.numpy as jnp
from jax import lax
from jax.experimental import pallas as pl
from jax.experimental.pallas import tpu as pltpu

F32 = jnp.float32
BF16 = jnp.bfloat16

D_MODEL = 1024
HEAD_DIM = 128
HEADS_PER_GROUP = 4
GROUP_WINDOWS = (128, 512, 2048)
GROUP_DILATIONS = (1, 4, 16)
N_GROUPS = 3
GROUP_WIDTH = HEADS_PER_GROUP * HEAD_DIM
ATT_WIDTH = N_GROUPS * GROUP_WIDTH
ATT_SCALE = HEAD_DIM ** -0.5
LRU_WIDTH = D_MODEL
LRU_BLOCKS = 16
LRU_BLOCK = LRU_WIDTH // LRU_BLOCKS
CONV_WIDTH = 4
RG_C = 8.0
D_FF = 4 * D_MODEL
PLE_DIM = 256
REL_BUCKETS = 32
REL_MAX_DIST = 2048
NORM_EPS = 1e-6
NEG_INF = -1e30
WINDOW_SLOTS = 128
KV_ROWS = 2 * HEADS_PER_GROUP

OFF_Q, OFF_K, OFF_V = 0, ATT_WIDTH, 2 * ATT_WIDTH
OFF_XB = 3 * ATT_WIDTH
OFF_YB = OFF_XB + LRU_WIDTH
OFF_GA = OFF_YB + LRU_WIDTH
OFF_GB = OFF_GA + D_MODEL

SUBLANES = 8
LANES = 128
ROW_TILE = 512
SAMPLE_TILE = 256
Q_BLOCK = 128
SWA_ROWS = 2048
LRU_STEPS = 128
GATE_TILE = 256
VMEM_LIMIT = 58 * 1024 * 1024


def _cparams(n_axes):
    return pltpu.CompilerParams(dimension_semantics=("arbitrary",) * n_axes,
                                vmem_limit_bytes=VMEM_LIMIT)


def _resident(shape):
    nd = len(shape)
    return pl.BlockSpec(shape, lambda *_: (0,) * nd, pipeline_mode=pl.Buffered(1))


def _rms(x, gain):
    return x * lax.rsqrt(jnp.mean(x * x, axis=-1, keepdims=True) + NORM_EPS) * gain


def _dot(a, b):
    return jnp.dot(a, b, preferred_element_type=F32)


def _dot_nt(a, b):
    return lax.dot_general(a, b, (((1,), (1,)), ((), ())), preferred_element_type=F32)


def _in_proj_kernel(x_ref, gmix_ref, w_ref, gq_ref, gk_ref,
                    q1, q2, q3, k1, k2, k3, v1, v2, v3, xb_ref, gy_ref, sga_ref, sgb_ref,
                    kv1, kv2, kv3, xbt_ref, slab_ref, *, dils, split):
    q_refs, k_refs, v_refs, kv_refs = (q1, q2, q3), (k1, k2, k3), (v1, v2, v3), (kv1, kv2, kv3)
    tm = x_ref.shape[0]
    tr = tm // split
    gq = gq_ref[...] * ATT_SCALE
    gk = gk_ref[...]

    def chunk(c):
        r0 = c * tr
        h = _rms(x_ref[r0:r0 + tr, :], gmix_ref[...]).astype(BF16)

        def put(ref, hd, val, dil, slab):
            cs = slice(hd * HEAD_DIM, (hd + 1) * HEAD_DIM)
            if dil == 1:
                ref[r0:r0 + tr, cs] = val.astype(BF16)
                return
            slab_ref[slab, r0:r0 + tr, :] = val
            n = tr // dil
            if dil == 16:
                spare = slab - 3 * HEADS_PER_GROUP
                q = tr // 4
                for r4 in range(4):
                    slab_ref[spare, r0 + r4 * q:r0 + (r4 + 1) * q, :] = slab_ref[slab, pl.ds(r0 + r4, q, stride=4), :]
                for r in range(dil):
                    src = r0 + (r % 4) * q + r // 4
                    ref[r, c * n:(c + 1) * n, cs] = slab_ref[spare, pl.ds(src, n, stride=4), :].astype(BF16)
                return
            for r in range(dil):
                ref[r, c * n:(c + 1) * n, cs] = slab_ref[slab, pl.ds(r0 + r, n, stride=dil), :].astype(BF16)

        def put_tail(ref, row, val):
            n_tail = ref.shape[0] // KV_ROWS
            lo = max(r0, tm - n_tail)
            cnt = r0 + tr - lo
            if cnt > 0:
                ref[pl.ds((lo - (tm - n_tail)) * KV_ROWS + row, cnt, stride=KV_ROWS), :] = val[lo - r0:, :]

        def group(g):
            c0 = g * GROUP_WIDTH
            s0 = max(g - 1, 0) * 3 * HEADS_PER_GROUP
            yq = _dot(h, w_ref[:, OFF_Q + c0:OFF_Q + c0 + GROUP_WIDTH])
            yk = _dot(h, w_ref[:, OFF_K + c0:OFF_K + c0 + GROUP_WIDTH])
            yv = _dot(h, w_ref[:, OFF_V + c0:OFF_V + c0 + GROUP_WIDTH])
            for hd in range(HEADS_PER_GROUP):
                cs = slice(hd * HEAD_DIM, (hd + 1) * HEAD_DIM)
                put(q_refs[g], hd, _rms(yq[:, cs], gq), dils[g], s0 + hd)
                kn = _rms(yk[:, cs], gk)
                put(k_refs[g], hd, kn, dils[g], s0 + HEADS_PER_GROUP + hd)
                put(v_refs[g], hd, yv[:, cs], dils[g], s0 + 2 * HEADS_PER_GROUP + hd)
                put_tail(kv_refs[g], hd, kn)
                put_tail(kv_refs[g], HEADS_PER_GROUP + hd, yv[:, cs])

        def wide(off):
            return _dot(h, w_ref[:, off:off + D_MODEL])

        rs = slice(r0, r0 + tr)
        group(2)
        y = wide(OFF_XB)
        xb_ref[rs, :] = y.astype(BF16)
        n_x = xbt_ref.shape[0]
        if r0 + tr == tm:
            xbt_ref[...] = y[tr - n_x:, :]
        gy_ref[rs, :] = jax.nn.gelu(wide(OFF_YB)).astype(BF16)
        group(1)
        sga_ref[rs, :] = jax.nn.sigmoid(wide(OFF_GA)).astype(BF16)
        group(0)
        sgb_ref[rs, :] = jax.nn.sigmoid(wide(OFF_GB)).astype(BF16)

    for c in range(split):
        chunk(c)


def _in_proj_prompt(x, g_mix, w_in, g_q, g_k):
    bsz, seq, _ = x.shape
    tm = ROW_TILE
    n_tiles = seq // tm
    gw = GROUP_WIDTH

    def nat(width):
        return pl.BlockSpec((None, tm, width), lambda b, i: (b, i, 0))

    def sds(shape, dt):
        return jax.ShapeDtypeStruct(shape, dt)

    qkv_shapes, qkv_specs = [], []
    for dil in GROUP_DILATIONS:
        if dil == 1:
            qkv_shapes.append(sds((bsz, seq, gw), BF16))
            qkv_specs.append(nat(gw))
        else:
            qkv_shapes.append(sds((bsz, dil, seq // dil, gw), BF16))
            qkv_specs.append(pl.BlockSpec((None, dil, tm // dil, gw), lambda b, i: (b, 0, i, 0)))
    tail_rows = [min(w, tm) for w in GROUP_WINDOWS]
    tail_start = [n_tiles - max(w // tm, 1) for w in GROUP_WINDOWS]
    tail_specs = [pl.BlockSpec((None, r * KV_ROWS, HEAD_DIM), lambda b, i, s=s: (b, jnp.maximum(i - s, 0), 0))
                  for r, s in zip(tail_rows, tail_start)]
    out_shape = (qkv_shapes * 3 + [sds((bsz, seq, LRU_WIDTH), BF16)] * 2 + [sds((bsz, seq, D_MODEL), BF16)] * 2
                 + [sds((bsz, w * KV_ROWS, HEAD_DIM), F32) for w in GROUP_WINDOWS]
                 + [sds((bsz, SUBLANES, LRU_WIDTH), F32)])
    out_specs = (qkv_specs * 3 + [nat(LRU_WIDTH)] * 2 + [nat(D_MODEL)] * 2 + tail_specs
                 + [pl.BlockSpec((None, SUBLANES, LRU_WIDTH), lambda b, i: (b, 0, 0))])
    return pl.pallas_call(
        functools.partial(_in_proj_kernel, dils=GROUP_DILATIONS, split=IN_SPLIT),
        grid=(bsz, n_tiles),
        in_specs=[nat(D_MODEL), _resident(g_mix.shape), _resident(w_in.shape),
                  _resident(g_q.shape), _resident(g_k.shape)],
        out_specs=out_specs,
        out_shape=out_shape,
        scratch_shapes=[pltpu.VMEM(((N_GROUPS - 1) * 3 * HEADS_PER_GROUP, tm, HEAD_DIM), F32)],
        compiler_params=_cparams(2),
        name="in_proj_prompt",
    )(x, g_mix, w_in, g_q, g_k)


def _in_proj_sample(x, g_mix, w_in, g_q, g_k):
    n = x.shape[0]
    tm = SAMPLE_TILE
    gw = GROUP_WIDTH

    def rows(width):
        return pl.BlockSpec((tm, width), lambda i: (i, 0))

    def sds(width, dt):
        return jax.ShapeDtypeStruct((n, width), dt)

    out_shape = ([sds(gw, BF16)] * 9 + [sds(LRU_WIDTH, BF16)] * 2 + [sds(D_MODEL, BF16)] * 2
                 + [jax.ShapeDtypeStruct((n * KV_ROWS, HEAD_DIM), F32)] * 3 + [sds(LRU_WIDTH, F32)])
    out_specs = ([rows(gw)] * 9 + [rows(LRU_WIDTH)] * 2 + [rows(D_MODEL)] * 2
                 + [pl.BlockSpec((tm * KV_ROWS, HEAD_DIM), lambda i: (i, 0))] * 3 + [rows(LRU_WIDTH)])
    return pl.pallas_call(
        functools.partial(_in_proj_kernel, dils=(1, 1, 1), split=1),
        grid=(n // tm,),
        in_specs=[rows(D_MODEL), _resident(g_mix.shape), _resident(w_in.shape),
                  _resident(g_q.shape), _resident(g_k.shape)],
        out_specs=out_specs,
        out_shape=out_shape,
        scratch_shapes=[pltpu.VMEM(((N_GROUPS - 1) * 3 * HEADS_PER_GROUP, tm, HEAD_DIM), F32)],
        compiler_params=_cparams(1),
        name="in_proj_sample",
    )(x, g_mix, w_in, g_q, g_k)


def _swa_kernel(q_ref, k_ref, v_ref, kp_ref, vp_ref, bias_ref, o_ref, st_ref):
    first = pl.program_id(1) == 0
    n_seq = q_ref.shape[0]
    n_blk = q_ref.shape[1] // Q_BLOCK
    col = lax.broadcasted_iota(jnp.int32, (Q_BLOCK, 2 * Q_BLOCK), 1)
    lane = lax.broadcasted_iota(jnp.int32, (Q_BLOCK, HEAD_DIM), 1)
    ones = jnp.ones((2 * Q_BLOCK, HEAD_DIM), BF16)
    for sq in range(n_seq):
        for j in range(n_blk):
            rq = slice(j * Q_BLOCK, (j + 1) * Q_BLOCK)
            st_tile = jnp.zeros((Q_BLOCK, HEAD_DIM), F32)
            for hd in range(HEADS_PER_GROUP):
                cs = slice(hd * HEAD_DIM, (hd + 1) * HEAD_DIM)
                q = q_ref[sq, rq, cs]
                if j == 0:
                    kk = jnp.concatenate([kp_ref[sq, :, cs], k_ref[sq, rq, cs]], axis=0)
                    vv = jnp.concatenate([vp_ref[sq, :, cs], v_ref[sq, rq, cs]], axis=0)
                else:
                    rk = slice((j - 1) * Q_BLOCK, (j + 1) * Q_BLOCK)
                    kk = k_ref[sq, rk, cs]
                    vv = v_ref[sq, rk, cs]
                s = _dot_nt(q, kk) + bias_ref[hd]
                if j == 0:
                    s = jnp.where(col < jnp.where(first, Q_BLOCK, 0), NEG_INF, s)
                m = jnp.max(s, axis=-1, keepdims=True)
                p = jnp.exp(s - m).astype(BF16)
                oe = _dot(p, jnp.concatenate([vv, ones], axis=1))
                o_ref[sq, rq, cs] = oe[:, :HEAD_DIM].astype(BF16)
                st_tile = jnp.where(lane == hd, m, st_tile)
                st_tile = jnp.where(lane == HEADS_PER_GROUP + hd, oe[:, HEAD_DIM:], st_tile)
            st_ref[sq, rq, :] = st_tile


def _swa(q, k, v, band_bias):
    n_seq, length, gw = q.shape
    tq = min(length, SWA_ROWS)
    sb = SWA_ROWS // tq
    ratio = tq // Q_BLOCK
    cur = pl.BlockSpec((sb, tq, gw), lambda s, i: (s, i, 0))
    prev = pl.BlockSpec((sb, Q_BLOCK, gw), lambda s, i: (s, jnp.maximum(i * ratio - 1, 0), 0))
    return pl.pallas_call(
        _swa_kernel,
        grid=(n_seq // sb, length // tq),
        in_specs=[cur, cur, cur, prev, prev, _resident(band_bias.shape)],
        out_specs=[cur, pl.BlockSpec((sb, tq, HEAD_DIM), lambda s, i: (s, i, 0))],
        out_shape=[jax.ShapeDtypeStruct((n_seq, length, gw), BF16),
                   jax.ShapeDtypeStruct((n_seq, length, HEAD_DIM), F32)],
        compiler_params=_cparams(2),
        name="swa",
    )(q, k, v, k, v, band_bias)


DEC_BATCH_TILE = 4
T_PAIR = 2


def _dec_attn_kernel(q_ref, k1, v1, k2, v2, k3, v3, newp_ref, newd_ref, bc_ref, bn1_ref, bn_ref, att_ref):
    k_refs, v_refs = (k1, k2, k3), (v1, v2, v3)
    n_b, _, n_p = q_ref.shape[:3]
    for bi in range(n_b):
        near_k = jnp.concatenate([k1[bi]] * T_PAIR, axis=1)
        near_v = jnp.concatenate([v1[bi]] * T_PAIR, axis=1)
        for tp in range(n_p):
            accs, tops, dens = [], [], []
            for g in range(N_GROUPS):
                qv = q_ref[bi, g, tp]
                if GROUP_DILATIONS[g] == 1:
                    kt, vt = near_k, near_v
                    kn, vn = newd_ref[bi, 0], newd_ref[bi, 1]
                    bn = bn1_ref[tp]
                else:
                    steps = slice(tp * T_PAIR, (tp + 1) * T_PAIR)
                    kt = k_refs[g][bi, :, steps, :, :].reshape(WINDOW_SLOTS, KV_ROWS, HEAD_DIM)
                    vt = v_refs[g][bi, :, steps, :, :].reshape(WINDOW_SLOTS, KV_ROWS, HEAD_DIM)
                    kn, vn = newp_ref[bi, 0, g, tp][None], newp_ref[bi, 1, g, tp][None]
                    bn = bn_ref[g][None]
                lc = jnp.sum(kt * qv[None], axis=-1, keepdims=True) + bc_ref[g, tp]
                ln = jnp.sum(kn * qv[None], axis=-1, keepdims=True) + bn
                m = jnp.maximum(jnp.max(lc, axis=0), jnp.max(ln, axis=0))
                pc = jnp.exp(lc - m[None])
                pn = jnp.exp(ln - m[None])
                dens.append(jnp.sum(pc, axis=0) + jnp.sum(pn, axis=0))
                accs.append(jnp.sum(pc * vt, axis=0) + jnp.sum(pn * vn, axis=0))
                tops.append(m)
            top = jnp.maximum(jnp.maximum(tops[0], tops[1]), tops[2])
            ws = [jnp.exp(m - top) for m in tops]
            tot = ws[0] * dens[0] + ws[1] * dens[1] + ws[2] * dens[2]
            out = (ws[0] / tot) * accs[0]
            for g in (1, 2):
                out = out + (ws[g] / tot) * accs[g]
            att_ref[bi, tp] = out


def _dec_attn(qv, caches, newp, newd, bias_cache, bias_new1, bias_new):
    bd, _, n_p = qv.shape[:3]
    bb = DEC_BATCH_TILE
    hd = (HEADS_PER_GROUP, HEAD_DIM)
    cspecs, cargs = [], []
    for g, dil in enumerate(GROUP_DILATIONS):
        for kv in range(2):
            if dil == 1:
                cspecs.append(pl.BlockSpec((bb, WINDOW_SLOTS, None) + hd, lambda i, kv=kv: (i, 0, kv, 0, 0)))
            else:
                cspecs.append(pl.BlockSpec((bb, WINDOW_SLOTS, n_p * T_PAIR, None) + hd,
                                           lambda i, kv=kv: (i, 0, 0, kv, 0, 0)))
            cargs.append(caches[g])

    def whole(a):
        return pl.BlockSpec((bb,) + a.shape[1:], lambda i: (i,) + (0,) * (a.ndim - 1))

    return pl.pallas_call(
        _dec_attn_kernel,
        grid=(bd // bb,),
        in_specs=[whole(qv)] + cspecs + [whole(newp), whole(newd), _resident(bias_cache.shape),
                                         _resident(bias_new1.shape), _resident(bias_new.shape)],
        out_specs=pl.BlockSpec((bb, n_p, KV_ROWS, HEAD_DIM), lambda i: (i, 0, 0, 0)),
        out_shape=jax.ShapeDtypeStruct((bd, n_p, KV_ROWS, HEAD_DIM), F32),
        compiler_params=_cparams(1),
        name="dec_attn",
    )(qv, *cargs, newp, newd, bias_cache, bias_new1, bias_new)


def _lru_gates(xc, cs, blk, wa_ref, ba_ref, wi_ref, bi_ref, lam_ref):
    xcb = xc.astype(BF16)
    tr = jnp.tanh(_dot(xcb, wa_ref[blk]) + 0.5 * ba_ref[:, cs])
    ti = jnp.tanh(_dot(xcb, wi_ref[blk]) + 0.5 * bi_ref[:, cs])
    lam = lam_ref[:, cs]
    log_sig = jnp.minimum(lam, 0.0) - jnp.log1p(jnp.exp(-jnp.abs(lam)))
    half_c = (0.5 * RG_C) * log_sig
    log_a = half_c * tr + half_c
    a = jnp.exp(log_a)
    hx = 0.5 * xc
    y = jnp.tanh(log_a) * (-1.0 - a * a)
    root = jnp.where(y > 0.0, y * lax.rsqrt(y), 0.0)
    b = root * (hx * ti + hx)
    return a, b


def _lru_seq_kernel(xb_ref, gy_ref, wconv_ref, bconv_ref, wa_ref, ba_ref, wi_ref, bi_ref, lam_ref, wob_ref,
                    out_ref, hlast_ref, xcat, h_sl, hg, hc, *, n_tiles):
    nb, tl, c = xb_ref.shape
    rows = nb * tl
    halo = (CONV_WIDTH - 1) * nb
    n_sl = c // LANES
    per = GATE_TILE // LANES
    step = pl.program_id(0)

    @pl.when(step == 0)
    def _():
        xcat[:, 0:halo, :] = jnp.zeros((n_sl, halo, LANES), F32)
        hc[...] = jnp.zeros((nb, c), F32)
        hg[...] = jnp.zeros(hg.shape, BF16)

    for b in range(nb):
        xf = xb_ref[b].astype(F32)
        for s in range(n_sl):
            xcat[s, pl.ds(halo + b, tl, stride=nb), :] = xf[:, s * LANES:(s + 1) * LANES]
    h_end = []
    for blk in range(c // GATE_TILE):
        cs = slice(blk * GATE_TILE, (blk + 1) * GATE_TILE)
        out_ref[:, :, cs] = _dot(hg[...], wob_ref[:, cs]).astype(BF16).reshape(nb, tl, GATE_TILE)
        parts = []
        for s in range(blk * per, (blk + 1) * per):
            ls = slice(s * LANES, (s + 1) * LANES)
            xc = bconv_ref[:, ls] + wconv_ref[0:1, ls] * xcat[s, 0:rows, :]
            for j in range(1, CONV_WIDTH):
                xc = xc + wconv_ref[j:j + 1, ls] * xcat[s, j * nb:j * nb + rows, :]
            parts.append(xc)
        a, b = _lru_gates(jnp.concatenate(parts, axis=1), cs, blk, wa_ref, ba_ref, wi_ref, bi_ref, lam_ref)
        h = hc[:, cs]
        for t in range(tl):
            rs = slice(t * nb, (t + 1) * nb)
            h = a[rs, :] * h + b[rs, :]
            for k in range(per):
                h_sl[blk * per + k, rs, :] = h[:, k * LANES:(k + 1) * LANES]
        h_end.append(h)
    for s in range(n_sl):
        xcat[s, 0:halo, :] = xcat[s, rows:rows + halo, :]
    h = jnp.concatenate(h_end, axis=1)
    hc[...] = h

    @pl.when(step == n_tiles - 1)
    def _():
        hlast_ref[...] = h

    for b in range(nb):
        for s in range(n_sl):
            ls = slice(s * LANES, (s + 1) * LANES)
            hg[b * tl:(b + 1) * tl, ls] = h_sl[s, pl.ds(b, tl, stride=nb), :].astype(BF16) * gy_ref[b, :, ls]


def _lru_seq(xb, gy, w_conv, b_conv, wa, ba, wi, bi, lam, w_ob):
    bsz, seq, c = xb.shape
    tl = LRU_STEPS
    rows = bsz * tl
    n_tiles = seq // tl
    ahead = pl.BlockSpec((bsz, tl, c), lambda s: (0, jnp.minimum(s, n_tiles - 1), 0))
    consts = [w_conv, b_conv, wa, ba, wi, bi, lam, w_ob]
    return pl.pallas_call(
        functools.partial(_lru_seq_kernel, n_tiles=n_tiles),
        grid=(n_tiles + 1,),
        in_specs=[ahead, ahead] + [_resident(a.shape) for a in consts],
        out_specs=[pl.BlockSpec((bsz, tl, D_MODEL), lambda s: (0, jnp.maximum(s - 1, 0), 0)),
                   pl.BlockSpec((bsz, c), lambda s: (0, 0))],
        out_shape=[jax.ShapeDtypeStruct((bsz, seq, D_MODEL), BF16), jax.ShapeDtypeStruct((bsz, c), F32)],
        scratch_shapes=[pltpu.VMEM((c // LANES, rows + (CONV_WIDTH - 1) * bsz, LANES), F32),
                        pltpu.VMEM((c // LANES, rows, LANES), F32), pltpu.VMEM((rows, c), BF16),
                        pltpu.VMEM((bsz, c), F32)],
        compiler_params=_cparams(1),
        name="lru_seq",
    )(xb, gy, *consts)


def _lru_step_kernel(xb_ref, gy_ref, conv0_ref, h0_ref, wconv_ref, bconv_ref, wa_ref, ba_ref, wi_ref, bi_ref,
                     lam_ref, wob_ref, out_ref, hlast_ref, xcat, b_s, *, nb, tl):
    rows = nb * tl
    halo = (CONV_WIDTH - 1) * nb
    xcat[0:halo, :] = conv0_ref[...]
    xcat[halo:halo + rows, :] = xb_ref[...].astype(F32)
    xc = bconv_ref[...] + wconv_ref[0:1, :] * xcat[0:rows, :]
    for j in range(1, CONV_WIDTH):
        xc = xc + wconv_ref[j:j + 1, :] * xcat[j * nb:j * nb + rows, :]
    for blk in range(xc.shape[1] // GATE_TILE):
        cs = slice(blk * GATE_TILE, (blk + 1) * GATE_TILE)
        a, b = _lru_gates(xc[:, cs], cs, blk, wa_ref, ba_ref, wi_ref, bi_ref, lam_ref)
        h = h0_ref[:, cs]
        for t in range(tl):
            rs = slice(t * nb, (t + 1) * nb)
            h = a[rs, :] * h + b[rs, :]
            b_s[rs, cs] = h
        hlast_ref[:, cs] = h
    out_ref[...] = _dot((b_s[...] * gy_ref[...].astype(F32)).astype(BF16), wob_ref[...]).astype(BF16)


def _lru_step(xb, gy, conv0, h0, w_conv, b_conv, wa, ba, wi, bi, lam, w_ob, *, nb, tl):
    n_rows, c = xb.shape
    consts = [conv0, h0, w_conv, b_conv, wa, ba, wi, bi, lam, w_ob]
    full = lambda shape: pl.BlockSpec(shape, lambda i: (0,) * len(shape))
    return pl.pallas_call(
        functools.partial(_lru_step_kernel, nb=nb, tl=tl),
        grid=(1,),
        in_specs=[full(xb.shape), full(gy.shape)] + [full(a.shape) for a in consts],
        out_specs=[full((n_rows, D_MODEL)), full((nb, c))],
        out_shape=[jax.ShapeDtypeStruct((n_rows, D_MODEL), BF16), jax.ShapeDtypeStruct((nb, c), F32)],
        scratch_shapes=[pltpu.VMEM((n_rows + (CONV_WIDTH - 1) * nb, c), F32), pltpu.VMEM((n_rows, c), F32)],
        compiler_params=_cparams(1),
        name="lru_step",
    )(xb, gy, *consts)


FF_CHUNK = 1024
IN_SPLIT = 2
MIX_SPLIT = 2
N_O_SLABS = GROUP_WIDTH // LANES


def _mix_mlp_kernel(x_ref, o1, o2, o3, l1, l2, l3, bout_ref, sga_ref, sgb_ref, pe_ref,
                    woa_ref, wo_ref, gmlp_ref, wup_ref, wdown_ref, gple_ref, wpg_ref, wpe_ref, y_ref,
                    slab_ref, att_ref, *, dils, split):
    tm = x_ref.shape[0]

    @pl.when(pl.program_id(0) == 0)
    def _():
        att_ref[...] = jnp.zeros(att_ref.shape, BF16)

    att_prev = att_ref[...]

    def natural(ref, n_slabs, dil, base):
        if dil == 1:
            return lambda rs: [ref[rs, c * LANES:(c + 1) * LANES].astype(F32) for c in range(n_slabs)]

        def get(rs):
            n = (rs.stop - rs.start) // dil
            m0 = rs.start // dil
            for c in range(n_slabs):
                cols = slice(c * LANES, (c + 1) * LANES)
                if dil == 16:
                    spare = base + c - 2 * (N_O_SLABS + 1)
                    q = (rs.stop - rs.start) // 4
                    for r in range(dil):
                        dst = rs.start + (r % 4) * q + r // 4
                        slab_ref[spare, pl.ds(dst, n, stride=4), :] = ref[r, m0:m0 + n, cols].astype(F32)
                    for r4 in range(4):
                        slab_ref[base + c, pl.ds(rs.start + r4, q, stride=4), :] = (
                            slab_ref[spare, rs.start + r4 * q:rs.start + (r4 + 1) * q, :])
                else:
                    for r in range(dil):
                        slab_ref[base + c, pl.ds(rs.start + r, n, stride=dil), :] = ref[r, m0:m0 + n, cols].astype(F32)
            return [slab_ref[base + c, rs, :] for c in range(n_slabs)]

        return get

    o_get, l_get = [], []
    for g, (o_ref, l_ref) in enumerate(((o1, l1), (o2, l2), (o3, l3))):
        base = g * (N_O_SLABS + 1)
        o_get.append(natural(o_ref, N_O_SLABS, dils[g], base))
        l_get.append(natural(l_ref, 1, dils[g], base + N_O_SLABS))

    def next_att():
        n_rows = tm // split
        for part in range(split):
            rs = slice(part * n_rows, (part + 1) * n_rows)
            os_ = [get(rs) for get in o_get]
            sts = [get(rs)[0] for get in l_get]
            dens = [pltpu.roll(st, LANES - HEADS_PER_GROUP, axis=1) for st in sts]
            top = jnp.maximum(jnp.maximum(sts[0], sts[1]), sts[2])
            ws = [jnp.exp(st - top) for st in sts]
            tot = ws[0] * dens[0] + ws[1] * dens[1] + ws[2] * dens[2]
            ws = [w / tot for w in ws]
            for hd in range(HEADS_PER_GROUP):
                acc = ws[0][:, hd:hd + 1] * os_[0][hd]
                for g in (1, 2):
                    acc = acc + ws[g][:, hd:hd + 1] * os_[g][hd]
                att_ref[rs, hd * HEAD_DIM:(hd + 1) * HEAD_DIM] = acc.astype(BF16)
            yield

    def rows_body(rs):
        att = att_prev[rs, :]
        a_out = _dot(att, woa_ref[...])
        mix = (sga_ref[rs, :].astype(F32) * a_out
               + sgb_ref[rs, :].astype(F32) * bout_ref[rs, :].astype(F32))
        yield
        x = x_ref[rs, :] + _dot(mix.astype(BF16), wo_ref[...])
        h = _rms(x, gmlp_ref[...]).astype(BF16)
        yield
        acc = jnp.zeros(x.shape, F32)
        for c in range(D_FF // FF_CHUNK):
            cs = slice(c * FF_CHUNK, (c + 1) * FF_CHUNK)
            u = jnp.square(jnp.maximum(_dot(h, wup_ref[:, cs]), 0.0))
            acc = acc + _dot(u.astype(BF16), wdown_ref[cs, :])
            yield
        x = x + acc
        h = _rms(x, gple_ref[...]).astype(BF16)
        yield
        gate = jax.nn.sigmoid(_dot(h, wpg_ref[...]))
        y_ref[rs, :] = x + gate * _dot(pe_ref[rs, :].astype(BF16), wpe_ref[...])

    n_rows = tm // split
    chunks = [rows_body(slice(part * n_rows, (part + 1) * n_rows)) for part in range(split)] + [next_att()]
    while chunks:
        chunks = [c for c in chunks if next(c, True) is None]


def _mix_mlp_prompt(x, os_, ls, bout, sga, sgb, pe, weights):
    bsz, seq, _ = x.shape
    tm = ROW_TILE

    per_seq = seq // tm
    n_tiles = bsz * per_seq

    def nat(width, ahead):
        def imap(s):
            t = jnp.minimum(s, n_tiles - 1) if ahead else jnp.maximum(s - 1, 0)
            return t // per_seq, t % per_seq, 0
        return pl.BlockSpec((None, tm, width), imap)

    def grouped(width):
        def imap(s):
            t = jnp.minimum(s, n_tiles - 1)
            return t // per_seq, 0, t % per_seq, 0
        return [nat(width, True) if dil == 1 else pl.BlockSpec((None, dil, tm // dil, width), imap)
                for dil in GROUP_DILATIONS]

    ws = list(weights)
    specs = ([nat(D_MODEL, False)] + grouped(GROUP_WIDTH) + grouped(HEAD_DIM) + [nat(D_MODEL, False)] * 3
             + [nat(PLE_DIM, False)] + [_resident(w.shape) for w in ws])
    return pl.pallas_call(
        functools.partial(_mix_mlp_kernel, dils=GROUP_DILATIONS, split=MIX_SPLIT),
        grid=(n_tiles + 1,),
        in_specs=specs,
        out_specs=nat(D_MODEL, False),
        out_shape=jax.ShapeDtypeStruct((bsz, seq, D_MODEL), F32),
        scratch_shapes=[pltpu.VMEM((N_GROUPS * (N_O_SLABS + 1), tm, LANES), F32),
                        pltpu.VMEM((tm, GROUP_WIDTH), BF16)],
        compiler_params=_cparams(1),
        name="mix_mlp_prompt",
    )(x, *os_, *ls, bout, sga, sgb, pe, *ws)


def _mix_mlp_sample(x, os_, ls, bout, sga, sgb, pe, weights):
    n = x.shape[0]
    tm = SAMPLE_TILE
    n_tiles = n // tm

    def rows(width, ahead):
        if ahead:
            return pl.BlockSpec((tm, width), lambda s: (jnp.minimum(s, n_tiles - 1), 0))
        return pl.BlockSpec((tm, width), lambda s: (jnp.maximum(s - 1, 0), 0))

    ws = list(weights)
    specs = ([rows(D_MODEL, False)] + [rows(GROUP_WIDTH, True)] * 3 + [rows(HEAD_DIM, True)] * 3
             + [rows(D_MODEL, False)] * 3 + [rows(PLE_DIM, False)] + [_resident(w.shape) for w in ws])
    return pl.pallas_call(
        functools.partial(_mix_mlp_kernel, dils=(1, 1, 1), split=1),
        grid=(n_tiles + 1,),
        in_specs=specs,
        out_specs=rows(D_MODEL, False),
        out_shape=jax.ShapeDtypeStruct((n, D_MODEL), F32),
        scratch_shapes=[pltpu.VMEM((N_GROUPS * (N_O_SLABS + 1), tm, LANES), F32),
                        pltpu.VMEM((tm, GROUP_WIDTH), BF16)],
        compiler_params=_cparams(1),
        name="mix_mlp_sample",
    )(x, *os_, *ls, bout, sga, sgb, pe, *ws)


def _t5_bucket(dist):
    max_exact = REL_BUCKETS // 2
    d = jnp.maximum(dist, 1).astype(F32)
    large = max_exact + (jnp.log(d / max_exact) / math.log(REL_MAX_DIST / max_exact)
                         * (REL_BUCKETS - max_exact)).astype(jnp.int32)
    large = jnp.minimum(large, REL_BUCKETS - 1)
    return jnp.where(dist < max_exact, dist, large)


def _slot_bias(rel_bias, g):
    dil = GROUP_DILATIONS[g]
    dist = dil * jnp.arange(WINDOW_SLOTS + 1, dtype=jnp.int32)
    hs = slice(g * HEADS_PER_GROUP, (g + 1) * HEADS_PER_GROUP)
    return rel_bias[_t5_bucket(dist)][:, hs].astype(F32)


def _band_bias(slot_bias):
    n_h = slot_bias.shape[1]
    pad = jnp.full((n_h, Q_BLOCK - 1), NEG_INF, F32)
    ext = jnp.concatenate([pad, slot_bias[::-1].T, pad, jnp.full((n_h, 1), NEG_INF, F32)], axis=1)
    width = ext.shape[1]
    skew = jnp.broadcast_to(ext[:, None, :], (n_h, Q_BLOCK, width)).reshape(n_h, Q_BLOCK * width)
    skew = skew[:, :Q_BLOCK * (width - 1)].reshape(n_h, Q_BLOCK, width - 1)
    return skew[:, :, Q_BLOCK - 1:3 * Q_BLOCK - 1]


def _pair_rows_table(per_step):
    pairs = [jnp.concatenate(per_step[p * T_PAIR:(p + 1) * T_PAIR], axis=-1) for p in range(len(per_step) // T_PAIR)]
    tbl = jnp.stack(pairs)
    return jnp.broadcast_to(tbl[..., None], tbl.shape + (HEAD_DIM,))


def _dec_bias(slot_biases, n_t):
    n_h = HEADS_PER_GROUP
    neg = lambda n: jnp.full((n, n_h), NEG_INF, F32)
    bc = []
    for g in range(N_GROUPS):
        near_first = slot_biases[g][::-1]
        if GROUP_DILATIONS[g] == 1:
            per_step = [jnp.concatenate([neg(t), near_first[:WINDOW_SLOTS - t]], axis=0) for t in range(n_t)]
        else:
            per_step = [near_first[:WINDOW_SLOTS]] * n_t
        bc.append(_pair_rows_table(per_step))
    sb = slot_biases[0]
    bn1 = _pair_rows_table([jnp.concatenate([sb[:t + 1][::-1], neg(n_t - 1 - t)], axis=0) for t in range(n_t)])
    own = jnp.stack([jnp.concatenate([s[:1]] * T_PAIR, axis=-1)[0] for s in slot_biases])
    bn = jnp.broadcast_to(own[..., None], own.shape + (HEAD_DIM,))
    return jnp.stack(bc), bn1, bn


def _gate_tiles(w):
    per = GATE_TILE // LRU_BLOCK
    w = w.reshape(LRU_WIDTH // GATE_TILE, per, LRU_BLOCK, LRU_BLOCK)
    eye = jnp.eye(per, dtype=w.dtype)
    return jnp.einsum('npij,pq->npiqj', w, eye).reshape(-1, GATE_TILE, GATE_TILE).astype(BF16)


def kernel(x_prompt, x_sample, p_prompt, p_sample, cache_kv1, cache_kv2, cache_kv3, state_conv, state_lru,
           rel_bias, g_mix, w_in, g_q, g_k, w_oa, w_conv, b_conv, w_rg_a, b_rg_a, w_rg_i, b_rg_i, lam,
           w_ob, w_o, g_mlp, w_up, w_down, g_ple, w_ple_gate, w_ple_in):
    depth = w_in.shape[0]
    assert depth == 1
    bsz, seq, _ = x_prompt.shape
    bd, n_t, _ = x_sample.shape
    gw = GROUP_WIDTH
    row = lambda a: a.reshape(1, -1)

    slot_biases = [_slot_bias(rel_bias, g) for g in range(N_GROUPS)]
    band = [_band_bias(sb) for sb in slot_biases]
    bias_cache, bias_new1, bias_new = _dec_bias(slot_biases, n_t)

    i = 0
    w_in_b = w_in[i].astype(BF16)
    gm, gq, gk = row(g_mix[i]), row(g_q[i]), row(g_k[i])
    lru_w = (w_conv[i], row(b_conv[i]), _gate_tiles(0.5 * w_rg_a[i]), row(b_rg_a[i]), _gate_tiles(0.5 * w_rg_i[i]),
             row(b_rg_i[i]), row(lam[i]), w_ob[i].astype(BF16))
    mlp_w = (w_oa[i].astype(BF16), w_o[i].astype(BF16), row(g_mlp[i]), w_up[i].astype(BF16),
             w_down[i].astype(BF16), row(g_ple[i]), w_ple_gate[i].astype(BF16), w_ple_in[i].astype(BF16))

    (q1, q2, q3, k1, k2, k3, v1, v2, v3, xb, gy, sga, sgb, kvt1, kvt2, kvt3, xbt) = _in_proj_prompt(
        x_prompt, gm, w_in_b, gq, gk)
    os_, ls = [], []
    for g, (qq, kk, vv) in enumerate(((q1, k1, v1), (q2, k2, v2), (q3, k3, v3))):
        dil = GROUP_DILATIONS[g]
        ns, ln = bsz * dil, seq // dil
        o, l = _swa(qq.reshape(ns, ln, gw), kk.reshape(ns, ln, gw), vv.reshape(ns, ln, gw), band[g])
        os_.append(o if dil == 1 else o.reshape(bsz, dil, ln, gw))
        ls.append(l if dil == 1 else l.reshape(bsz, dil, ln, HEAD_DIM))
    bout, lru_p = _lru_seq(xb, gy, *lru_w)
    y_prompt = _mix_mlp_prompt(x_prompt, os_, ls, bout, sga, sgb, p_prompt[i], mlp_w)
    kv_p = [t.reshape(bsz, GROUP_WINDOWS[g], 2, HEADS_PER_GROUP, HEAD_DIM)[None]
            for g, t in enumerate((kvt1, kvt2, kvt3))]
    conv_p = xbt[:, SUBLANES - (CONV_WIDTH - 1):][None]
    lru_p = lru_p[None]

    n_s = bd * n_t
    (sq1, sq2, sq3, _, _, _, _, _, _, sxb, sgy, ssga, ssgb, skv1, skv2, skv3, sxbt) = _in_proj_sample(
        x_sample.reshape(n_s, D_MODEL), gm, w_in_b, gq, gk)
    n_p = n_t // T_PAIR
    qv = jnp.stack([q.astype(F32).reshape(bd, n_p, KV_ROWS, HEAD_DIM) for q in (sq1, sq2, sq3)], axis=1)
    kv_new = jnp.stack([t.reshape(bd, n_t, 2, HEADS_PER_GROUP, HEAD_DIM) for t in (skv1, skv2, skv3)], axis=1)
    kv_new = jnp.moveaxis(kv_new, 3, 1)
    newp = kv_new.reshape(bd, 2, N_GROUPS, n_p, KV_ROWS, HEAD_DIM)
    newd = jnp.concatenate([kv_new[:, :, 0]] * T_PAIR, axis=-2)
    caches = [c.reshape((bd, WINDOW_SLOTS) + ((dil,) if dil > 1 else ()) + (2, HEADS_PER_GROUP, HEAD_DIM))
              for c, dil in zip((cache_kv1, cache_kv2, cache_kv3), GROUP_DILATIONS)]
    att_s = _dec_attn(qv, caches, newp, newd, bias_cache, bias_new1, bias_new).reshape(n_s, gw)

    def to_tb(a):
        return a.reshape(bd, n_t, -1).transpose(1, 0, 2).reshape(n_t * bd, -1)

    def from_tb(a):
        return a.reshape(n_t, bd, -1).transpose(1, 0, 2).reshape(bd * n_t, -1)

    conv0_s = state_conv[i].transpose(1, 0, 2).reshape((CONV_WIDTH - 1) * bd, LRU_WIDTH)
    bout_s, lru_s = _lru_step(to_tb(sxb), to_tb(sgy), conv0_s, state_lru[i].astype(F32), *lru_w, nb=bd, tl=n_t)
    zero_o = jnp.zeros((n_s, gw), BF16)
    lane = jnp.arange(HEAD_DIM)[None, :]
    den_one = jnp.where(lane < HEADS_PER_GROUP, 0.0, 1.0)
    st_on = jnp.broadcast_to(den_one, (n_s, HEAD_DIM)).astype(F32)
    st_off = jnp.broadcast_to(jnp.where(lane < HEADS_PER_GROUP, NEG_INF, den_one), (n_s, HEAD_DIM)).astype(F32)
    y_sample = _mix_mlp_sample(x_sample.reshape(n_s, D_MODEL), [att_s.astype(BF16), zero_o, zero_o],
                               [st_on, st_off, st_off], from_tb(bout_s), ssga, ssgb,
                               p_sample[i].reshape(n_s, PLE_DIM), mlp_w).reshape(bd, n_t, D_MODEL)
    kv_s = [t.reshape(bd, n_t, 2, HEADS_PER_GROUP, HEAD_DIM)[None] for t in (skv1, skv2, skv3)]
    xcat_s = jnp.concatenate([state_conv[i], sxbt.reshape(bd, n_t, LRU_WIDTH)], axis=1)
    conv_s = xcat_s[:, n_t:][None]

    return (y_prompt, y_sample, kv_p[0], kv_p[1], kv_p[2], conv_p, lru_p,
            kv_s[0], kv_s[1], kv_s[2], conv_s, lru_s[None])
```

```python
import functools
import math

import jax
import jax.numpy as jnp
from jax import lax
from jax.experimental import pallas as pl
from jax.experimental.pallas import tpu as pltpu

F32 = jnp.float32
BF16 = jnp.bfloat16

D_MODEL = 1024
HEAD_DIM = 128
HEADS_PER_GROUP = 4
GROUP_WINDOWS = (128, 512, 2048)
GROUP_DILATIONS = (1, 4, 16)
N_GROUPS = 3
GROUP_WIDTH = HEADS_PER_GROUP * HEAD_DIM
ATT_WIDTH = N_GROUPS * GROUP_WIDTH
ATT_SCALE = HEAD_DIM ** -0.5
LRU_WIDTH = D_MODEL
LRU_BLOCKS = 16
LRU_BLOCK = LRU_WIDTH // LRU_BLOCKS
CONV_WIDTH = 4
RG_C = 8.0
D_FF = 4 * D_MODEL
PLE_DIM = 256
REL_BUCKETS = 32
REL_MAX_DIST = 2048
NORM_EPS = 1e-6
NEG_INF = -1e30
WINDOW_SLOTS = 128
KV_ROWS = 2 * HEADS_PER_GROUP

OFF_Q, OFF_K, OFF_V = 0, ATT_WIDTH, 2 * ATT_WIDTH
OFF_XB = 3 * ATT_WIDTH
OFF_YB = OFF_XB + LRU_WIDTH
OFF_GA = OFF_YB + LRU_WIDTH
OFF_GB = OFF_GA + D_MODEL

SUBLANES = 8
LANES = 128
ROW_TILE = 512
SAMPLE_TILE = 256
Q_BLOCK = 128
SWA_ROWS = 4096
LRU_STEPS = 128
GATE_TILE = 256
VMEM_LIMIT = 58 * 1024 * 1024


def _cparams(n_axes):
    return pltpu.CompilerParams(dimension_semantics=("arbitrary",) * n_axes,
                                vmem_limit_bytes=VMEM_LIMIT)


def _resident(shape):
    nd = len(shape)
    return pl.BlockSpec(shape, lambda *_: (0,) * nd, pipeline_mode=pl.Buffered(1))


def _rms(x, gain):
    return x * lax.rsqrt(jnp.mean(x * x, axis=-1, keepdims=True) + NORM_EPS) * gain


def _dot(a, b):
    return jnp.dot(a, b, preferred_element_type=F32)


def _dot_nt(a, b):
    return lax.dot_general(a, b, (((1,), (1,)), ((), ())), preferred_element_type=F32)


def _in_proj_kernel(x_ref, gmix_ref, w_ref, gq_ref, gk_ref,
                    q1, q2, q3, k1, k2, k3, v1, v2, v3, xb_ref, gy_ref, sga_ref, sgb_ref,
                    kv1, kv2, kv3, xbt_ref, slab_ref, *, dils, split):
    q_refs, k_refs, v_refs, kv_refs = (q1, q2, q3), (k1, k2, k3), (v1, v2, v3), (kv1, kv2, kv3)
    tm = x_ref.shape[0]
    tr = tm // split
    gq = gq_ref[...] * ATT_SCALE
    gk = gk_ref[...]

    def chunk(c):
        r0 = c * tr
        h = _rms(x_ref[r0:r0 + tr, :], gmix_ref[...]).astype(BF16)

        def put(ref, hd, val, dil, slab):
            cs = slice(hd * HEAD_DIM, (hd + 1) * HEAD_DIM)
            if dil == 1:
                ref[r0:r0 + tr, cs] = val.astype(BF16)
                return
            slab_ref[slab, r0:r0 + tr, :] = val
            n = tr // dil
            if dil == 16:
                spare = slab - 3 * HEADS_PER_GROUP
                q = tr // 4
                for r4 in range(4):
                    slab_ref[spare, r0 + r4 * q:r0 + (r4 + 1) * q, :] = slab_ref[slab, pl.ds(r0 + r4, q, stride=4), :]
                for r in range(dil):
                    src = r0 + (r % 4) * q + r // 4
                    ref[r, c * n:(c + 1) * n, cs] = slab_ref[spare, pl.ds(src, n, stride=4), :].astype(BF16)
                return
            for r in range(dil):
                ref[r, c * n:(c + 1) * n, cs] = slab_ref[slab, pl.ds(r0 + r, n, stride=dil), :].astype(BF16)

        def put_tail(ref, row, val):
            n_tail = ref.shape[0] // KV_ROWS
            lo = max(r0, tm - n_tail)
            cnt = r0 + tr - lo
            if cnt > 0:
                ref[pl.ds((lo - (tm - n_tail)) * KV_ROWS + row, cnt, stride=KV_ROWS), :] = val[lo - r0:, :]

        def group(g):
            c0 = g * GROUP_WIDTH
            s0 = max(g - 1, 0) * 3 * HEADS_PER_GROUP
            yq = _dot(h, w_ref[:, OFF_Q + c0:OFF_Q + c0 + GROUP_WIDTH])
            yk = _dot(h, w_ref[:, OFF_K + c0:OFF_K + c0 + GROUP_WIDTH])
            yv = _dot(h, w_ref[:, OFF_V + c0:OFF_V + c0 + GROUP_WIDTH])
            for hd in range(HEADS_PER_GROUP):
                cs = slice(hd * HEAD_DIM, (hd + 1) * HEAD_DIM)
                put(q_refs[g], hd, _rms(yq[:, cs], gq), dils[g], s0 + hd)
                kn = _rms(yk[:, cs], gk)
                put(k_refs[g], hd, kn, dils[g], s0 + HEADS_PER_GROUP + hd)
                put(v_refs[g], hd, yv[:, cs], dils[g], s0 + 2 * HEADS_PER_GROUP + hd)
                put_tail(kv_refs[g], hd, kn)
                put_tail(kv_refs[g], HEADS_PER_GROUP + hd, yv[:, cs])

        def wide(off):
            return _dot(h, w_ref[:, off:off + D_MODEL])

        rs = slice(r0, r0 + tr)
        group(2)
        yield
        y = wide(OFF_XB)
        xb_ref[rs, :] = y.astype(BF16)
        n_x = xbt_ref.shape[0]
        if r0 + tr == tm:
            xbt_ref[...] = y[tr - n_x:, :]
        gy_ref[rs, :] = jax.nn.gelu(wide(OFF_YB)).astype(BF16)
        yield
        group(1)
        yield
        sga_ref[rs, :] = jax.nn.sigmoid(wide(OFF_GA)).astype(BF16)
        sgb_ref[rs, :] = jax.nn.sigmoid(wide(OFF_GB)).astype(BF16)
        yield
        group(0)

    chunks = [chunk(c) for c in range(split)]
    while chunks:
        chunks = [c for c in chunks if next(c, True) is None]


def _in_proj_prompt(x, g_mix, w_in, g_q, g_k):
    bsz, seq, _ = x.shape
    tm = ROW_TILE
    n_tiles = seq // tm
    gw = GROUP_WIDTH

    def nat(width):
        return pl.BlockSpec((None, tm, width), lambda b, i: (b, i, 0))

    def sds(shape, dt):
        return jax.ShapeDtypeStruct(shape, dt)

    qkv_shapes, qkv_specs = [], []
    for dil in GROUP_DILATIONS:
        if dil == 1:
            qkv_shapes.append(sds((bsz, seq, gw), BF16))
            qkv_specs.append(nat(gw))
        else:
            qkv_shapes.append(sds((bsz, dil, seq // dil, gw), BF16))
            qkv_specs.append(pl.BlockSpec((None, dil, tm // dil, gw), lambda b, i: (b, 0, i, 0)))
    tail_rows = [min(w, tm) for w in GROUP_WINDOWS]
    tail_start = [n_tiles - max(w // tm, 1) for w in GROUP_WINDOWS]
    tail_specs = [pl.BlockSpec((None, r * KV_ROWS, HEAD_DIM), lambda b, i, s=s: (b, jnp.maximum(i - s, 0), 0))
                  for r, s in zip(tail_rows, tail_start)]
    out_shape = (qkv_shapes * 3 + [sds((bsz, seq, LRU_WIDTH), BF16)] * 2 + [sds((bsz, seq, D_MODEL), BF16)] * 2
                 + [sds((bsz, w * KV_ROWS, HEAD_DIM), F32) for w in GROUP_WINDOWS]
                 + [sds((bsz, SUBLANES, LRU_WIDTH), F32)])
    out_specs = (qkv_specs * 3 + [nat(LRU_WIDTH)] * 2 + [nat(D_MODEL)] * 2 + tail_specs
                 + [pl.BlockSpec((None, SUBLANES, LRU_WIDTH), lambda b, i: (b, 0, 0))])
    return pl.pallas_call(
        functools.partial(_in_proj_kernel, dils=GROUP_DILATIONS, split=IN_SPLIT),
        grid=(bsz, n_tiles),
        in_specs=[nat(D_MODEL), _resident(g_mix.shape), _resident(w_in.shape),
                  _resident(g_q.shape), _resident(g_k.shape)],
        out_specs=out_specs,
        out_shape=out_shape,
        scratch_shapes=[pltpu.VMEM(((N_GROUPS - 1) * 3 * HEADS_PER_GROUP, tm, HEAD_DIM), F32)],
        compiler_params=_cparams(2),
        name="in_proj_prompt",
    )(x, g_mix, w_in, g_q, g_k)


def _in_proj_sample(x, g_mix, w_in, g_q, g_k):
    n = x.shape[0]
    tm = SAMPLE_TILE
    gw = GROUP_WIDTH

    def rows(width):
        return pl.BlockSpec((tm, width), lambda i: (i, 0))

    def sds(width, dt):
        return jax.ShapeDtypeStruct((n, width), dt)

    out_shape = ([sds(gw, BF16)] * 9 + [sds(LRU_WIDTH, BF16)] * 2 + [sds(D_MODEL, BF16)] * 2
                 + [jax.ShapeDtypeStruct((n * KV_ROWS, HEAD_DIM), F32)] * 3 + [sds(LRU_WIDTH, F32)])
    out_specs = ([rows(gw)] * 9 + [rows(LRU_WIDTH)] * 2 + [rows(D_MODEL)] * 2
                 + [pl.BlockSpec((tm * KV_ROWS, HEAD_DIM), lambda i: (i, 0))] * 3 + [rows(LRU_WIDTH)])
    return pl.pallas_call(
        functools.partial(_in_proj_kernel, dils=(1, 1, 1), split=1),
        grid=(n // tm,),
        in_specs=[rows(D_MODEL), _resident(g_mix.shape), _resident(w_in.shape),
                  _resident(g_q.shape), _resident(g_k.shape)],
        out_specs=out_specs,
        out_shape=out_shape,
        scratch_shapes=[pltpu.VMEM(((N_GROUPS - 1) * 3 * HEADS_PER_GROUP, tm, HEAD_DIM), F32)],
        compiler_params=_cparams(1),
        name="in_proj_sample",
    )(x, g_mix, w_in, g_q, g_k)


def _swa_kernel(q_ref, k_ref, v_ref, kp_ref, vp_ref, bias_ref, o_ref, st_ref):
    first = pl.program_id(1) == 0
    n_seq = q_ref.shape[0]
    n_blk = q_ref.shape[1] // Q_BLOCK
    col = lax.broadcasted_iota(jnp.int32, (Q_BLOCK, 2 * Q_BLOCK), 1)
    lane = lax.broadcasted_iota(jnp.int32, (Q_BLOCK, HEAD_DIM), 1)
    ones = jnp.ones((2 * Q_BLOCK, HEAD_DIM), BF16)
    for sq in range(n_seq):
        for j in range(n_blk):
            rq = slice(j * Q_BLOCK, (j + 1) * Q_BLOCK)
            st_tile = jnp.zeros((Q_BLOCK, HEAD_DIM), F32)
            for hd in range(HEADS_PER_GROUP):
                cs = slice(hd * HEAD_DIM, (hd + 1) * HEAD_DIM)
                q = q_ref[sq, rq, cs]
                if j == 0:
                    kk = jnp.concatenate([kp_ref[sq, :, cs], k_ref[sq, rq, cs]], axis=0)
                    vv = jnp.concatenate([vp_ref[sq, :, cs], v_ref[sq, rq, cs]], axis=0)
                else:
                    rk = slice((j - 1) * Q_BLOCK, (j + 1) * Q_BLOCK)
                    kk = k_ref[sq, rk, cs]
                    vv = v_ref[sq, rk, cs]
                s = _dot_nt(q, kk) + bias_ref[hd]
                if j == 0:
                    s = jnp.where(col < jnp.where(first, Q_BLOCK, 0), NEG_INF, s)
                m = jnp.max(s, axis=-1, keepdims=True)
                p = jnp.exp(s - m).astype(BF16)
                oe = _dot(p, jnp.concatenate([vv, ones], axis=1))
                o_ref[sq, rq, cs] = oe[:, :HEAD_DIM].astype(BF16)
                st_tile = jnp.where(lane == hd, m, st_tile)
                st_tile = jnp.where(lane == HEADS_PER_GROUP + hd, oe[:, HEAD_DIM:], st_tile)
            st_ref[sq, rq, :] = st_tile


def _swa(q, k, v, band_bias):
    n_seq, length, gw = q.shape
    tq = min(length, SWA_ROWS)
    sb = SWA_ROWS // tq
    ratio = tq // Q_BLOCK
    cur = pl.BlockSpec((sb, tq, gw), lambda s, i: (s, i, 0))
    prev = pl.BlockSpec((sb, Q_BLOCK, gw), lambda s, i: (s, jnp.maximum(i * ratio - 1, 0), 0))
    return pl.pallas_call(
        _swa_kernel,
        grid=(n_seq // sb, length // tq),
        in_specs=[cur, cur, cur, prev, prev, _resident(band_bias.shape)],
        out_specs=[cur, pl.BlockSpec((sb, tq, HEAD_DIM), lambda s, i: (s, i, 0))],
        out_shape=[jax.ShapeDtypeStruct((n_seq, length, gw), BF16),
                   jax.ShapeDtypeStruct((n_seq, length, HEAD_DIM), F32)],
        compiler_params=_cparams(2),
        name="swa",
    )(q, k, v, k, v, band_bias)


DEC_BATCH_TILE = 4
T_PAIR = 2


def _dec_attn_kernel(q_ref, k1, v1, k2, v2, k3, v3, newp_ref, newd_ref, bc_ref, bn1_ref, bn_ref, att_ref):
    k_refs, v_refs = (k1, k2, k3), (v1, v2, v3)
    n_b, _, n_p = q_ref.shape[:3]
    for bi in range(n_b):
        near_k = jnp.concatenate([k1[bi]] * T_PAIR, axis=1)
        near_v = jnp.concatenate([v1[bi]] * T_PAIR, axis=1)
        for tp in range(n_p):
            accs, tops, dens = [], [], []
            for g in range(N_GROUPS):
                qv = q_ref[bi, g, tp]
                if GROUP_DILATIONS[g] == 1:
                    kt, vt = near_k, near_v
                    kn, vn = newd_ref[bi, 0], newd_ref[bi, 1]
                    bn = bn1_ref[tp]
                else:
                    steps = slice(tp * T_PAIR, (tp + 1) * T_PAIR)
                    kt = k_refs[g][bi, :, steps, :, :].reshape(WINDOW_SLOTS, KV_ROWS, HEAD_DIM)
                    vt = v_refs[g][bi, :, steps, :, :].reshape(WINDOW_SLOTS, KV_ROWS, HEAD_DIM)
                    kn, vn = newp_ref[bi, 0, g, tp][None], newp_ref[bi, 1, g, tp][None]
                    bn = bn_ref[g][None]
                lc = jnp.sum(kt * qv[None], axis=-1, keepdims=True) + bc_ref[g, tp]
                ln = jnp.sum(kn * qv[None], axis=-1, keepdims=True) + bn
                m = jnp.maximum(jnp.max(lc, axis=0), jnp.max(ln, axis=0))
                pc = jnp.exp(lc - m[None])
                pn = jnp.exp(ln - m[None])
                dens.append(jnp.sum(pc, axis=0) + jnp.sum(pn, axis=0))
                accs.append(jnp.sum(pc * vt, axis=0) + jnp.sum(pn * vn, axis=0))
                tops.append(m)
            top = jnp.maximum(jnp.maximum(tops[0], tops[1]), tops[2])
            ws = [jnp.exp(m - top) for m in tops]
            tot = ws[0] * dens[0] + ws[1] * dens[1] + ws[2] * dens[2]
            out = (ws[0] / tot) * accs[0]
            for g in (1, 2):
                out = out + (ws[g] / tot) * accs[g]
            att_ref[bi, tp] = out


def _dec_attn(qv, caches, newp, newd, bias_cache, bias_new1, bias_new):
    bd, _, n_p = qv.shape[:3]
    bb = DEC_BATCH_TILE
    hd = (HEADS_PER_GROUP, HEAD_DIM)
    cspecs, cargs = [], []
    for g, dil in enumerate(GROUP_DILATIONS):
        for kv in range(2):
            if dil == 1:
                cspecs.append(pl.BlockSpec((bb, WINDOW_SLOTS, None) + hd, lambda i, kv=kv: (i, 0, kv, 0, 0)))
            else:
                cspecs.append(pl.BlockSpec((bb, WINDOW_SLOTS, n_p * T_PAIR, None) + hd,
                                           lambda i, kv=kv: (i, 0, 0, kv, 0, 0)))
            cargs.append(caches[g])

    def whole(a):
        return pl.BlockSpec((bb,) + a.shape[1:], lambda i: (i,) + (0,) * (a.ndim - 1))

    return pl.pallas_call(
        _dec_attn_kernel,
        grid=(bd // bb,),
        in_specs=[whole(qv)] + cspecs + [whole(newp), whole(newd), _resident(bias_cache.shape),
                                         _resident(bias_new1.shape), _resident(bias_new.shape)],
        out_specs=pl.BlockSpec((bb, n_p, KV_ROWS, HEAD_DIM), lambda i: (i, 0, 0, 0)),
        out_shape=jax.ShapeDtypeStruct((bd, n_p, KV_ROWS, HEAD_DIM), F32),
        compiler_params=_cparams(1),
        name="dec_attn",
    )(qv, *cargs, newp, newd, bias_cache, bias_new1, bias_new)


def _lru_gates(hx, cs, blk, wa_ref, ba_ref, wi_ref, bi_ref, lam_ref):
    hxb = hx.astype(BF16)
    tr = jnp.tanh(_dot(hxb, wa_ref[blk]) + 0.5 * ba_ref[:, cs])
    ti = jnp.tanh(_dot(hxb, wi_ref[blk]) + 0.5 * bi_ref[:, cs])
    lam = lam_ref[:, cs]
    log_sig = jnp.minimum(lam, 0.0) - jnp.log1p(jnp.exp(-jnp.abs(lam)))
    half_c = (0.5 * RG_C) * log_sig
    log_a = half_c * tr + half_c
    a = jnp.exp(log_a)
    y = jnp.tanh(log_a) * (-1.0 - a * a)
    root = jnp.where(y > 0.0, y * lax.rsqrt(y), 0.0)
    b = root * (hx * ti + hx)
    return a, b


def _lru_seq_kernel(xb_ref, gy_ref, wconv_ref, bconv_ref, wa_ref, ba_ref, wi_ref, bi_ref, lam_ref, wob_ref,
                    out_ref, hlast_ref, xcat, h_sl, hg, hc, *, n_tiles):
    nb, tl, c = xb_ref.shape
    rows = nb * tl
    halo = (CONV_WIDTH - 1) * nb
    n_sl = c // LANES
    per = GATE_TILE // LANES
    step = pl.program_id(0)

    @pl.when(step == 0)
    def _():
        xcat[:, 0:halo, :] = jnp.zeros((n_sl, halo, LANES), F32)
        hc[...] = jnp.zeros((nb, c), F32)
        hg[...] = jnp.zeros(hg.shape, BF16)

    for b in range(nb):
        xf = xb_ref[b].astype(F32)
        for s in range(n_sl):
            xcat[s, pl.ds(halo + b, tl, stride=nb), :] = xf[:, s * LANES:(s + 1) * LANES]
    h_end = []
    for blk in range(c // GATE_TILE):
        cs = slice(blk * GATE_TILE, (blk + 1) * GATE_TILE)
        out_ref[:, :, cs] = _dot(hg[...], wob_ref[:, cs]).astype(BF16).reshape(nb, tl, GATE_TILE)
        parts = []
        for s in range(blk * per, (blk + 1) * per):
            ls = slice(s * LANES, (s + 1) * LANES)
            xc = bconv_ref[:, ls] + wconv_ref[0:1, ls] * xcat[s, 0:rows, :]
            for j in range(1, CONV_WIDTH):
                xc = xc + wconv_ref[j:j + 1, ls] * xcat[s, j * nb:j * nb + rows, :]
            parts.append(xc)
        a, b = _lru_gates(jnp.concatenate(parts, axis=1), cs, blk, wa_ref, ba_ref, wi_ref, bi_ref, lam_ref)
        h = hc[:, cs]
        for t in range(tl):
            rs = slice(t * nb, (t + 1) * nb)
            h = a[rs, :] * h + b[rs, :]
            for k in range(per):
                h_sl[blk * per + k, rs, :] = h[:, k * LANES:(k + 1) * LANES]
        h_end.append(h)
    for s in range(n_sl):
        xcat[s, 0:halo, :] = xcat[s, rows:rows + halo, :]
    h = jnp.concatenate(h_end, axis=1)
    hc[...] = h

    @pl.when(step == n_tiles - 1)
    def _():
        hlast_ref[...] = h

    for b in range(nb):
        for s in range(n_sl):
            ls = slice(s * LANES, (s + 1) * LANES)
            hg[b * tl:(b + 1) * tl, ls] = h_sl[s, pl.ds(b, tl, stride=nb), :].astype(BF16) * gy_ref[b, :, ls]


def _lru_seq(xb, gy, w_conv, b_conv, wa, ba, wi, bi, lam, w_ob):
    bsz, seq, c = xb.shape
    tl = LRU_STEPS
    rows = bsz * tl
    n_tiles = seq // tl
    ahead = pl.BlockSpec((bsz, tl, c), lambda s: (0, jnp.minimum(s, n_tiles - 1), 0))
    consts = [w_conv, b_conv, wa, ba, wi, bi, lam, w_ob]
    return pl.pallas_call(
        functools.partial(_lru_seq_kernel, n_tiles=n_tiles),
        grid=(n_tiles + 1,),
        in_specs=[ahead, ahead] + [_resident(a.shape) for a in consts],
        out_specs=[pl.BlockSpec((bsz, tl, D_MODEL), lambda s: (0, jnp.maximum(s - 1, 0), 0)),
                   pl.BlockSpec((bsz, c), lambda s: (0, 0))],
        out_shape=[jax.ShapeDtypeStruct((bsz, seq, D_MODEL), BF16), jax.ShapeDtypeStruct((bsz, c), F32)],
        scratch_shapes=[pltpu.VMEM((c // LANES, rows + (CONV_WIDTH - 1) * bsz, LANES), F32),
                        pltpu.VMEM((c // LANES, rows, LANES), F32), pltpu.VMEM((rows, c), BF16),
                        pltpu.VMEM((bsz, c), F32)],
        compiler_params=_cparams(1),
        name="lru_seq",
    )(xb, gy, *consts)


def _lru_step_kernel(xb_ref, gy_ref, conv0_ref, h0_ref, wconv_ref, bconv_ref, wa_ref, ba_ref, wi_ref, bi_ref,
                     lam_ref, wob_ref, out_ref, hlast_ref, xcat, b_s, *, nb, tl):
    rows = nb * tl
    halo = (CONV_WIDTH - 1) * nb
    xcat[0:halo, :] = conv0_ref[...]
    xcat[halo:halo + rows, :] = xb_ref[...].astype(F32)
    xc = bconv_ref[...] + wconv_ref[0:1, :] * xcat[0:rows, :]
    for j in range(1, CONV_WIDTH):
        xc = xc + wconv_ref[j:j + 1, :] * xcat[j * nb:j * nb + rows, :]
    for blk in range(xc.shape[1] // GATE_TILE):
        cs = slice(blk * GATE_TILE, (blk + 1) * GATE_TILE)
        a, b = _lru_gates(xc[:, cs], cs, blk, wa_ref, ba_ref, wi_ref, bi_ref, lam_ref)
        h = h0_ref[:, cs]
        for t in range(tl):
            rs = slice(t * nb, (t + 1) * nb)
            h = a[rs, :] * h + b[rs, :]
            b_s[rs, cs] = h
        hlast_ref[:, cs] = h
    out_ref[...] = _dot((b_s[...] * gy_ref[...].astype(F32)).astype(BF16), wob_ref[...]).astype(BF16)


def _lru_step(xb, gy, conv0, h0, w_conv, b_conv, wa, ba, wi, bi, lam, w_ob, *, nb, tl):
    n_rows, c = xb.shape
    consts = [conv0, h0, w_conv, b_conv, wa, ba, wi, bi, lam, w_ob]
    full = lambda shape: pl.BlockSpec(shape, lambda i: (0,) * len(shape))
    return pl.pallas_call(
        functools.partial(_lru_step_kernel, nb=nb, tl=tl),
        grid=(1,),
        in_specs=[full(xb.shape), full(gy.shape)] + [full(a.shape) for a in consts],
        out_specs=[full((n_rows, D_MODEL)), full((nb, c))],
        out_shape=[jax.ShapeDtypeStruct((n_rows, D_MODEL), BF16), jax.ShapeDtypeStruct((nb, c), F32)],
        scratch_shapes=[pltpu.VMEM((n_rows + (CONV_WIDTH - 1) * nb, c), F32), pltpu.VMEM((n_rows, c), F32)],
        compiler_params=_cparams(1),
        name="lru_step",
    )(xb, gy, *consts)


FF_CHUNK = 1024
IN_SPLIT = 2
MIX_SPLIT = 2
N_O_SLABS = GROUP_WIDTH // LANES


def _mix_mlp_kernel(x_ref, o1, o2, o3, l1, l2, l3, bout_ref, sga_ref, sgb_ref, pe_ref,
                    woa_ref, wo_ref, gmlp_ref, wup_ref, wdown_ref, gple_ref, wpg_ref, wpe_ref, y_ref,
                    slab_ref, att_ref, *, dils, split):
    tm = x_ref.shape[0]

    @pl.when(pl.program_id(0) == 0)
    def _():
        att_ref[...] = jnp.zeros(att_ref.shape, BF16)

    att_prev = att_ref[...]

    def natural(ref, n_slabs, dil, base):
        if dil == 1:
            return lambda rs: [ref[rs, c * LANES:(c + 1) * LANES].astype(F32) for c in range(n_slabs)]

        def get(rs):
            n = (rs.stop - rs.start) // dil
            m0 = rs.start // dil
            for c in range(n_slabs):
                cols = slice(c * LANES, (c + 1) * LANES)
                if dil == 16:
                    spare = base + c - 2 * (N_O_SLABS + 1)
                    q = (rs.stop - rs.start) // 4
                    for r in range(dil):
                        dst = rs.start + (r % 4) * q + r // 4
                        slab_ref[spare, pl.ds(dst, n, stride=4), :] = ref[r, m0:m0 + n, cols].astype(F32)
                    for r4 in range(4):
                        slab_ref[base + c, pl.ds(rs.start + r4, q, stride=4), :] = (
                            slab_ref[spare, rs.start + r4 * q:rs.start + (r4 + 1) * q, :])
                else:
                    for r in range(dil):
                        slab_ref[base + c, pl.ds(rs.start + r, n, stride=dil), :] = ref[r, m0:m0 + n, cols].astype(F32)
            return [slab_ref[base + c, rs, :] for c in range(n_slabs)]

        return get

    o_get, l_get = [], []
    for g, (o_ref, l_ref) in enumerate(((o1, l1), (o2, l2), (o3, l3))):
        base = g * (N_O_SLABS + 1)
        o_get.append(natural(o_ref, N_O_SLABS, dils[g], base))
        l_get.append(natural(l_ref, 1, dils[g], base + N_O_SLABS))

    def next_att():
        n_rows = tm // split
        for part in range(split):
            rs = slice(part * n_rows, (part + 1) * n_rows)
            os_ = [get(rs) for get in o_get]
            sts = [get(rs)[0] for get in l_get]
            dens = [pltpu.roll(st, LANES - HEADS_PER_GROUP, axis=1) for st in sts]
            top = jnp.maximum(jnp.maximum(sts[0], sts[1]), sts[2])
            ws = [jnp.exp(st - top) for st in sts]
            tot = ws[0] * dens[0] + ws[1] * dens[1] + ws[2] * dens[2]
            ws = [w / tot for w in ws]
            for hd in range(HEADS_PER_GROUP):
                acc = ws[0][:, hd:hd + 1] * os_[0][hd]
                for g in (1, 2):
                    acc = acc + ws[g][:, hd:hd + 1] * os_[g][hd]
                att_ref[rs, hd * HEAD_DIM:(hd + 1) * HEAD_DIM] = acc.astype(BF16)
            yield

    def rows_body(rs):
        att = att_prev[rs, :]
        a_out = _dot(att, woa_ref[...])
        mix = (sga_ref[rs, :].astype(F32) * a_out
               + sgb_ref[rs, :].astype(F32) * bout_ref[rs, :].astype(F32))
        yield
        x = x_ref[rs, :] + _dot(mix.astype(BF16), wo_ref[...])
        h = _rms(x, gmlp_ref[...]).astype(BF16)
        yield
        acc = jnp.zeros(x.shape, F32)
        for c in range(D_FF // FF_CHUNK):
            cs = slice(c * FF_CHUNK, (c + 1) * FF_CHUNK)
            u = jnp.square(jnp.maximum(_dot(h, wup_ref[:, cs]), 0.0))
            acc = acc + _dot(u.astype(BF16), wdown_ref[cs, :])
            yield
        x = x + acc
        h = _rms(x, gple_ref[...]).astype(BF16)
        yield
        gate = jax.nn.sigmoid(_dot(h, wpg_ref[...]))
        y_ref[rs, :] = x + gate * _dot(pe_ref[rs, :].astype(BF16), wpe_ref[...])

    n_rows = tm // split
    chunks = [rows_body(slice(part * n_rows, (part + 1) * n_rows)) for part in range(split)] + [next_att()]
    while chunks:
        chunks = [c for c in chunks if next(c, True) is None]


def _mix_mlp_prompt(x, os_, ls, bout, sga, sgb, pe, weights):
    bsz, seq, _ = x.shape
    tm = ROW_TILE

    per_seq = seq // tm
    n_tiles = bsz * per_seq

    def nat(width, ahead):
        def imap(s):
            t = jnp.minimum(s, n_tiles - 1) if ahead else jnp.maximum(s - 1, 0)
            return t // per_seq, t % per_seq, 0
        return pl.BlockSpec((None, tm, width), imap)

    def grouped(width):
        def imap(s):
            t = jnp.minimum(s, n_tiles - 1)
            return t // per_seq, 0, t % per_seq, 0
        return [nat(width, True) if dil == 1 else pl.BlockSpec((None, dil, tm // dil, width), imap)
                for dil in GROUP_DILATIONS]

    ws = list(weights)
    specs = ([nat(D_MODEL, False)] + grouped(GROUP_WIDTH) + grouped(HEAD_DIM) + [nat(D_MODEL, False)] * 3
             + [nat(PLE_DIM, False)] + [_resident(w.shape) for w in ws])
    return pl.pallas_call(
        functools.partial(_mix_mlp_kernel, dils=GROUP_DILATIONS, split=MIX_SPLIT),
        grid=(n_tiles + 1,),
        in_specs=specs,
        out_specs=nat(D_MODEL, False),
        out_shape=jax.ShapeDtypeStruct((bsz, seq, D_MODEL), F32),
        scratch_shapes=[pltpu.VMEM((N_GROUPS * (N_O_SLABS + 1), tm, LANES), F32),
                        pltpu.VMEM((tm, GROUP_WIDTH), BF16)],
        compiler_params=_cparams(1),
        name="mix_mlp_prompt",
    )(x, *os_, *ls, bout, sga, sgb, pe, *ws)


def _mix_mlp_sample(x, os_, ls, bout, sga, sgb, pe, weights):
    n = x.shape[0]
    tm = SAMPLE_TILE
    n_tiles = n // tm

    def rows(width, ahead):
        if ahead:
            return pl.BlockSpec((tm, width), lambda s: (jnp.minimum(s, n_tiles - 1), 0))
        return pl.BlockSpec((tm, width), lambda s: (jnp.maximum(s - 1, 0), 0))

    ws = list(weights)
    specs = ([rows(D_MODEL, False)] + [rows(GROUP_WIDTH, True)] * 3 + [rows(HEAD_DIM, True)] * 3
             + [rows(D_MODEL, False)] * 3 + [rows(PLE_DIM, False)] + [_resident(w.shape) for w in ws])
    return pl.pallas_call(
        functools.partial(_mix_mlp_kernel, dils=(1, 1, 1), split=1),
        grid=(n_tiles + 1,),
        in_specs=specs,
        out_specs=rows(D_MODEL, False),
        out_shape=jax.ShapeDtypeStruct((n, D_MODEL), F32),
        scratch_shapes=[pltpu.VMEM((N_GROUPS * (N_O_SLABS + 1), tm, LANES), F32),
                        pltpu.VMEM((tm, GROUP_WIDTH), BF16)],
        compiler_params=_cparams(1),
        name="mix_mlp_sample",
    )(x, *os_, *ls, bout, sga, sgb, pe, *ws)


def _t5_bucket(dist):
    max_exact = REL_BUCKETS // 2
    d = jnp.maximum(dist, 1).astype(F32)
    large = max_exact + (jnp.log(d / max_exact) / math.log(REL_MAX_DIST / max_exact)
                         * (REL_BUCKETS - max_exact)).astype(jnp.int32)
    large = jnp.minimum(large, REL_BUCKETS - 1)
    return jnp.where(dist < max_exact, dist, large)


def _slot_bias(rel_bias, g):
    dil = GROUP_DILATIONS[g]
    dist = dil * jnp.arange(WINDOW_SLOTS + 1, dtype=jnp.int32)
    hs = slice(g * HEADS_PER_GROUP, (g + 1) * HEADS_PER_GROUP)
    return rel_bias[_t5_bucket(dist)][:, hs].astype(F32)


def _band_bias(slot_bias):
    n_h = slot_bias.shape[1]
    pad = jnp.full((n_h, Q_BLOCK - 1), NEG_INF, F32)
    ext = jnp.concatenate([pad, slot_bias[::-1].T, pad, jnp.full((n_h, 1), NEG_INF, F32)], axis=1)
    width = ext.shape[1]
    skew = jnp.broadcast_to(ext[:, None, :], (n_h, Q_BLOCK, width)).reshape(n_h, Q_BLOCK * width)
    skew = skew[:, :Q_BLOCK * (width - 1)].reshape(n_h, Q_BLOCK, width - 1)
    return skew[:, :, Q_BLOCK - 1:3 * Q_BLOCK - 1]


def _pair_rows_table(per_step):
    pairs = [jnp.concatenate(per_step[p * T_PAIR:(p + 1) * T_PAIR], axis=-1) for p in range(len(per_step) // T_PAIR)]
    tbl = jnp.stack(pairs)
    return jnp.broadcast_to(tbl[..., None], tbl.shape + (HEAD_DIM,))


def _dec_bias(slot_biases, n_t):
    n_h = HEADS_PER_GROUP
    neg = lambda n: jnp.full((n, n_h), NEG_INF, F32)
    bc = []
    for g in range(N_GROUPS):
        near_first = slot_biases[g][::-1]
        if GROUP_DILATIONS[g] == 1:
            per_step = [jnp.concatenate([neg(t), near_first[:WINDOW_SLOTS - t]], axis=0) for t in range(n_t)]
        else:
            per_step = [near_first[:WINDOW_SLOTS]] * n_t
        bc.append(_pair_rows_table(per_step))
    sb = slot_biases[0]
    bn1 = _pair_rows_table([jnp.concatenate([sb[:t + 1][::-1], neg(n_t - 1 - t)], axis=0) for t in range(n_t)])
    own = jnp.stack([jnp.concatenate([s[:1]] * T_PAIR, axis=-1)[0] for s in slot_biases])
    bn = jnp.broadcast_to(own[..., None], own.shape + (HEAD_DIM,))
    return jnp.stack(bc), bn1, bn


def _gate_tiles(w):
    per = GATE_TILE // LRU_BLOCK
    w = w.reshape(LRU_WIDTH // GATE_TILE, per, LRU_BLOCK, LRU_BLOCK)
    eye = jnp.eye(per, dtype=w.dtype)
    return jnp.einsum('npij,pq->npiqj', w, eye).reshape(-1, GATE_TILE, GATE_TILE).astype(BF16)


def kernel(x_prompt, x_sample, p_prompt, p_sample, cache_kv1, cache_kv2, cache_kv3, state_conv, state_lru,
           rel_bias, g_mix, w_in, g_q, g_k, w_oa, w_conv, b_conv, w_rg_a, b_rg_a, w_rg_i, b_rg_i, lam,
           w_ob, w_o, g_mlp, w_up, w_down, g_ple, w_ple_gate, w_ple_in):
    depth = w_in.shape[0]
    assert depth == 1
    bsz, seq, _ = x_prompt.shape
    bd, n_t, _ = x_sample.shape
    gw = GROUP_WIDTH
    row = lambda a: a.reshape(1, -1)

    slot_biases = [_slot_bias(rel_bias, g) for g in range(N_GROUPS)]
    band = [_band_bias(sb) for sb in slot_biases]
    bias_cache, bias_new1, bias_new = _dec_bias(slot_biases, n_t)

    i = 0
    w_in_b = w_in[i].astype(BF16)
    gm, gq, gk = row(g_mix[i]), row(g_q[i]), row(g_k[i])
    lru_w = (0.5 * w_conv[i], row(0.5 * b_conv[i]), _gate_tiles(w_rg_a[i]), row(b_rg_a[i]), _gate_tiles(w_rg_i[i]),
             row(b_rg_i[i]), row(lam[i]), w_ob[i].astype(BF16))
    mlp_w = (w_oa[i].astype(BF16), w_o[i].astype(BF16), row(g_mlp[i]), w_up[i].astype(BF16),
             w_down[i].astype(BF16), row(g_ple[i]), w_ple_gate[i].astype(BF16), w_ple_in[i].astype(BF16))

    (q1, q2, q3, k1, k2, k3, v1, v2, v3, xb, gy, sga, sgb, kvt1, kvt2, kvt3, xbt) = _in_proj_prompt(
        x_prompt, gm, w_in_b, gq, gk)
    os_, ls = [], []
    for g, (qq, kk, vv) in enumerate(((q1, k1, v1), (q2, k2, v2), (q3, k3, v3))):
        dil = GROUP_DILATIONS[g]
        ns, ln = bsz * dil, seq // dil
        o, l = _swa(qq.reshape(ns, ln, gw), kk.reshape(ns, ln, gw), vv.reshape(ns, ln, gw), band[g])
        os_.append(o if dil == 1 else o.reshape(bsz, dil, ln, gw))
        ls.append(l if dil == 1 else l.reshape(bsz, dil, ln, HEAD_DIM))
    bout, lru_p = _lru_seq(xb, gy, *lru_w)
    y_prompt = _mix_mlp_prompt(x_prompt, os_, ls, bout, sga, sgb, p_prompt[i], mlp_w)
    kv_p = [t.reshape(bsz, GROUP_WINDOWS[g], 2, HEADS_PER_GROUP, HEAD_DIM)[None]
            for g, t in enumerate((kvt1, kvt2, kvt3))]
    conv_p = xbt[:, SUBLANES - (CONV_WIDTH - 1):][None]
    lru_p = lru_p[None]

    n_s = bd * n_t
    (sq1, sq2, sq3, _, _, _, _, _, _, sxb, sgy, ssga, ssgb, skv1, skv2, skv3, sxbt) = _in_proj_sample(
        x_sample.reshape(n_s, D_MODEL), gm, w_in_b, gq, gk)
    n_p = n_t // T_PAIR
    qv = jnp.stack([q.astype(F32).reshape(bd, n_p, KV_ROWS, HEAD_DIM) for q in (sq1, sq2, sq3)], axis=1)
    kv_new = jnp.stack([t.reshape(bd, n_t, 2, HEADS_PER_GROUP, HEAD_DIM) for t in (skv1, skv2, skv3)], axis=1)
    kv_new = jnp.moveaxis(kv_new, 3, 1)
    newp = kv_new.reshape(bd, 2, N_GROUPS, n_p, KV_ROWS, HEAD_DIM)
    newd = jnp.concatenate([kv_new[:, :, 0]] * T_PAIR, axis=-2)
    caches = [c.reshape((bd, WINDOW_SLOTS) + ((dil,) if dil > 1 else ()) + (2, HEADS_PER_GROUP, HEAD_DIM))
              for c, dil in zip((cache_kv1, cache_kv2, cache_kv3), GROUP_DILATIONS)]
    att_s = _dec_attn(qv, caches, newp, newd, bias_cache, bias_new1, bias_new).reshape(n_s, gw)

    def to_tb(a):
        return a.reshape(bd, n_t, -1).transpose(1, 0, 2).reshape(n_t * bd, -1)

    def from_tb(a):
        return a.reshape(n_t, bd, -1).transpose(1, 0, 2).reshape(bd * n_t, -1)

    conv0_s = state_conv[i].transpose(1, 0, 2).reshape((CONV_WIDTH - 1) * bd, LRU_WIDTH)
    bout_s, lru_s = _lru_step(to_tb(sxb), to_tb(sgy), conv0_s, state_lru[i].astype(F32), *lru_w, nb=bd, tl=n_t)
    zero_o = jnp.zeros((n_s, gw), BF16)
    lane = jnp.arange(HEAD_DIM)[None, :]
    den_one = jnp.where(lane < HEADS_PER_GROUP, 0.0, 1.0)
    st_on = jnp.broadcast_to(den_one, (n_s, HEAD_DIM)).astype(F32)
    st_off = jnp.broadcast_to(jnp.where(lane < HEADS_PER_GROUP, NEG_INF, den_one), (n_s, HEAD_DIM)).astype(F32)
    y_sample = _mix_mlp_sample(x_sample.reshape(n_s, D_MODEL), [att_s.astype(BF16), zero_o, zero_o],
                               [st_on, st_off, st_off], from_tb(bout_s), ssga, ssgb,
                               p_sample[i].reshape(n_s, PLE_DIM), mlp_w).reshape(bd, n_t, D_MODEL)
    kv_s = [t.reshape(bd, n_t, 2, HEADS_PER_GROUP, HEAD_DIM)[None] for t in (skv1, skv2, skv3)]
    xcat_s = jnp.concatenate([state_conv[i], sxbt.reshape(bd, n_t, LRU_WIDTH)], axis=1)
    conv_s = xcat_s[:, n_t:][None]

    return (y_prompt, y_sample, kv_p[0], kv_p[1], kv_p[2], conv_p, lru_p,
            kv_s[0], kv_s[1], kv_s[2], conv_s, lru_s[None])
```

```python
import functools
import math

import jax
import jax.numpy as jnp
from jax import lax
from jax.experimental import pallas as pl
from jax.experimental.pallas import tpu as pltpu

F32 = jnp.float32
BF16 = jnp.bfloat16

D_MODEL = 1024
HEAD_DIM = 128
HEADS_PER_GROUP = 4
GROUP_WINDOWS = (128, 512, 2048)
GROUP_DILATIONS = (1, 4, 16)
N_GROUPS = 3
GROUP_WIDTH = HEADS_PER_GROUP * HEAD_DIM
ATT_WIDTH = N_GROUPS * GROUP_WIDTH
ATT_SCALE = HEAD_DIM ** -0.5
LRU_WIDTH = D_MODEL
LRU_BLOCKS = 16
LRU_BLOCK = LRU_WIDTH // LRU_BLOCKS
CONV_WIDTH = 4
RG_C = 8.0
D_FF = 4 * D_MODEL
PLE_DIM = 256
REL_BUCKETS = 32
REL_MAX_DIST = 2048
NORM_EPS = 1e-6
NEG_INF = -1e30
WINDOW_SLOTS = 128
KV_ROWS = 2 * HEADS_PER_GROUP

OFF_Q, OFF_K, OFF_V = 0, ATT_WIDTH, 2 * ATT_WIDTH
OFF_XB = 3 * ATT_WIDTH
OFF_YB = OFF_XB + LRU_WIDTH
OFF_GA = OFF_YB + LRU_WIDTH
OFF_GB = OFF_GA + D_MODEL

SUBLANES = 8
LANES = 128
ROW_TILE = 512
SAMPLE_TILE = 256
Q_BLOCK = 128
SWA_ROWS = 4096
LRU_STEPS = 128
GATE_TILE = 256
VMEM_LIMIT = 58 * 1024 * 1024


def _cparams(n_axes):
    return pltpu.CompilerParams(dimension_semantics=("arbitrary",) * n_axes,
                                vmem_limit_bytes=VMEM_LIMIT)


def _resident(shape):
    nd = len(shape)
    return pl.BlockSpec(shape, lambda *_: (0,) * nd, pipeline_mode=pl.Buffered(1))


def _rms(x, gain):
    return x * lax.rsqrt(jnp.mean(x * x, axis=-1, keepdims=True) + NORM_EPS) * gain


def _dot(a, b):
    return jnp.dot(a, b, preferred_element_type=F32)


def _sigmoid(z):
    return 0.5 * jnp.tanh(0.5 * z) + 0.5


def _dot_nt(a, b):
    return lax.dot_general(a, b, (((1,), (1,)), ((), ())), preferred_element_type=F32)


def _in_proj_kernel(x_ref, gmix_ref, w_ref, gq_ref, gk_ref,
                    q1, q2, q3, k1, k2, k3, v1, v2, v3, xb_ref, gy_ref, sga_ref, sgb_ref,
                    kv1, kv2, kv3, xbt_ref, slab_ref, *, dils, split):
    q_refs, k_refs, v_refs, kv_refs = (q1, q2, q3), (k1, k2, k3), (v1, v2, v3), (kv1, kv2, kv3)
    tm = x_ref.shape[0]
    tr = tm // split
    gq = gq_ref[...] * ATT_SCALE
    gk = gk_ref[...]

    def chunk(c):
        r0 = c * tr
        h = _rms(x_ref[r0:r0 + tr, :], gmix_ref[...]).astype(BF16)

        def put(ref, hd, val, dil, slab):
            cs = slice(hd * HEAD_DIM, (hd + 1) * HEAD_DIM)
            if dil == 1:
                ref[r0:r0 + tr, cs] = val.astype(BF16)
                return
            slab_ref[slab, r0:r0 + tr, :] = val
            n = tr // dil
            if dil == 16:
                spare = slab - 3 * HEADS_PER_GROUP
                q = tr // 4
                for r4 in range(4):
                    slab_ref[spare, r0 + r4 * q:r0 + (r4 + 1) * q, :] = slab_ref[slab, pl.ds(r0 + r4, q, stride=4), :]
                for r in range(dil):
                    src = r0 + (r % 4) * q + r // 4
                    ref[r, c * n:(c + 1) * n, cs] = slab_ref[spare, pl.ds(src, n, stride=4), :].astype(BF16)
                return
            for r in range(dil):
                ref[r, c * n:(c + 1) * n, cs] = slab_ref[slab, pl.ds(r0 + r, n, stride=dil), :].astype(BF16)

        def put_tail(ref, row, val):
            n_tail = ref.shape[0] // KV_ROWS
            lo = max(r0, tm - n_tail)
            cnt = r0 + tr - lo
            if cnt > 0:
                ref[pl.ds((lo - (tm - n_tail)) * KV_ROWS + row, cnt, stride=KV_ROWS), :] = val[lo - r0:, :]

        def group(g):
            c0 = g * GROUP_WIDTH
            s0 = max(g - 1, 0) * 3 * HEADS_PER_GROUP
            yq = _dot(h, w_ref[:, OFF_Q + c0:OFF_Q + c0 + GROUP_WIDTH])
            yk = _dot(h, w_ref[:, OFF_K + c0:OFF_K + c0 + GROUP_WIDTH])
            yv = _dot(h, w_ref[:, OFF_V + c0:OFF_V + c0 + GROUP_WIDTH])
            for hd in range(HEADS_PER_GROUP):
                cs = slice(hd * HEAD_DIM, (hd + 1) * HEAD_DIM)
                put(q_refs[g], hd, _rms(yq[:, cs], gq), dils[g], s0 + hd)
                kn = _rms(yk[:, cs], gk)
                put(k_refs[g], hd, kn, dils[g], s0 + HEADS_PER_GROUP + hd)
                put(v_refs[g], hd, yv[:, cs], dils[g], s0 + 2 * HEADS_PER_GROUP + hd)
                put_tail(kv_refs[g], hd, kn)
                put_tail(kv_refs[g], HEADS_PER_GROUP + hd, yv[:, cs])

        def wide(off):
            return _dot(h, w_ref[:, off:off + D_MODEL])

        rs = slice(r0, r0 + tr)
        group(2)
        yield
        y = wide(OFF_XB)
        xb_ref[rs, :] = y.astype(BF16)
        n_x = xbt_ref.shape[0]
        if r0 + tr == tm:
            xbt_ref[...] = y[tr - n_x:, :]
        gy_ref[rs, :] = jax.nn.gelu(wide(OFF_YB)).astype(BF16)
        yield
        group(1)
        yield
        sga_ref[rs, :] = _sigmoid(wide(OFF_GA)).astype(BF16)
        sgb_ref[rs, :] = _sigmoid(wide(OFF_GB)).astype(BF16)
        yield
        group(0)

    chunks = [chunk(c) for c in range(split)]
    while chunks:
        chunks = [c for c in chunks if next(c, True) is None]


def _in_proj_prompt(x, g_mix, w_in, g_q, g_k):
    bsz, seq, _ = x.shape
    tm = ROW_TILE
    n_tiles = seq // tm
    gw = GROUP_WIDTH

    def nat(width):
        return pl.BlockSpec((None, tm, width), lambda b, i: (b, i, 0))

    def sds(shape, dt):
        return jax.ShapeDtypeStruct(shape, dt)

    qkv_shapes, qkv_specs = [], []
    for dil in GROUP_DILATIONS:
        if dil == 1:
            qkv_shapes.append(sds((bsz, seq, gw), BF16))
            qkv_specs.append(nat(gw))
        else:
            qkv_shapes.append(sds((bsz, dil, seq // dil, gw), BF16))
            qkv_specs.append(pl.BlockSpec((None, dil, tm // dil, gw), lambda b, i: (b, 0, i, 0)))
    tail_rows = [min(w, tm) for w in GROUP_WINDOWS]
    tail_start = [n_tiles - max(w // tm, 1) for w in GROUP_WINDOWS]
    tail_specs = [pl.BlockSpec((None, r * KV_ROWS, HEAD_DIM), lambda b, i, s=s: (b, jnp.maximum(i - s, 0), 0))
                  for r, s in zip(tail_rows, tail_start)]
    out_shape = (qkv_shapes * 3 + [sds((bsz, seq, LRU_WIDTH), BF16)] * 2 + [sds((bsz, seq, D_MODEL), BF16)] * 2
                 + [sds((bsz, w * KV_ROWS, HEAD_DIM), F32) for w in GROUP_WINDOWS]
                 + [sds((bsz, SUBLANES, LRU_WIDTH), F32)])
    out_specs = (qkv_specs * 3 + [nat(LRU_WIDTH)] * 2 + [nat(D_MODEL)] * 2 + tail_specs
                 + [pl.BlockSpec((None, SUBLANES, LRU_WIDTH), lambda b, i: (b, 0, 0))])
    return pl.pallas_call(
        functools.partial(_in_proj_kernel, dils=GROUP_DILATIONS, split=IN_SPLIT),
        grid=(bsz, n_tiles),
        in_specs=[nat(D_MODEL), _resident(g_mix.shape), _resident(w_in.shape),
                  _resident(g_q.shape), _resident(g_k.shape)],
        out_specs=out_specs,
        out_shape=out_shape,
        scratch_shapes=[pltpu.VMEM(((N_GROUPS - 1) * 3 * HEADS_PER_GROUP, tm, HEAD_DIM), F32)],
        compiler_params=_cparams(2),
        name="in_proj_prompt",
    )(x, g_mix, w_in, g_q, g_k)


def _in_proj_sample(x, g_mix, w_in, g_q, g_k):
    n = x.shape[0]
    tm = SAMPLE_TILE
    gw = GROUP_WIDTH

    def rows(width):
        return pl.BlockSpec((tm, width), lambda i: (i, 0))

    def sds(width, dt):
        return jax.ShapeDtypeStruct((n, width), dt)

    out_shape = ([sds(gw, BF16)] * 9 + [sds(LRU_WIDTH, BF16)] * 2 + [sds(D_MODEL, BF16)] * 2
                 + [jax.ShapeDtypeStruct((n * KV_ROWS, HEAD_DIM), F32)] * 3 + [sds(LRU_WIDTH, F32)])
    out_specs = ([rows(gw)] * 9 + [rows(LRU_WIDTH)] * 2 + [rows(D_MODEL)] * 2
                 + [pl.BlockSpec((tm * KV_ROWS, HEAD_DIM), lambda i: (i, 0))] * 3 + [rows(LRU_WIDTH)])
    return pl.pallas_call(
        functools.partial(_in_proj_kernel, dils=(1, 1, 1), split=1),
        grid=(n // tm,),
        in_specs=[rows(D_MODEL), _resident(g_mix.shape), _resident(w_in.shape),
                  _resident(g_q.shape), _resident(g_k.shape)],
        out_specs=out_specs,
        out_shape=out_shape,
        scratch_shapes=[pltpu.VMEM(((N_GROUPS - 1) * 3 * HEADS_PER_GROUP, tm, HEAD_DIM), F32)],
        compiler_params=_cparams(1),
        name="in_proj_sample",
    )(x, g_mix, w_in, g_q, g_k)


def _swa_kernel(q_ref, k_ref, v_ref, kp_ref, vp_ref, bias_ref, o_ref, st_ref):
    first = pl.program_id(1) == 0
    n_seq = q_ref.shape[0]
    n_blk = q_ref.shape[1] // Q_BLOCK
    col = lax.broadcasted_iota(jnp.int32, (Q_BLOCK, 2 * Q_BLOCK), 1)
    lane = lax.broadcasted_iota(jnp.int32, (Q_BLOCK, HEAD_DIM), 1)
    ones = jnp.ones((2 * Q_BLOCK, HEAD_DIM), BF16)
    for sq in range(n_seq):
        for j in range(n_blk):
            rq = slice(j * Q_BLOCK, (j + 1) * Q_BLOCK)
            st_tile = jnp.zeros((Q_BLOCK, HEAD_DIM), F32)
            for hd in range(HEADS_PER_GROUP):
                cs = slice(hd * HEAD_DIM, (hd + 1) * HEAD_DIM)
                q = q_ref[sq, rq, cs]
                if j == 0:
                    kk = jnp.concatenate([kp_ref[sq, :, cs], k_ref[sq, rq, cs]], axis=0)
                    vv = jnp.concatenate([vp_ref[sq, :, cs], v_ref[sq, rq, cs]], axis=0)
                else:
                    rk = slice((j - 1) * Q_BLOCK, (j + 1) * Q_BLOCK)
                    kk = k_ref[sq, rk, cs]
                    vv = v_ref[sq, rk, cs]
                s = _dot_nt(q, kk) + bias_ref[hd]
                if j == 0:
                    s = jnp.where(col < jnp.where(first, Q_BLOCK, 0), NEG_INF, s)
                m = jnp.max(s, axis=-1, keepdims=True)
                p = jnp.exp(s - m).astype(BF16)
                oe = _dot(p, jnp.concatenate([vv, ones], axis=1))
                o_ref[sq, rq, cs] = oe[:, :HEAD_DIM].astype(BF16)
                st_tile = jnp.where(lane == hd, m, st_tile)
                st_tile = jnp.where(lane == HEADS_PER_GROUP + hd, oe[:, HEAD_DIM:], st_tile)
            st_ref[sq, rq, :] = st_tile


def _swa(q, k, v, band_bias):
    n_seq, length, gw = q.shape
    tq = min(length, SWA_ROWS)
    sb = SWA_ROWS // tq
    ratio = tq // Q_BLOCK
    cur = pl.BlockSpec((sb, tq, gw), lambda s, i: (s, i, 0))
    prev = pl.BlockSpec((sb, Q_BLOCK, gw), lambda s, i: (s, jnp.maximum(i * ratio - 1, 0), 0))
    return pl.pallas_call(
        _swa_kernel,
        grid=(n_seq // sb, length // tq),
        in_specs=[cur, cur, cur, prev, prev, _resident(band_bias.shape)],
        out_specs=[cur, pl.BlockSpec((sb, tq, HEAD_DIM), lambda s, i: (s, i, 0))],
        out_shape=[jax.ShapeDtypeStruct((n_seq, length, gw), BF16),
                   jax.ShapeDtypeStruct((n_seq, length, HEAD_DIM), F32)],
        compiler_params=_cparams(2),
        name="swa",
    )(q, k, v, k, v, band_bias)


DEC_BATCH_TILE = 4
T_PAIR = 2


def _dec_attn_kernel(q_ref, k1, v1, k2, v2, k3, v3, newp_ref, newd_ref, bc_ref, bn1_ref, bn_ref, att_ref):
    k_refs, v_refs = (k1, k2, k3), (v1, v2, v3)
    n_b, _, n_p = q_ref.shape[:3]
    for bi in range(n_b):
        near_k = jnp.concatenate([k1[bi]] * T_PAIR, axis=1)
        near_v = jnp.concatenate([v1[bi]] * T_PAIR, axis=1)
        for tp in range(n_p):
            accs, tops, dens = [], [], []
            for g in range(N_GROUPS):
                qv = q_ref[bi, g, tp]
                if GROUP_DILATIONS[g] == 1:
                    kt, vt = near_k, near_v
                    kn, vn = newd_ref[bi, 0], newd_ref[bi, 1]
                    bn = bn1_ref[tp]
                else:
                    steps = slice(tp * T_PAIR, (tp + 1) * T_PAIR)
                    kt = k_refs[g][bi, :, steps, :, :].reshape(WINDOW_SLOTS, KV_ROWS, HEAD_DIM)
                    vt = v_refs[g][bi, :, steps, :, :].reshape(WINDOW_SLOTS, KV_ROWS, HEAD_DIM)
                    kn, vn = newp_ref[bi, 0, g, tp][None], newp_ref[bi, 1, g, tp][None]
                    bn = bn_ref[g][None]
                lc = jnp.sum(kt * qv[None], axis=-1, keepdims=True) + bc_ref[g, tp]
                ln = jnp.sum(kn * qv[None], axis=-1, keepdims=True) + bn
                m = jnp.maximum(jnp.max(lc, axis=0), jnp.max(ln, axis=0))
                pc = jnp.exp(lc - m[None])
                pn = jnp.exp(ln - m[None])
                dens.append(jnp.sum(pc, axis=0) + jnp.sum(pn, axis=0))
                accs.append(jnp.sum(pc * vt, axis=0) + jnp.sum(pn * vn, axis=0))
                tops.append(m)
            top = jnp.maximum(jnp.maximum(tops[0], tops[1]), tops[2])
            ws = [jnp.exp(m - top) for m in tops]
            tot = ws[0] * dens[0] + ws[1] * dens[1] + ws[2] * dens[2]
            out = (ws[0] / tot) * accs[0]
            for g in (1, 2):
                out = out + (ws[g] / tot) * accs[g]
            att_ref[bi, tp] = out


def _dec_attn(qv, caches, newp, newd, bias_cache, bias_new1, bias_new):
    bd, _, n_p = qv.shape[:3]
    bb = DEC_BATCH_TILE
    hd = (HEADS_PER_GROUP, HEAD_DIM)
    cspecs, cargs = [], []
    for g, dil in enumerate(GROUP_DILATIONS):
        for kv in range(2):
            if dil == 1:
                cspecs.append(pl.BlockSpec((bb, WINDOW_SLOTS, None) + hd, lambda i, kv=kv: (i, 0, kv, 0, 0)))
            else:
                cspecs.append(pl.BlockSpec((bb, WINDOW_SLOTS, n_p * T_PAIR, None) + hd,
                                           lambda i, kv=kv: (i, 0, 0, kv, 0, 0)))
            cargs.append(caches[g])

    def whole(a):
        return pl.BlockSpec((bb,) + a.shape[1:], lambda i: (i,) + (0,) * (a.ndim - 1))

    return pl.pallas_call(
        _dec_attn_kernel,
        grid=(bd // bb,),
        in_specs=[whole(qv)] + cspecs + [whole(newp), whole(newd), _resident(bias_cache.shape),
                                         _resident(bias_new1.shape), _resident(bias_new.shape)],
        out_specs=pl.BlockSpec((bb, n_p, KV_ROWS, HEAD_DIM), lambda i: (i, 0, 0, 0)),
        out_shape=jax.ShapeDtypeStruct((bd, n_p, KV_ROWS, HEAD_DIM), F32),
        compiler_params=_cparams(1),
        name="dec_attn",
    )(qv, *cargs, newp, newd, bias_cache, bias_new1, bias_new)


def _lru_gates(hx, cs, blk, wa_ref, ba_ref, wi_ref, bi_ref, lam_ref):
    hxb = hx.astype(BF16)
    tr = jnp.tanh(_dot(hxb, wa_ref[blk]) + 0.5 * ba_ref[:, cs])
    ti = jnp.tanh(_dot(hxb, wi_ref[blk]) + 0.5 * bi_ref[:, cs])
    lam = lam_ref[:, cs]
    log_sig = jnp.minimum(lam, 0.0) - jnp.log1p(jnp.exp(-jnp.abs(lam)))
    half_c = (0.5 * RG_C) * log_sig
    log_a = half_c * tr + half_c
    a = jnp.exp(log_a)
    y = jnp.tanh(log_a) * (-1.0 - a * a)
    root = jnp.where(y > 0.0, y * lax.rsqrt(y), 0.0)
    b = root * (hx * ti + hx)
    return a, b


def _lru_seq_kernel(xb_ref, gy_ref, wconv_ref, bconv_ref, wa_ref, ba_ref, wi_ref, bi_ref, lam_ref, wob_ref,
                    out_ref, hlast_ref, xcat, h_sl, hg, hc, *, n_tiles):
    nb, tl, c = xb_ref.shape
    rows = nb * tl
    halo = (CONV_WIDTH - 1) * nb
    n_sl = c // LANES
    per = GATE_TILE // LANES
    step = pl.program_id(0)

    @pl.when(step == 0)
    def _():
        xcat[:, 0:halo, :] = jnp.zeros((n_sl, halo, LANES), F32)
        hc[...] = jnp.zeros((nb, c), F32)
        hg[...] = jnp.zeros(hg.shape, BF16)

    for b in range(nb):
        xf = xb_ref[b].astype(F32)
        for s in range(n_sl):
            xcat[s, pl.ds(halo + b, tl, stride=nb), :] = xf[:, s * LANES:(s + 1) * LANES]
    h_end = []
    for blk in range(c // GATE_TILE):
        cs = slice(blk * GATE_TILE, (blk + 1) * GATE_TILE)
        out_ref[:, :, cs] = _dot(hg[...], wob_ref[:, cs]).astype(BF16).reshape(nb, tl, GATE_TILE)
        parts = []
        for s in range(blk * per, (blk + 1) * per):
            ls = slice(s * LANES, (s + 1) * LANES)
            xc = bconv_ref[:, ls] + wconv_ref[0:1, ls] * xcat[s, 0:rows, :]
            for j in range(1, CONV_WIDTH):
                xc = xc + wconv_ref[j:j + 1, ls] * xcat[s, j * nb:j * nb + rows, :]
            parts.append(xc)
        a, b = _lru_gates(jnp.concatenate(parts, axis=1), cs, blk, wa_ref, ba_ref, wi_ref, bi_ref, lam_ref)
        h = hc[:, cs]
        for t in range(tl):
            rs = slice(t * nb, (t + 1) * nb)
            h = a[rs, :] * h + b[rs, :]
            for k in range(per):
                h_sl[blk * per + k, rs, :] = h[:, k * LANES:(k + 1) * LANES]
        h_end.append(h)
    for s in range(n_sl):
        xcat[s, 0:halo, :] = xcat[s, rows:rows + halo, :]
    h = jnp.concatenate(h_end, axis=1)
    hc[...] = h

    @pl.when(step == n_tiles - 1)
    def _():
        hlast_ref[...] = h

    for b in range(nb):
        for s in range(n_sl):
            ls = slice(s * LANES, (s + 1) * LANES)
            hg[b * tl:(b + 1) * tl, ls] = h_sl[s, pl.ds(b, tl, stride=nb), :].astype(BF16) * gy_ref[b, :, ls]


def _lru_seq(xb, gy, w_conv, b_conv, wa, ba, wi, bi, lam, w_ob):
    bsz, seq, c = xb.shape
    tl = LRU_STEPS
    rows = bsz * tl
    n_tiles = seq // tl
    ahead = pl.BlockSpec((bsz, tl, c), lambda s: (0, jnp.minimum(s, n_tiles - 1), 0))
    consts = [w_conv, b_conv, wa, ba, wi, bi, lam, w_ob]
    return pl.pallas_call(
        functools.partial(_lru_seq_kernel, n_tiles=n_tiles),
        grid=(n_tiles + 1,),
        in_specs=[ahead, ahead] + [_resident(a.shape) for a in consts],
        out_specs=[pl.BlockSpec((bsz, tl, D_MODEL), lambda s: (0, jnp.maximum(s - 1, 0), 0)),
                   pl.BlockSpec((bsz, c), lambda s: (0, 0))],
        out_shape=[jax.ShapeDtypeStruct((bsz, seq, D_MODEL), BF16), jax.ShapeDtypeStruct((bsz, c), F32)],
        scratch_shapes=[pltpu.VMEM((c // LANES, rows + (CONV_WIDTH - 1) * bsz, LANES), F32),
                        pltpu.VMEM((c // LANES, rows, LANES), F32), pltpu.VMEM((rows, c), BF16),
                        pltpu.VMEM((bsz, c), F32)],
        compiler_params=_cparams(1),
        name="lru_seq",
    )(xb, gy, *consts)


def _lru_step_kernel(xb_ref, gy_ref, conv0_ref, h0_ref, wconv_ref, bconv_ref, wa_ref, ba_ref, wi_ref, bi_ref,
                     lam_ref, wob_ref, out_ref, hlast_ref, xcat, b_s, *, nb, tl):
    rows = nb * tl
    halo = (CONV_WIDTH - 1) * nb
    xcat[0:halo, :] = conv0_ref[...]
    xcat[halo:halo + rows, :] = xb_ref[...].astype(F32)
    xc = bconv_ref[...] + wconv_ref[0:1, :] * xcat[0:rows, :]
    for j in range(1, CONV_WIDTH):
        xc = xc + wconv_ref[j:j + 1, :] * xcat[j * nb:j * nb + rows, :]
    for blk in range(xc.shape[1] // GATE_TILE):
        cs = slice(blk * GATE_TILE, (blk + 1) * GATE_TILE)
        a, b = _lru_gates(xc[:, cs], cs, blk, wa_ref, ba_ref, wi_ref, bi_ref, lam_ref)
        h = h0_ref[:, cs]
        for t in range(tl):
            rs = slice(t * nb, (t + 1) * nb)
            h = a[rs, :] * h + b[rs, :]
            b_s[rs, cs] = h
        hlast_ref[:, cs] = h
    out_ref[...] = _dot((b_s[...] * gy_ref[...].astype(F32)).astype(BF16), wob_ref[...]).astype(BF16)


def _lru_step(xb, gy, conv0, h0, w_conv, b_conv, wa, ba, wi, bi, lam, w_ob, *, nb, tl):
    n_rows, c = xb.shape
    consts = [conv0, h0, w_conv, b_conv, wa, ba, wi, bi, lam, w_ob]
    full = lambda shape: pl.BlockSpec(shape, lambda i: (0,) * len(shape))
    return pl.pallas_call(
        functools.partial(_lru_step_kernel, nb=nb, tl=tl),
        grid=(1,),
        in_specs=[full(xb.shape), full(gy.shape)] + [full(a.shape) for a in consts],
        out_specs=[full((n_rows, D_MODEL)), full((nb, c))],
        out_shape=[jax.ShapeDtypeStruct((n_rows, D_MODEL), BF16), jax.ShapeDtypeStruct((nb, c), F32)],
        scratch_shapes=[pltpu.VMEM((n_rows + (CONV_WIDTH - 1) * nb, c), F32), pltpu.VMEM((n_rows, c), F32)],
        compiler_params=_cparams(1),
        name="lru_step",
    )(xb, gy, *consts)


FF_CHUNK = 1024
IN_SPLIT = 2
MIX_SPLIT = 2
N_O_SLABS = GROUP_WIDTH // LANES


def _mix_mlp_kernel(x_ref, o1, o2, o3, l1, l2, l3, bout_ref, sga_ref, sgb_ref, pe_ref,
                    woa_ref, wo_ref, gmlp_ref, wup_ref, wdown_ref, gple_ref, wpg_ref, wpe_ref, y_ref,
                    slab_ref, att_ref, *, dils, split):
    tm = x_ref.shape[0]

    @pl.when(pl.program_id(0) == 0)
    def _():
        att_ref[...] = jnp.zeros(att_ref.shape, BF16)

    att_prev = att_ref[...]

    def natural(ref, n_slabs, dil, base):
        if dil == 1:
            return lambda rs: [ref[rs, c * LANES:(c + 1) * LANES].astype(F32) for c in range(n_slabs)]

        def get(rs):
            n = (rs.stop - rs.start) // dil
            m0 = rs.start // dil
            for c in range(n_slabs):
                cols = slice(c * LANES, (c + 1) * LANES)
                if dil == 16:
                    spare = base + c - 2 * (N_O_SLABS + 1)
                    q = (rs.stop - rs.start) // 4
                    for r in range(dil):
                        dst = rs.start + (r % 4) * q + r // 4
                        slab_ref[spare, pl.ds(dst, n, stride=4), :] = ref[r, m0:m0 + n, cols].astype(F32)
                    for r4 in range(4):
                        slab_ref[base + c, pl.ds(rs.start + r4, q, stride=4), :] = (
                            slab_ref[spare, rs.start + r4 * q:rs.start + (r4 + 1) * q, :])
                else:
                    for r in range(dil):
                        slab_ref[base + c, pl.ds(rs.start + r, n, stride=dil), :] = ref[r, m0:m0 + n, cols].astype(F32)
            return [slab_ref[base + c, rs, :] for c in range(n_slabs)]

        return get

    o_get, l_get = [], []
    for g, (o_ref, l_ref) in enumerate(((o1, l1), (o2, l2), (o3, l3))):
        base = g * (N_O_SLABS + 1)
        o_get.append(natural(o_ref, N_O_SLABS, dils[g], base))
        l_get.append(natural(l_ref, 1, dils[g], base + N_O_SLABS))

    def next_att():
        n_rows = tm // split
        for part in range(split):
            rs = slice(part * n_rows, (part + 1) * n_rows)
            os_ = [get(rs) for get in o_get]
            sts = [get(rs)[0] for get in l_get]
            dens = [pltpu.roll(st, LANES - HEADS_PER_GROUP, axis=1) for st in sts]
            top = jnp.maximum(jnp.maximum(sts[0], sts[1]), sts[2])
            ws = [jnp.exp(st - top) for st in sts]
            tot = ws[0] * dens[0] + ws[1] * dens[1] + ws[2] * dens[2]
            ws = [w / tot for w in ws]
            for hd in range(HEADS_PER_GROUP):
                acc = ws[0][:, hd:hd + 1] * os_[0][hd]
                for g in (1, 2):
                    acc = acc + ws[g][:, hd:hd + 1] * os_[g][hd]
                att_ref[rs, hd * HEAD_DIM:(hd + 1) * HEAD_DIM] = acc.astype(BF16)
            yield

    def rows_body(rs):
        att = att_prev[rs, :]
        a_out = _dot(att, woa_ref[...])
        mix = (sga_ref[rs, :].astype(F32) * a_out
               + sgb_ref[rs, :].astype(F32) * bout_ref[rs, :].astype(F32))
        yield
        x = x_ref[rs, :] + _dot(mix.astype(BF16), wo_ref[...])
        h = _rms(x, gmlp_ref[...]).astype(BF16)
        yield
        acc = jnp.zeros(x.shape, F32)
        for c in range(D_FF // FF_CHUNK):
            cs = slice(c * FF_CHUNK, (c + 1) * FF_CHUNK)
            u = jnp.square(jnp.maximum(_dot(h, wup_ref[:, cs]), 0.0))
            acc = acc + _dot(u.astype(BF16), wdown_ref[cs, :])
            yield
        x = x + acc
        h = _rms(x, gple_ref[...]).astype(BF16)
        yield
        gate = _sigmoid(_dot(h, wpg_ref[...]))
        y_ref[rs, :] = x + gate * _dot(pe_ref[rs, :].astype(BF16), wpe_ref[...])

    n_rows = tm // split
    chunks = [rows_body(slice(part * n_rows, (part + 1) * n_rows)) for part in range(split)] + [next_att()]
    while chunks:
        chunks = [c for c in chunks if next(c, True) is None]


def _mix_mlp_prompt(x, os_, ls, bout, sga, sgb, pe, weights):
    bsz, seq, _ = x.shape
    tm = ROW_TILE

    per_seq = seq // tm
    n_tiles = bsz * per_seq

    def nat(width, ahead):
        def imap(s):
            t = jnp.minimum(s, n_tiles - 1) if ahead else jnp.maximum(s - 1, 0)
            return t // per_seq, t % per_seq, 0
        return pl.BlockSpec((None, tm, width), imap)

    def grouped(width):
        def imap(s):
            t = jnp.minimum(s, n_tiles - 1)
            return t // per_seq, 0, t % per_seq, 0
        return [nat(width, True) if dil == 1 else pl.BlockSpec((None, dil, tm // dil, width), imap)
                for dil in GROUP_DILATIONS]

    ws = list(weights)
    specs = ([nat(D_MODEL, False)] + grouped(GROUP_WIDTH) + grouped(HEAD_DIM) + [nat(D_MODEL, False)] * 3
             + [nat(PLE_DIM, False)] + [_resident(w.shape) for w in ws])
    return pl.pallas_call(
        functools.partial(_mix_mlp_kernel, dils=GROUP_DILATIONS, split=MIX_SPLIT),
        grid=(n_tiles + 1,),
        in_specs=specs,
        out_specs=nat(D_MODEL, False),
        out_shape=jax.ShapeDtypeStruct((bsz, seq, D_MODEL), F32),
        scratch_shapes=[pltpu.VMEM((N_GROUPS * (N_O_SLABS + 1), tm, LANES), F32),
                        pltpu.VMEM((tm, GROUP_WIDTH), BF16)],
        compiler_params=_cparams(1),
        name="mix_mlp_prompt",
    )(x, *os_, *ls, bout, sga, sgb, pe, *ws)


def _mix_mlp_sample(x, os_, ls, bout, sga, sgb, pe, weights):
    n = x.shape[0]
    tm = SAMPLE_TILE
    n_tiles = n // tm

    def rows(width, ahead):
        if ahead:
            return pl.BlockSpec((tm, width), lambda s: (jnp.minimum(s, n_tiles - 1), 0))
        return pl.BlockSpec((tm, width), lambda s: (jnp.maximum(s - 1, 0), 0))

    ws = list(weights)
    specs = ([rows(D_MODEL, False)] + [rows(GROUP_WIDTH, True)] * 3 + [rows(HEAD_DIM, True)] * 3
             + [rows(D_MODEL, False)] * 3 + [rows(PLE_DIM, False)] + [_resident(w.shape) for w in ws])
    return pl.pallas_call(
        functools.partial(_mix_mlp_kernel, dils=(1, 1, 1), split=1),
        grid=(n_tiles + 1,),
        in_specs=specs,
        out_specs=rows(D_MODEL, False),
        out_shape=jax.ShapeDtypeStruct((n, D_MODEL), F32),
        scratch_shapes=[pltpu.VMEM((N_GROUPS * (N_O_SLABS + 1), tm, LANES), F32),
                        pltpu.VMEM((tm, GROUP_WIDTH), BF16)],
        compiler_params=_cparams(1),
        name="mix_mlp_sample",
    )(x, *os_, *ls, bout, sga, sgb, pe, *ws)


def _t5_bucket(dist):
    max_exact = REL_BUCKETS // 2
    d = jnp.maximum(dist, 1).astype(F32)
    large = max_exact + (jnp.log(d / max_exact) / math.log(REL_MAX_DIST / max_exact)
                         * (REL_BUCKETS - max_exact)).astype(jnp.int32)
    large = jnp.minimum(large, REL_BUCKETS - 1)
    return jnp.where(dist < max_exact, dist, large)


def _slot_bias(rel_bias, g):
    dil = GROUP_DILATIONS[g]
    dist = dil * jnp.arange(WINDOW_SLOTS + 1, dtype=jnp.int32)
    hs = slice(g * HEADS_PER_GROUP, (g + 1) * HEADS_PER_GROUP)
    return rel_bias[_t5_bucket(dist)][:, hs].astype(F32)


def _band_bias(slot_bias):
    n_h = slot_bias.shape[1]
    pad = jnp.full((n_h, Q_BLOCK - 1), NEG_INF, F32)
    ext = jnp.concatenate([pad, slot_bias[::-1].T, pad, jnp.full((n_h, 1), NEG_INF, F32)], axis=1)
    width = ext.shape[1]
    skew = jnp.broadcast_to(ext[:, None, :], (n_h, Q_BLOCK, width)).reshape(n_h, Q_BLOCK * width)
    skew = skew[:, :Q_BLOCK * (width - 1)].reshape(n_h, Q_BLOCK, width - 1)
    return skew[:, :, Q_BLOCK - 1:3 * Q_BLOCK - 1]


def _pair_rows_table(per_step):
    pairs = [jnp.concatenate(per_step[p * T_PAIR:(p + 1) * T_PAIR], axis=-1) for p in range(len(per_step) // T_PAIR)]
    tbl = jnp.stack(pairs)
    return jnp.broadcast_to(tbl[..., None], tbl.shape + (HEAD_DIM,))


def _dec_bias(slot_biases, n_t):
    n_h = HEADS_PER_GROUP
    neg = lambda n: jnp.full((n, n_h), NEG_INF, F32)
    bc = []
    for g in range(N_GROUPS):
        near_first = slot_biases[g][::-1]
        if GROUP_DILATIONS[g] == 1:
            per_step = [jnp.concatenate([neg(t), near_first[:WINDOW_SLOTS - t]], axis=0) for t in range(n_t)]
        else:
            per_step = [near_first[:WINDOW_SLOTS]] * n_t
        bc.append(_pair_rows_table(per_step))
    sb = slot_biases[0]
    bn1 = _pair_rows_table([jnp.concatenate([sb[:t + 1][::-1], neg(n_t - 1 - t)], axis=0) for t in range(n_t)])
    own = jnp.stack([jnp.concatenate([s[:1]] * T_PAIR, axis=-1)[0] for s in slot_biases])
    bn = jnp.broadcast_to(own[..., None], own.shape + (HEAD_DIM,))
    return jnp.stack(bc), bn1, bn


def _gate_tiles(w):
    per = GATE_TILE // LRU_BLOCK
    w = w.reshape(LRU_WIDTH // GATE_TILE, per, LRU_BLOCK, LRU_BLOCK)
    eye = jnp.eye(per, dtype=w.dtype)
    return jnp.einsum('npij,pq->npiqj', w, eye).reshape(-1, GATE_TILE, GATE_TILE).astype(BF16)


def kernel(x_prompt, x_sample, p_prompt, p_sample, cache_kv1, cache_kv2, cache_kv3, state_conv, state_lru,
           rel_bias, g_mix, w_in, g_q, g_k, w_oa, w_conv, b_conv, w_rg_a, b_rg_a, w_rg_i, b_rg_i, lam,
           w_ob, w_o, g_mlp, w_up, w_down, g_ple, w_ple_gate, w_ple_in):
    depth = w_in.shape[0]
    assert depth == 1
    bsz, seq, _ = x_prompt.shape
    bd, n_t, _ = x_sample.shape
    gw = GROUP_WIDTH
    row = lambda a: a.reshape(1, -1)

    slot_biases = [_slot_bias(rel_bias, g) for g in range(N_GROUPS)]
    band = [_band_bias(sb) for sb in slot_biases]
    bias_cache, bias_new1, bias_new = _dec_bias(slot_biases, n_t)

    i = 0
    w_in_b = w_in[i].astype(BF16)
    gm, gq, gk = row(g_mix[i]), row(g_q[i]), row(g_k[i])
    lru_w = (0.5 * w_conv[i], row(0.5 * b_conv[i]), _gate_tiles(w_rg_a[i]), row(b_rg_a[i]), _gate_tiles(w_rg_i[i]),
             row(b_rg_i[i]), row(lam[i]), w_ob[i].astype(BF16))
    mlp_w = (w_oa[i].astype(BF16), w_o[i].astype(BF16), row(g_mlp[i]), w_up[i].astype(BF16),
             w_down[i].astype(BF16), row(g_ple[i]), w_ple_gate[i].astype(BF16), w_ple_in[i].astype(BF16))

    (q1, q2, q3, k1, k2, k3, v1, v2, v3, xb, gy, sga, sgb, kvt1, kvt2, kvt3, xbt) = _in_proj_prompt(
        x_prompt, gm, w_in_b, gq, gk)
    os_, ls = [], []
    for g, (qq, kk, vv) in enumerate(((q1, k1, v1), (q2, k2, v2), (q3, k3, v3))):
        dil = GROUP_DILATIONS[g]
        ns, ln = bsz * dil, seq // dil
        o, l = _swa(qq.reshape(ns, ln, gw), kk.reshape(ns, ln, gw), vv.reshape(ns, ln, gw), band[g])
        os_.append(o if dil == 1 else o.reshape(bsz, dil, ln, gw))
        ls.append(l if dil == 1 else l.reshape(bsz, dil, ln, HEAD_DIM))
    bout, lru_p = _lru_seq(xb, gy, *lru_w)
    y_prompt = _mix_mlp_prompt(x_prompt, os_, ls, bout, sga, sgb, p_prompt[i], mlp_w)
    kv_p = [t.reshape(bsz, GROUP_WINDOWS[g], 2, HEADS_PER_GROUP, HEAD_DIM)[None]
            for g, t in enumerate((kvt1, kvt2, kvt3))]
    conv_p = xbt[:, SUBLANES - (CONV_WIDTH - 1):][None]
    lru_p = lru_p[None]

    n_s = bd * n_t
    (sq1, sq2, sq3, _, _, _, _, _, _, sxb, sgy, ssga, ssgb, skv1, skv2, skv3, sxbt) = _in_proj_sample(
        x_sample.reshape(n_s, D_MODEL), gm, w_in_b, gq, gk)
    n_p = n_t // T_PAIR
    qv = jnp.stack([q.astype(F32).reshape(bd, n_p, KV_ROWS, HEAD_DIM) for q in (sq1, sq2, sq3)], axis=1)
    kv_new = jnp.stack([t.reshape(bd, n_t, 2, HEADS_PER_GROUP, HEAD_DIM) for t in (skv1, skv2, skv3)], axis=1)
    kv_new = jnp.moveaxis(kv_new, 3, 1)
    newp = kv_new.reshape(bd, 2, N_GROUPS, n_p, KV_ROWS, HEAD_DIM)
    newd = jnp.concatenate([kv_new[:, :, 0]] * T_PAIR, axis=-2)
    caches = [c.reshape((bd, WINDOW_SLOTS) + ((dil,) if dil > 1 else ()) + (2, HEADS_PER_GROUP, HEAD_DIM))
              for c, dil in zip((cache_kv1, cache_kv2, cache_kv3), GROUP_DILATIONS)]
    att_s = _dec_attn(qv, caches, newp, newd, bias_cache, bias_new1, bias_new).reshape(n_s, gw)

    def to_tb(a):
        return a.reshape(bd, n_t, -1).transpose(1, 0, 2).reshape(n_t * bd, -1)

    def from_tb(a):
        return a.reshape(n_t, bd, -1).transpose(1, 0, 2).reshape(bd * n_t, -1)

    conv0_s = state_conv[i].transpose(1, 0, 2).reshape((CONV_WIDTH - 1) * bd, LRU_WIDTH)
    bout_s, lru_s = _lru_step(to_tb(sxb), to_tb(sgy), conv0_s, state_lru[i].astype(F32), *lru_w, nb=bd, tl=n_t)
    zero_o = jnp.zeros((n_s, gw), BF16)
    lane = jnp.arange(HEAD_DIM)[None, :]
    den_one = jnp.where(lane < HEADS_PER_GROUP, 0.0, 1.0)
    st_on = jnp.broadcast_to(den_one, (n_s, HEAD_DIM)).astype(F32)
    st_off = jnp.broadcast_to(jnp.where(lane < HEADS_PER_GROUP, NEG_INF, den_one), (n_s, HEAD_DIM)).astype(F32)
    y_sample = _mix_mlp_sample(x_sample.reshape(n_s, D_MODEL), [att_s.astype(BF16), zero_o, zero_o],
                               [st_on, st_off, st_off], from_tb(bout_s), ssga, ssgb,
                               p_sample[i].reshape(n_s, PLE_DIM), mlp_w).reshape(bd, n_t, D_MODEL)
    kv_s = [t.reshape(bd, n_t, 2, HEADS_PER_GROUP, HEAD_DIM)[None] for t in (skv1, skv2, skv3)]
    xcat_s = jnp.concatenate([state_conv[i], sxbt.reshape(bd, n_t, LRU_WIDTH)], axis=1)
    conv_s = xcat_s[:, n_t:][None]

    return (y_prompt, y_sample, kv_p[0], kv_p[1], kv_p[2], conv_p, lru_p,
            kv_s[0], kv_s[1], kv_s[2], conv_s, lru_s[None])
```

```python
import functools
import math

import jax
import jax.numpy as jnp
from jax import lax
from jax.experimental import pallas as pl
from jax.experimental.pallas import tpu as pltpu

F32 = jnp.float32
BF16 = jnp.bfloat16

D_MODEL = 1024
HEAD_DIM = 128
HEADS_PER_GROUP = 4
GROUP_WINDOWS = (128, 512, 2048)
GROUP_DILATIONS = (1, 4, 16)
N_GROUPS = 3
GROUP_WIDTH = HEADS_PER_GROUP * HEAD_DIM
ATT_WIDTH = N_GROUPS * GROUP_WIDTH
ATT_SCALE = HEAD_DIM ** -0.5
LRU_WIDTH = D_MODEL
LRU_BLOCKS = 16
LRU_BLOCK = LRU_WIDTH // LRU_BLOCKS
CONV_WIDTH = 4
RG_C = 8.0
D_FF = 4 * D_MODEL
PLE_DIM = 256
REL_BUCKETS = 32
REL_MAX_DIST = 2048
NORM_EPS = 1e-6
NEG_INF = -1e30
WINDOW_SLOTS = 128
KV_ROWS = 2 * HEADS_PER_GROUP

OFF_Q, OFF_K, OFF_V = 0, ATT_WIDTH, 2 * ATT_WIDTH
OFF_XB = 3 * ATT_WIDTH
OFF_YB = OFF_XB + LRU_WIDTH
OFF_GA = OFF_YB + LRU_WIDTH
OFF_GB = OFF_GA + D_MODEL

SUBLANES = 8
LANES = 128
ROW_TILE = 512
SAMPLE_TILE = 256
Q_BLOCK = 128
SWA_ROWS = 4096
LRU_STEPS = 128
GATE_TILE = 256
VMEM_LIMIT = 58 * 1024 * 1024


def _cparams(n_axes):
    return pltpu.CompilerParams(dimension_semantics=("arbitrary",) * n_axes,
                                vmem_limit_bytes=VMEM_LIMIT)


def _resident(shape):
    nd = len(shape)
    return pl.BlockSpec(shape, lambda *_: (0,) * nd, pipeline_mode=pl.Buffered(1))


def _rms(x, gain):
    return x * lax.rsqrt(jnp.mean(x * x, axis=-1, keepdims=True) + NORM_EPS) * gain


def _dot(a, b):
    return jnp.dot(a, b, preferred_element_type=F32)


def _dot_nt(a, b):
    return lax.dot_general(a, b, (((1,), (1,)), ((), ())), preferred_element_type=F32)


def _in_proj_kernel(x_ref, gmix_ref, w_ref, gq_ref, gk_ref,
                    q1, q2, q3, k1, k2, k3, v1, v2, v3, xb_ref, gy_ref, sga_ref, sgb_ref,
                    kv1, kv2, kv3, xbt_ref, slab_ref, *, dils, split):
    q_refs, k_refs, v_refs, kv_refs = (q1, q2, q3), (k1, k2, k3), (v1, v2, v3), (kv1, kv2, kv3)
    tm = x_ref.shape[0]
    tr = tm // split
    gq = gq_ref[...] * ATT_SCALE
    gk = gk_ref[...]

    def chunk(c):
        r0 = c * tr
        h = _rms(x_ref[r0:r0 + tr, :], gmix_ref[...]).astype(BF16)

        def put(ref, hd, val, dil, slab):
            cs = slice(hd * HEAD_DIM, (hd + 1) * HEAD_DIM)
            if dil == 1:
                ref[r0:r0 + tr, cs] = val.astype(BF16)
                return
            slab_ref[slab, r0:r0 + tr, :] = val
            n = tr // dil
            if dil == 16:
                spare = slab - 3 * HEADS_PER_GROUP
                q = tr // 4
                for r4 in range(4):
                    slab_ref[spare, r0 + r4 * q:r0 + (r4 + 1) * q, :] = slab_ref[slab, pl.ds(r0 + r4, q, stride=4), :]
                for r in range(dil):
                    src = r0 + (r % 4) * q + r // 4
                    ref[r, c * n:(c + 1) * n, cs] = slab_ref[spare, pl.ds(src, n, stride=4), :].astype(BF16)
                return
            for r in range(dil):
                ref[r, c * n:(c + 1) * n, cs] = slab_ref[slab, pl.ds(r0 + r, n, stride=dil), :].astype(BF16)

        def put_tail(ref, row, val):
            n_tail = ref.shape[0] // KV_ROWS
            lo = max(r0, tm - n_tail)
            cnt = r0 + tr - lo
            if cnt > 0:
                ref[pl.ds((lo - (tm - n_tail)) * KV_ROWS + row, cnt, stride=KV_ROWS), :] = val[lo - r0:, :]

        def group(g):
            c0 = g * GROUP_WIDTH
            s0 = max(g - 1, 0) * 3 * HEADS_PER_GROUP
            yq = _dot(h, w_ref[:, OFF_Q + c0:OFF_Q + c0 + GROUP_WIDTH])
            yk = _dot(h, w_ref[:, OFF_K + c0:OFF_K + c0 + GROUP_WIDTH])
            yv = _dot(h, w_ref[:, OFF_V + c0:OFF_V + c0 + GROUP_WIDTH])
            for hd in range(HEADS_PER_GROUP):
                cs = slice(hd * HEAD_DIM, (hd + 1) * HEAD_DIM)
                put(q_refs[g], hd, _rms(yq[:, cs], gq), dils[g], s0 + hd)
                kn = _rms(yk[:, cs], gk)
                put(k_refs[g], hd, kn, dils[g], s0 + HEADS_PER_GROUP + hd)
                put(v_refs[g], hd, yv[:, cs], dils[g], s0 + 2 * HEADS_PER_GROUP + hd)
                put_tail(kv_refs[g], hd, kn)
                put_tail(kv_refs[g], HEADS_PER_GROUP + hd, yv[:, cs])

        def wide(off):
            return _dot(h, w_ref[:, off:off + D_MODEL])

        rs = slice(r0, r0 + tr)
        group(2)
        yield
        y = wide(OFF_XB)
        xb_ref[rs, :] = y.astype(BF16)
        n_x = xbt_ref.shape[0]
        if r0 + tr == tm:
            xbt_ref[...] = y[tr - n_x:, :]
        gy_ref[rs, :] = jax.nn.gelu(wide(OFF_YB)).astype(BF16)
        yield
        group(1)
        yield
        sga_ref[rs, :] = jax.nn.sigmoid(wide(OFF_GA)).astype(BF16)
        sgb_ref[rs, :] = jax.nn.sigmoid(wide(OFF_GB)).astype(BF16)
        yield
        group(0)

    chunks = [chunk(c) for c in range(split)]
    while chunks:
        chunks = [c for c in chunks if next(c, True) is None]


def _in_proj_prompt(x, g_mix, w_in, g_q, g_k):
    bsz, seq, _ = x.shape
    tm = ROW_TILE
    n_tiles = seq // tm
    gw = GROUP_WIDTH

    def nat(width):
        return pl.BlockSpec((None, tm, width), lambda b, i: (b, i, 0))

    def sds(shape, dt):
        return jax.ShapeDtypeStruct(shape, dt)

    qkv_shapes, qkv_specs = [], []
    for dil in GROUP_DILATIONS:
        if dil == 1:
            qkv_shapes.append(sds((bsz, seq, gw), BF16))
            qkv_specs.append(nat(gw))
        else:
            qkv_shapes.append(sds((bsz, dil, seq // dil, gw), BF16))
            qkv_specs.append(pl.BlockSpec((None, dil, tm // dil, gw), lambda b, i: (b, 0, i, 0)))
    tail_rows = [min(w, tm) for w in GROUP_WINDOWS]
    tail_start = [n_tiles - max(w // tm, 1) for w in GROUP_WINDOWS]
    tail_specs = [pl.BlockSpec((None, r * KV_ROWS, HEAD_DIM), lambda b, i, s=s: (b, jnp.maximum(i - s, 0), 0))
                  for r, s in zip(tail_rows, tail_start)]
    out_shape = (qkv_shapes * 3 + [sds((bsz, seq, LRU_WIDTH), BF16)] * 2 + [sds((bsz, seq, D_MODEL), BF16)] * 2
                 + [sds((bsz, w * KV_ROWS, HEAD_DIM), F32) for w in GROUP_WINDOWS]
                 + [sds((bsz, SUBLANES, LRU_WIDTH), F32)])
    out_specs = (qkv_specs * 3 + [nat(LRU_WIDTH)] * 2 + [nat(D_MODEL)] * 2 + tail_specs
                 + [pl.BlockSpec((None, SUBLANES, LRU_WIDTH), lambda b, i: (b, 0, 0))])
    return pl.pallas_call(
        functools.partial(_in_proj_kernel, dils=GROUP_DILATIONS, split=IN_SPLIT),
        grid=(bsz, n_tiles),
        in_specs=[nat(D_MODEL), _resident(g_mix.shape), _resident(w_in.shape),
                  _resident(g_q.shape), _resident(g_k.shape)],
        out_specs=out_specs,
        out_shape=out_shape,
        scratch_shapes=[pltpu.VMEM(((N_GROUPS - 1) * 3 * HEADS_PER_GROUP, tm, HEAD_DIM), F32)],
        compiler_params=_cparams(2),
        name="in_proj_prompt",
    )(x, g_mix, w_in, g_q, g_k)


def _in_proj_sample(x, g_mix, w_in, g_q, g_k):
    n = x.shape[0]
    tm = SAMPLE_TILE
    gw = GROUP_WIDTH

    def rows(width):
        return pl.BlockSpec((tm, width), lambda i: (i, 0))

    def sds(width, dt):
        return jax.ShapeDtypeStruct((n, width), dt)

    out_shape = ([sds(gw, BF16)] * 9 + [sds(LRU_WIDTH, BF16)] * 2 + [sds(D_MODEL, BF16)] * 2
                 + [jax.ShapeDtypeStruct((n * KV_ROWS, HEAD_DIM), F32)] * 3 + [sds(LRU_WIDTH, F32)])
    out_specs = ([rows(gw)] * 9 + [rows(LRU_WIDTH)] * 2 + [rows(D_MODEL)] * 2
                 + [pl.BlockSpec((tm * KV_ROWS, HEAD_DIM), lambda i: (i, 0))] * 3 + [rows(LRU_WIDTH)])
    return pl.pallas_call(
        functools.partial(_in_proj_kernel, dils=(1, 1, 1), split=1),
        grid=(n // tm,),
        in_specs=[rows(D_MODEL), _resident(g_mix.shape), _resident(w_in.shape),
                  _resident(g_q.shape), _resident(g_k.shape)],
        out_specs=out_specs,
        out_shape=out_shape,
        scratch_shapes=[pltpu.VMEM(((N_GROUPS - 1) * 3 * HEADS_PER_GROUP, tm, HEAD_DIM), F32)],
        compiler_params=_cparams(1),
        name="in_proj_sample",
    )(x, g_mix, w_in, g_q, g_k)


def _swa_kernel(q_ref, k_ref, v_ref, kp_ref, vp_ref, bias_ref, o_ref, st_ref):
    first = pl.program_id(1) == 0
    n_seq = q_ref.shape[0]
    n_blk = q_ref.shape[1] // Q_BLOCK
    col = lax.broadcasted_iota(jnp.int32, (Q_BLOCK, 2 * Q_BLOCK), 1)
    lane = lax.broadcasted_iota(jnp.int32, (Q_BLOCK, HEAD_DIM), 1)
    ones = jnp.ones((2 * Q_BLOCK, HEAD_DIM), BF16)
    for sq in range(n_seq):
        for j in range(n_blk):
            rq = slice(j * Q_BLOCK, (j + 1) * Q_BLOCK)
            st_tile = jnp.zeros((Q_BLOCK, HEAD_DIM), F32)
            for hd in range(HEADS_PER_GROUP):
                cs = slice(hd * HEAD_DIM, (hd + 1) * HEAD_DIM)
                q = q_ref[sq, rq, cs]
                if j == 0:
                    kk = jnp.concatenate([kp_ref[sq, :, cs], k_ref[sq, rq, cs]], axis=0)
                    vv = jnp.concatenate([vp_ref[sq, :, cs], v_ref[sq, rq, cs]], axis=0)
                else:
                    rk = slice((j - 1) * Q_BLOCK, (j + 1) * Q_BLOCK)
                    kk = k_ref[sq, rk, cs]
                    vv = v_ref[sq, rk, cs]
                s = _dot_nt(q, kk) + bias_ref[hd]
                if j == 0:
                    s = jnp.where(col < jnp.where(first, Q_BLOCK, 0), NEG_INF, s)
                m = jnp.max(s, axis=-1, keepdims=True)
                p = jnp.exp(s - m).astype(BF16)
                oe = _dot(p, jnp.concatenate([vv, ones], axis=1))
                o_ref[sq, rq, cs] = oe[:, :HEAD_DIM].astype(BF16)
                st_tile = jnp.where(lane == hd, m, st_tile)
                st_tile = jnp.where(lane == HEADS_PER_GROUP + hd, oe[:, HEAD_DIM:], st_tile)
            st_ref[sq, rq, :] = st_tile


def _swa(q, k, v, band_bias):
    n_seq, length, gw = q.shape
    tq = min(length, SWA_ROWS)
    sb = SWA_ROWS // tq
    ratio = tq // Q_BLOCK
    cur = pl.BlockSpec((sb, tq, gw), lambda s, i: (s, i, 0))
    prev = pl.BlockSpec((sb, Q_BLOCK, gw), lambda s, i: (s, jnp.maximum(i * ratio - 1, 0), 0))
    return pl.pallas_call(
        _swa_kernel,
        grid=(n_seq // sb, length // tq),
        in_specs=[cur, cur, cur, prev, prev, _resident(band_bias.shape)],
        out_specs=[cur, pl.BlockSpec((sb, tq, HEAD_DIM), lambda s, i: (s, i, 0))],
        out_shape=[jax.ShapeDtypeStruct((n_seq, length, gw), BF16),
                   jax.ShapeDtypeStruct((n_seq, length, HEAD_DIM), F32)],
        compiler_params=_cparams(2),
        name="swa",
    )(q, k, v, k, v, band_bias)


DEC_BATCH_TILE = 4
T_PAIR = 2


def _dec_attn_kernel(q_ref, k1, v1, k2, v2, k3, v3, newp_ref, newd_ref, bc_ref, bn1_ref, bn_ref, att_ref):
    k_refs, v_refs = (k1, k2, k3), (v1, v2, v3)
    n_b, _, n_p = q_ref.shape[:3]
    for bi in range(n_b):
        near_k = jnp.concatenate([k1[bi]] * T_PAIR, axis=1)
        near_v = jnp.concatenate([v1[bi]] * T_PAIR, axis=1)
        for tp in range(n_p):
            accs, tops, dens = [], [], []
            for g in range(N_GROUPS):
                qv = q_ref[bi, g, tp]
                if GROUP_DILATIONS[g] == 1:
                    kt, vt = near_k, near_v
                    kn, vn = newd_ref[bi, 0], newd_ref[bi, 1]
                    bn = bn1_ref[tp]
                else:
                    steps = slice(tp * T_PAIR, (tp + 1) * T_PAIR)
                    kt = k_refs[g][bi, :, steps, :, :].reshape(WINDOW_SLOTS, KV_ROWS, HEAD_DIM)
                    vt = v_refs[g][bi, :, steps, :, :].reshape(WINDOW_SLOTS, KV_ROWS, HEAD_DIM)
                    kn, vn = newp_ref[bi, 0, g, tp][None], newp_ref[bi, 1, g, tp][None]
                    bn = bn_ref[g][None]
                lc = jnp.sum(kt * qv[None], axis=-1, keepdims=True) + bc_ref[g, tp]
                ln = jnp.sum(kn * qv[None], axis=-1, keepdims=True) + bn
                m = jnp.maximum(jnp.max(lc, axis=0), jnp.max(ln, axis=0))
                pc = jnp.exp(lc - m[None])
                pn = jnp.exp(ln - m[None])
                dens.append(jnp.sum(pc, axis=0) + jnp.sum(pn, axis=0))
                accs.append(jnp.sum(pc * vt, axis=0) + jnp.sum(pn * vn, axis=0))
                tops.append(m)
            top = jnp.maximum(jnp.maximum(tops[0], tops[1]), tops[2])
            ws = [jnp.exp(m - top) for m in tops]
            tot = ws[0] * dens[0] + ws[1] * dens[1] + ws[2] * dens[2]
            out = (ws[0] / tot) * accs[0]
            for g in (1, 2):
                out = out + (ws[g] / tot) * accs[g]
            att_ref[bi, tp] = out


def _dec_attn(qv, caches, newp, newd, bias_cache, bias_new1, bias_new):
    bd, _, n_p = qv.shape[:3]
    bb = DEC_BATCH_TILE
    hd = (HEADS_PER_GROUP, HEAD_DIM)
    cspecs, cargs = [], []
    for g, dil in enumerate(GROUP_DILATIONS):
        for kv in range(2):
            if dil == 1:
                cspecs.append(pl.BlockSpec((bb, WINDOW_SLOTS, None) + hd, lambda i, kv=kv: (i, 0, kv, 0, 0)))
            else:
                cspecs.append(pl.BlockSpec((bb, WINDOW_SLOTS, n_p * T_PAIR, None) + hd,
                                           lambda i, kv=kv: (i, 0, 0, kv, 0, 0)))
            cargs.append(caches[g])

    def whole(a):
        return pl.BlockSpec((bb,) + a.shape[1:], lambda i: (i,) + (0,) * (a.ndim - 1))

    return pl.pallas_call(
        _dec_attn_kernel,
        grid=(bd // bb,),
        in_specs=[whole(qv)] + cspecs + [whole(newp), whole(newd), _resident(bias_cache.shape),
                                         _resident(bias_new1.shape), _resident(bias_new.shape)],
        out_specs=pl.BlockSpec((bb, n_p, KV_ROWS, HEAD_DIM), lambda i: (i, 0, 0, 0)),
        out_shape=jax.ShapeDtypeStruct((bd, n_p, KV_ROWS, HEAD_DIM), F32),
        compiler_params=_cparams(1),
        name="dec_attn",
    )(qv, *cargs, newp, newd, bias_cache, bias_new1, bias_new)


def _lru_gates(hx, cs, blk, wa_ref, ba_ref, wi_ref, bi_ref, lam_ref):
    hxb = hx.astype(BF16)
    tr = jnp.tanh(_dot(hxb, wa_ref[blk]) + 0.5 * ba_ref[:, cs])
    ti = jnp.tanh(_dot(hxb, wi_ref[blk]) + 0.5 * bi_ref[:, cs])
    lam = lam_ref[:, cs]
    log_sig = jnp.minimum(lam, 0.0) - jnp.log1p(jnp.exp(-jnp.abs(lam)))
    half_c = (0.5 * RG_C) * log_sig
    log_a = half_c * tr + half_c
    a = jnp.exp(log_a)
    y = jnp.tanh(log_a) * (-1.0 - a * a)
    root = jnp.where(y > 0.0, y * lax.rsqrt(y), 0.0)
    b = root * (hx * ti + hx)
    return a, b


def _lru_seq_kernel(xb_ref, gy_ref, wconv_ref, bconv_ref, wa_ref, ba_ref, wi_ref, bi_ref, lam_ref, wob_ref,
                    out_ref, hlast_ref, xcat, h_sl, hg, hc, *, n_tiles):
    nb, tl, c = xb_ref.shape
    rows = nb * tl
    halo = (CONV_WIDTH - 1) * nb
    n_sl = c // LANES
    per = GATE_TILE // LANES
    step = pl.program_id(0)

    @pl.when(step == 0)
    def _():
        xcat[:, 0:halo, :] = jnp.zeros((n_sl, halo, LANES), F32)
        hc[...] = jnp.zeros((nb, c), F32)
        hg[...] = jnp.zeros(hg.shape, BF16)

    for b in range(nb):
        xf = xb_ref[b].astype(F32)
        for s in range(n_sl):
            xcat[s, pl.ds(halo + b, tl, stride=nb), :] = xf[:, s * LANES:(s + 1) * LANES]
    h_end = []
    for blk in range(c // GATE_TILE):
        cs = slice(blk * GATE_TILE, (blk + 1) * GATE_TILE)
        out_ref[:, :, cs] = _dot(hg[...], wob_ref[:, cs]).astype(BF16).reshape(nb, tl, GATE_TILE)
        parts = []
        for s in range(blk * per, (blk + 1) * per):
            ls = slice(s * LANES, (s + 1) * LANES)
            xc = bconv_ref[:, ls] + wconv_ref[0:1, ls] * xcat[s, 0:rows, :]
            for j in range(1, CONV_WIDTH):
                xc = xc + wconv_ref[j:j + 1, ls] * xcat[s, j * nb:j * nb + rows, :]
            parts.append(xc)
        a, b = _lru_gates(jnp.concatenate(parts, axis=1), cs, blk, wa_ref, ba_ref, wi_ref, bi_ref, lam_ref)
        h = hc[:, cs]
        for t in range(tl):
            rs = slice(t * nb, (t + 1) * nb)
            h = a[rs, :] * h + b[rs, :]
            for k in range(per):
                h_sl[blk * per + k, rs, :] = h[:, k * LANES:(k + 1) * LANES]
        h_end.append(h)
    for s in range(n_sl):
        xcat[s, 0:halo, :] = xcat[s, rows:rows + halo, :]
    h = jnp.concatenate(h_end, axis=1)
    hc[...] = h

    @pl.when(step == n_tiles - 1)
    def _():
        hlast_ref[...] = h

    for b in range(nb):
        for s in range(n_sl):
            ls = slice(s * LANES, (s + 1) * LANES)
            hg[b * tl:(b + 1) * tl, ls] = h_sl[s, pl.ds(b, tl, stride=nb), :].astype(BF16) * gy_ref[b, :, ls]


def _lru_seq(xb, gy, w_conv, b_conv, wa, ba, wi, bi, lam, w_ob):
    bsz, seq, c = xb.shape
    tl = LRU_STEPS
    rows = bsz * tl
    n_tiles = seq // tl
    ahead = pl.BlockSpec((bsz, tl, c), lambda s: (0, jnp.minimum(s, n_tiles - 1), 0))
    consts = [w_conv, b_conv, wa, ba, wi, bi, lam, w_ob]
    return pl.pallas_call(
        functools.partial(_lru_seq_kernel, n_tiles=n_tiles),
        grid=(n_tiles + 1,),
        in_specs=[ahead, ahead] + [_resident(a.shape) for a in consts],
        out_specs=[pl.BlockSpec((bsz, tl, D_MODEL), lambda s: (0, jnp.maximum(s - 1, 0), 0)),
                   pl.BlockSpec((bsz, c), lambda s: (0, 0))],
        out_shape=[jax.ShapeDtypeStruct((bsz, seq, D_MODEL), BF16), jax.ShapeDtypeStruct((bsz, c), F32)],
        scratch_shapes=[pltpu.VMEM((c // LANES, rows + (CONV_WIDTH - 1) * bsz, LANES), F32),
                        pltpu.VMEM((c // LANES, rows, LANES), F32), pltpu.VMEM((rows, c), BF16),
                        pltpu.VMEM((bsz, c), F32)],
        compiler_params=_cparams(1),
        name="lru_seq",
    )(xb, gy, *consts)


def _lru_step_kernel(xb_ref, gy_ref, conv0_ref, h0_ref, wconv_ref, bconv_ref, wa_ref, ba_ref, wi_ref, bi_ref,
                     lam_ref, wob_ref, out_ref, hlast_ref, xcat, b_s, *, nb, tl):
    rows = nb * tl
    halo = (CONV_WIDTH - 1) * nb
    xcat[0:halo, :] = conv0_ref[...]
    xcat[halo:halo + rows, :] = xb_ref[...].astype(F32)
    xc = bconv_ref[...] + wconv_ref[0:1, :] * xcat[0:rows, :]
    for j in range(1, CONV_WIDTH):
        xc = xc + wconv_ref[j:j + 1, :] * xcat[j * nb:j * nb + rows, :]
    for blk in range(xc.shape[1] // GATE_TILE):
        cs = slice(blk * GATE_TILE, (blk + 1) * GATE_TILE)
        a, b = _lru_gates(xc[:, cs], cs, blk, wa_ref, ba_ref, wi_ref, bi_ref, lam_ref)
        h = h0_ref[:, cs]
        for t in range(tl):
            rs = slice(t * nb, (t + 1) * nb)
            h = a[rs, :] * h + b[rs, :]
            b_s[rs, cs] = h
        hlast_ref[:, cs] = h
    out_ref[...] = _dot((b_s[...] * gy_ref[...].astype(F32)).astype(BF16), wob_ref[...]).astype(BF16)


def _lru_step(xb, gy, conv0, h0, w_conv, b_conv, wa, ba, wi, bi, lam, w_ob, *, nb, tl):
    n_rows, c = xb.shape
    consts = [conv0, h0, w_conv, b_conv, wa, ba, wi, bi, lam, w_ob]
    full = lambda shape: pl.BlockSpec(shape, lambda i: (0,) * len(shape))
    return pl.pallas_call(
        functools.partial(_lru_step_kernel, nb=nb, tl=tl),
        grid=(1,),
        in_specs=[full(xb.shape), full(gy.shape)] + [full(a.shape) for a in consts],
        out_specs=[full((n_rows, D_MODEL)), full((nb, c))],
        out_shape=[jax.ShapeDtypeStruct((n_rows, D_MODEL), BF16), jax.ShapeDtypeStruct((nb, c), F32)],
        scratch_shapes=[pltpu.VMEM((n_rows + (CONV_WIDTH - 1) * nb, c), F32), pltpu.VMEM((n_rows, c), F32)],
        compiler_params=_cparams(1),
        name="lru_step",
    )(xb, gy, *consts)


FF_CHUNK = 1024
IN_SPLIT = 2
MIX_SPLIT = 2
N_O_SLABS = GROUP_WIDTH // LANES


def _mix_mlp_kernel(x_ref, o1, o2, o3, l1, l2, l3, bout_ref, sga_ref, sgb_ref, pe_ref,
                    woa_ref, wo_ref, gmlp_ref, wup_ref, wdown_ref, gple_ref, wpg_ref, wpe_ref, y_ref,
                    slab_ref, att_ref, *, dils, split):
    tm = x_ref.shape[0]

    @pl.when(pl.program_id(0) == 0)
    def _():
        att_ref[...] = jnp.zeros(att_ref.shape, BF16)

    att_prev = att_ref[...]

    def natural(ref, n_slabs, dil, base):
        if dil == 1:
            return lambda rs: [ref[rs, c * LANES:(c + 1) * LANES].astype(F32) for c in range(n_slabs)]

        def get(rs):
            n = (rs.stop - rs.start) // dil
            m0 = rs.start // dil
            for c in range(n_slabs):
                cols = slice(c * LANES, (c + 1) * LANES)
                if dil == 16:
                    spare = base + c - 2 * (N_O_SLABS + 1)
                    q = (rs.stop - rs.start) // 4
                    for r in range(dil):
                        dst = rs.start + (r % 4) * q + r // 4
                        slab_ref[spare, pl.ds(dst, n, stride=4), :] = ref[r, m0:m0 + n, cols].astype(F32)
                    for r4 in range(4):
                        slab_ref[base + c, pl.ds(rs.start + r4, q, stride=4), :] = (
                            slab_ref[spare, rs.start + r4 * q:rs.start + (r4 + 1) * q, :])
                else:
                    for r in range(dil):
                        slab_ref[base + c, pl.ds(rs.start + r, n, stride=dil), :] = ref[r, m0:m0 + n, cols].astype(F32)
            return [slab_ref[base + c, rs, :] for c in range(n_slabs)]

        return get

    o_get, l_get = [], []
    for g, (o_ref, l_ref) in enumerate(((o1, l1), (o2, l2), (o3, l3))):
        base = g * (N_O_SLABS + 1)
        o_get.append(natural(o_ref, N_O_SLABS, dils[g], base))
        l_get.append(natural(l_ref, 1, dils[g], base + N_O_SLABS))

    def next_att():
        n_rows = tm // split
        for part in range(split):
            rs = slice(part * n_rows, (part + 1) * n_rows)
            os_ = [get(rs) for get in o_get]
            sts = [get(rs)[0] for get in l_get]
            dens = [pltpu.roll(st, LANES - HEADS_PER_GROUP, axis=1) for st in sts]
            top = jnp.maximum(jnp.maximum(sts[0], sts[1]), sts[2])
            ws = [jnp.exp(st - top) for st in sts]
            tot = ws[0] * dens[0] + ws[1] * dens[1] + ws[2] * dens[2]
            ws = [w / tot for w in ws]
            for hd in range(HEADS_PER_GROUP):
                acc = ws[0][:, hd:hd + 1] * os_[0][hd]
                for g in (1, 2):
                    acc = acc + ws[g][:, hd:hd + 1] * os_[g][hd]
                att_ref[rs, hd * HEAD_DIM:(hd + 1) * HEAD_DIM] = acc.astype(BF16)
            yield

    def rows_body(rs):
        att = att_prev[rs, :]
        a_out = _dot(att, woa_ref[...])
        mix = (sga_ref[rs, :].astype(F32) * a_out
               + sgb_ref[rs, :].astype(F32) * bout_ref[rs, :].astype(F32))
        yield
        x = x_ref[rs, :] + _dot(mix.astype(BF16), wo_ref[...])
        h = _rms(x, gmlp_ref[...]).astype(BF16)
        yield
        for c in range(D_FF // FF_CHUNK):
            cs = slice(c * FF_CHUNK, (c + 1) * FF_CHUNK)
            u = jnp.square(jnp.maximum(_dot(h, wup_ref[:, cs]), 0.0))
            x = x + _dot(u.astype(BF16), wdown_ref[cs, :])
            yield
        h = _rms(x, gple_ref[...]).astype(BF16)
        yield
        gate = jax.nn.sigmoid(_dot(h, wpg_ref[...]))
        y_ref[rs, :] = x + gate * _dot(pe_ref[rs, :].astype(BF16), wpe_ref[...])

    n_rows = tm // split
    chunks = [rows_body(slice(part * n_rows, (part + 1) * n_rows)) for part in range(split)] + [next_att()]
    while chunks:
        chunks = [c for c in chunks if next(c, True) is None]


def _mix_mlp_prompt(x, os_, ls, bout, sga, sgb, pe, weights):
    bsz, seq, _ = x.shape
    tm = ROW_TILE

    per_seq = seq // tm
    n_tiles = bsz * per_seq

    def nat(width, ahead):
        def imap(s):
            t = jnp.minimum(s, n_tiles - 1) if ahead else jnp.maximum(s - 1, 0)
            return t // per_seq, t % per_seq, 0
        return pl.BlockSpec((None, tm, width), imap)

    def grouped(width):
        def imap(s):
            t = jnp.minimum(s, n_tiles - 1)
            return t // per_seq, 0, t % per_seq, 0
        return [nat(width, True) if dil == 1 else pl.BlockSpec((None, dil, tm // dil, width), imap)
                for dil in GROUP_DILATIONS]

    ws = list(weights)
    specs = ([nat(D_MODEL, False)] + grouped(GROUP_WIDTH) + grouped(HEAD_DIM) + [nat(D_MODEL, False)] * 3
             + [nat(PLE_DIM, False)] + [_resident(w.shape) for w in ws])
    return pl.pallas_call(
        functools.partial(_mix_mlp_kernel, dils=GROUP_DILATIONS, split=MIX_SPLIT),
        grid=(n_tiles + 1,),
        in_specs=specs,
        out_specs=nat(D_MODEL, False),
        out_shape=jax.ShapeDtypeStruct((bsz, seq, D_MODEL), F32),
        scratch_shapes=[pltpu.VMEM((N_GROUPS * (N_O_SLABS + 1), tm, LANES), F32),
                        pltpu.VMEM((tm, GROUP_WIDTH), BF16)],
        compiler_params=_cparams(1),
        name="mix_mlp_prompt",
    )(x, *os_, *ls, bout, sga, sgb, pe, *ws)


def _mix_mlp_sample(x, os_, ls, bout, sga, sgb, pe, weights):
    n = x.shape[0]
    tm = SAMPLE_TILE
    n_tiles = n // tm

    def rows(width, ahead):
        if ahead:
            return pl.BlockSpec((tm, width), lambda s: (jnp.minimum(s, n_tiles - 1), 0))
        return pl.BlockSpec((tm, width), lambda s: (jnp.maximum(s - 1, 0), 0))

    ws = list(weights)
    specs = ([rows(D_MODEL, False)] + [rows(GROUP_WIDTH, True)] * 3 + [rows(HEAD_DIM, True)] * 3
             + [rows(D_MODEL, False)] * 3 + [rows(PLE_DIM, False)] + [_resident(w.shape) for w in ws])
    return pl.pallas_call(
        functools.partial(_mix_mlp_kernel, dils=(1, 1, 1), split=1),
        grid=(n_tiles + 1,),
        in_specs=specs,
        out_specs=rows(D_MODEL, False),
        out_shape=jax.ShapeDtypeStruct((n, D_MODEL), F32),
        scratch_shapes=[pltpu.VMEM((N_GROUPS * (N_O_SLABS + 1), tm, LANES), F32),
                        pltpu.VMEM((tm, GROUP_WIDTH), BF16)],
        compiler_params=_cparams(1),
        name="mix_mlp_sample",
    )(x, *os_, *ls, bout, sga, sgb, pe, *ws)


def _t5_bucket(dist):
    max_exact = REL_BUCKETS // 2
    d = jnp.maximum(dist, 1).astype(F32)
    large = max_exact + (jnp.log(d / max_exact) / math.log(REL_MAX_DIST / max_exact)
                         * (REL_BUCKETS - max_exact)).astype(jnp.int32)
    large = jnp.minimum(large, REL_BUCKETS - 1)
    return jnp.where(dist < max_exact, dist, large)


def _slot_bias(rel_bias, g):
    dil = GROUP_DILATIONS[g]
    dist = dil * jnp.arange(WINDOW_SLOTS + 1, dtype=jnp.int32)
    hs = slice(g * HEADS_PER_GROUP, (g + 1) * HEADS_PER_GROUP)
    return rel_bias[_t5_bucket(dist)][:, hs].astype(F32)


def _band_bias(slot_bias):
    n_h = slot_bias.shape[1]
    pad = jnp.full((n_h, Q_BLOCK - 1), NEG_INF, F32)
    ext = jnp.concatenate([pad, slot_bias[::-1].T, pad, jnp.full((n_h, 1), NEG_INF, F32)], axis=1)
    width = ext.shape[1]
    skew = jnp.broadcast_to(ext[:, None, :], (n_h, Q_BLOCK, width)).reshape(n_h, Q_BLOCK * width)
    skew = skew[:, :Q_BLOCK * (width - 1)].reshape(n_h, Q_BLOCK, width - 1)
    return skew[:, :, Q_BLOCK - 1:3 * Q_BLOCK - 1]


def _pair_rows_table(per_step):
    pairs = [jnp.concatenate(per_step[p * T_PAIR:(p + 1) * T_PAIR], axis=-1) for p in range(len(per_step) // T_PAIR)]
    tbl = jnp.stack(pairs)
    return jnp.broadcast_to(tbl[..., None], tbl.shape + (HEAD_DIM,))


def _dec_bias(slot_biases, n_t):
    n_h = HEADS_PER_GROUP
    neg = lambda n: jnp.full((n, n_h), NEG_INF, F32)
    bc = []
    for g in range(N_GROUPS):
        near_first = slot_biases[g][::-1]
        if GROUP_DILATIONS[g] == 1:
            per_step = [jnp.concatenate([neg(t), near_first[:WINDOW_SLOTS - t]], axis=0) for t in range(n_t)]
        else:
            per_step = [near_first[:WINDOW_SLOTS]] * n_t
        bc.append(_pair_rows_table(per_step))
    sb = slot_biases[0]
    bn1 = _pair_rows_table([jnp.concatenate([sb[:t + 1][::-1], neg(n_t - 1 - t)], axis=0) for t in range(n_t)])
    own = jnp.stack([jnp.concatenate([s[:1]] * T_PAIR, axis=-1)[0] for s in slot_biases])
    bn = jnp.broadcast_to(own[..., None], own.shape + (HEAD_DIM,))
    return jnp.stack(bc), bn1, bn


def _gate_tiles(w):
    per = GATE_TILE // LRU_BLOCK
    w = w.reshape(LRU_WIDTH // GATE_TILE, per, LRU_BLOCK, LRU_BLOCK)
    eye = jnp.eye(per, dtype=w.dtype)
    return jnp.einsum('npij,pq->npiqj', w, eye).reshape(-1, GATE_TILE, GATE_TILE).astype(BF16)


def kernel(x_prompt, x_sample, p_prompt, p_sample, cache_kv1, cache_kv2, cache_kv3, state_conv, state_lru,
           rel_bias, g_mix, w_in, g_q, g_k, w_oa, w_conv, b_conv, w_rg_a, b_rg_a, w_rg_i, b_rg_i, lam,
           w_ob, w_o, g_mlp, w_up, w_down, g_ple, w_ple_gate, w_ple_in):
    depth = w_in.shape[0]
    assert depth == 1
    bsz, seq, _ = x_prompt.shape
    bd, n_t, _ = x_sample.shape
    gw = GROUP_WIDTH
    row = lambda a: a.reshape(1, -1)

    slot_biases = [_slot_bias(rel_bias, g) for g in range(N_GROUPS)]
    band = [_band_bias(sb) for sb in slot_biases]
    bias_cache, bias_new1, bias_new = _dec_bias(slot_biases, n_t)

    i = 0
    w_in_b = w_in[i].astype(BF16)
    gm, gq, gk = row(g_mix[i]), row(g_q[i]), row(g_k[i])
    lru_w = (0.5 * w_conv[i], row(0.5 * b_conv[i]), _gate_tiles(w_rg_a[i]), row(b_rg_a[i]), _gate_tiles(w_rg_i[i]),
             row(b_rg_i[i]), row(lam[i]), w_ob[i].astype(BF16))
    mlp_w = (w_oa[i].astype(BF16), w_o[i].astype(BF16), row(g_mlp[i]), w_up[i].astype(BF16),
             w_down[i].astype(BF16), row(g_ple[i]), w_ple_gate[i].astype(BF16), w_ple_in[i].astype(BF16))

    (q1, q2, q3, k1, k2, k3, v1, v2, v3, xb, gy, sga, sgb, kvt1, kvt2, kvt3, xbt) = _in_proj_prompt(
        x_prompt, gm, w_in_b, gq, gk)
    os_, ls = [], []
    for g, (qq, kk, vv) in enumerate(((q1, k1, v1), (q2, k2, v2), (q3, k3, v3))):
        dil = GROUP_DILATIONS[g]
        ns, ln = bsz * dil, seq // dil
        o, l = _swa(qq.reshape(ns, ln, gw), kk.reshape(ns, ln, gw), vv.reshape(ns, ln, gw), band[g])
        os_.append(o if dil == 1 else o.reshape(bsz, dil, ln, gw))
        ls.append(l if dil == 1 else l.reshape(bsz, dil, ln, HEAD_DIM))
    bout, lru_p = _lru_seq(xb, gy, *lru_w)
    y_prompt = _mix_mlp_prompt(x_prompt, os_, ls, bout, sga, sgb, p_prompt[i], mlp_w)
    kv_p = [t.reshape(bsz, GROUP_WINDOWS[g], 2, HEADS_PER_GROUP, HEAD_DIM)[None]
            for g, t in enumerate((kvt1, kvt2, kvt3))]
    conv_p = xbt[:, SUBLANES - (CONV_WIDTH - 1):][None]
    lru_p = lru_p[None]

    n_s = bd * n_t
    (sq1, sq2, sq3, _, _, _, _, _, _, sxb, sgy, ssga, ssgb, skv1, skv2, skv3, sxbt) = _in_proj_sample(
        x_sample.reshape(n_s, D_MODEL), gm, w_in_b, gq, gk)
    n_p = n_t // T_PAIR
    qv = jnp.stack([q.astype(F32).reshape(bd, n_p, KV_ROWS, HEAD_DIM) for q in (sq1, sq2, sq3)], axis=1)
    kv_new = jnp.stack([t.reshape(bd, n_t, 2, HEADS_PER_GROUP, HEAD_DIM) for t in (skv1, skv2, skv3)], axis=1)
    kv_new = jnp.moveaxis(kv_new, 3, 1)
    newp = kv_new.reshape(bd, 2, N_GROUPS, n_p, KV_ROWS, HEAD_DIM)
    newd = jnp.concatenate([kv_new[:, :, 0]] * T_PAIR, axis=-2)
    caches = [c.reshape((bd, WINDOW_SLOTS) + ((dil,) if dil > 1 else ()) + (2, HEADS_PER_GROUP, HEAD_DIM))
              for c, dil in zip((cache_kv1, cache_kv2, cache_kv3), GROUP_DILATIONS)]
    att_s = _dec_attn(qv, caches, newp, newd, bias_cache, bias_new1, bias_new).reshape(n_s, gw)

    def to_tb(a):
        return a.reshape(bd, n_t, -1).transpose(1, 0, 2).reshape(n_t * bd, -1)

    def from_tb(a):
        return a.reshape(n_t, bd, -1).transpose(1, 0, 2).reshape(bd * n_t, -1)

    conv0_s = state_conv[i].transpose(1, 0, 2).reshape((CONV_WIDTH - 1) * bd, LRU_WIDTH)
    bout_s, lru_s = _lru_step(to_tb(sxb), to_tb(sgy), conv0_s, state_lru[i].astype(F32), *lru_w, nb=bd, tl=n_t)
    zero_o = jnp.zeros((n_s, gw), BF16)
    lane = jnp.arange(HEAD_DIM)[None, :]
    den_one = jnp.where(lane < HEADS_PER_GROUP, 0.0, 1.0)
    st_on = jnp.broadcast_to(den_one, (n_s, HEAD_DIM)).astype(F32)
    st_off = jnp.broadcast_to(jnp.where(lane < HEADS_PER_GROUP, NEG_INF, den_one), (n_s, HEAD_DIM)).astype(F32)
    y_sample = _mix_mlp_sample(x_sample.reshape(n_s, D_MODEL), [att_s.astype(BF16), zero_o, zero_o],
                               [st_on, st_off, st_off], from_tb(bout_s), ssga, ssgb,
                               p_sample[i].reshape(n_s, PLE_DIM), mlp_w).reshape(bd, n_t, D_MODEL)
    kv_s = [t.reshape(bd, n_t, 2, HEADS_PER_GROUP, HEAD_DIM)[None] for t in (skv1, skv2, skv3)]
    xcat_s = jnp.concatenate([state_conv[i], sxbt.reshape(bd, n_t, LRU_WIDTH)], axis=1)
    conv_s = xcat_s[:, n_t:][None]

    return (y_prompt, y_sample, kv_p[0], kv_p[1], kv_p[2], conv_p, lru_p,
            kv_s[0], kv_s[1], kv_s[2], conv_s, lru_s[None])
```
